```python
import math
import jax, jax.numpy as jnp
from jax import lax
import numpy as np

D_MODEL = 1024
BATCH = 1
SEQ = 16384
DEPTH = 1

D_MIX = D_MODEL
D_ATTN = D_MIX // 2
D_POOL = D_MIX - D_ATTN
HEAD_DIM = 64
N_Q_HEADS = D_ATTN // HEAD_DIM
N_KV_HEADS = 2
Q_PER_KV = N_Q_HEADS // N_KV_HEADS
D_KV = N_KV_HEADS * HEAD_DIM
WINDOW = 128
BLOCK = 128
N_BUCKETS = 32
MAX_DISTANCE = 128
POOL_WINDOWS = (2, 4, 8, 16)
N_POOL_GROUPS = len(POOL_WINDOWS)
POOL_GROUP = D_POOL // N_POOL_GROUPS
D_IN = 2 * D_ATTN + 2 * D_KV + 2 * D_POOL
EPS = 1e-6
NEG = -1e30

kernel_name = "hymba_bidir_swa_pool_hybrid"


def rmsnorm(x, g):
    xf = x.astype(jnp.float32)
    y = xf * lax.rsqrt(jnp.mean(xf * xf, axis=-1, keepdims=True) + EPS)
    return (y * g.astype(jnp.float32)).astype(x.dtype)


def t5_bucket(rel):
    nb = N_BUCKETS // 2
    max_exact = nb // 2
    ret = jnp.where(rel > 0, nb, 0)
    n = jnp.abs(rel)
    nf = jnp.maximum(n, 1).astype(jnp.float32)
    large = max_exact + (jnp.log(nf / max_exact) / math.log(MAX_DISTANCE / max_exact)
                         * (nb - max_exact)).astype(jnp.int32)
    large = jnp.minimum(large, nb - 1)
    return ret + jnp.where(n < max_exact, n, large)


def banded_attention(q, k, v, rel_bias, sink):
    B, S = q.shape[0], q.shape[1]
    nb = S // BLOCK
    qb = q.reshape(B, nb, BLOCK, N_KV_HEADS, Q_PER_KV, HEAD_DIM)
    pad = ((0, 0), (BLOCK, BLOCK), (0, 0), (0, 0))
    kp = jnp.pad(k, pad).reshape(B, nb + 2, BLOCK, N_KV_HEADS, HEAD_DIM)
    vp = jnp.pad(v, pad).reshape(B, nb + 2, BLOCK, N_KV_HEADS, HEAD_DIM)
    kw = jnp.concatenate([kp[:, :-2], kp[:, 1:-1], kp[:, 2:]], axis=2)
    vw = jnp.concatenate([vp[:, :-2], vp[:, 1:-1], vp[:, 2:]], axis=2)

    scale = HEAD_DIM ** -0.5
    s = jnp.einsum('bnqkgd,bnskd->bnkgqs', qb, kw).astype(jnp.float32) * scale

    qi = jnp.arange(BLOCK)[:, None]
    kj = jnp.arange(3 * BLOCK)[None, :]
    rel = (kj - BLOCK) - qi
    bias = rel_bias.astype(jnp.float32)[t5_bucket(rel)]
    bias = jnp.transpose(bias, (2, 0, 1)).reshape(N_KV_HEADS, Q_PER_KV, BLOCK, 3 * BLOCK)
    key_pos = jnp.arange(nb)[:, None] * BLOCK - BLOCK + jnp.arange(3 * BLOCK)[None, :]
    valid = (jnp.abs(rel) <= WINDOW)[None] & ((key_pos >= 0) & (key_pos < S))[:, None, :]
    s = jnp.where(valid[None, :, None, None], s + bias[None, None], NEG)

    sk = sink.astype(jnp.float32).reshape(1, 1, N_KV_HEADS, Q_PER_KV, 1, 1)
    m = jnp.maximum(jnp.max(s, axis=-1, keepdims=True), sk)
    p = jnp.exp(s - m)
    p = p / (jnp.sum(p, axis=-1, keepdims=True) + jnp.exp(sk - m))
    o = jnp.einsum('bnkgqs,bnskd->bnqkgd', p.astype(v.dtype), vw)
    return o.reshape(B, S, N_Q_HEADS * HEAD_DIM)


def multiscale_pool(u, pool_w, pool_scale):
    B, S, _ = u.shape
    uf = u.astype(jnp.float32)
    c = jnp.concatenate([jnp.zeros((B, 1, D_POOL), jnp.float32), jnp.cumsum(uf, axis=1)], axis=1)
    t = jnp.arange(S)
    outs = []
    for gi, w in enumerate(POOL_WINDOWS):
        left = w // 2
        right = w - 1 - left
        lo = jnp.maximum(t - left, 0)
        hi = jnp.minimum(t + right + 1, S)
        sl = slice(gi * POOL_GROUP, (gi + 1) * POOL_GROUP)
        cg = c[..., sl]
        mean = (cg[:, hi] - cg[:, lo]) / (hi - lo).astype(jnp.float32)[None, :, None]
        outs.append(mean - uf[..., sl])
    y = jnp.stack(outs, axis=2)
    y = jnp.einsum('bsgc,gcd->bsgd', y, pool_w.astype(jnp.float32)).reshape(B, S, D_POOL)
    return (y * pool_scale.astype(jnp.float32)).astype(u.dtype)


def setup_inputs(seed: int = 0) -> dict:
    key = jax.random.key(seed)
    ks = jax.random.split(key, 10)
    f32 = jnp.float32
    x = jax.random.normal(ks[0], (BATCH, SEQ, D_MODEL), f32)
    pre_norm_g = 1.0 + 0.05 * jax.random.normal(ks[1], (DEPTH, D_MODEL), f32)
    w_in = jax.random.normal(ks[2], (DEPTH, D_MODEL, D_IN), f32) * D_MODEL ** -0.5
    rel_bias = 0.5 * jax.random.normal(ks[3], (N_BUCKETS, N_Q_HEADS), f32)
    attn_sink = 0.5 * jax.random.normal(ks[4], (DEPTH, N_Q_HEADS), f32)
    pool_w = jax.random.normal(ks[5], (DEPTH, N_POOL_GROUPS, POOL_GROUP, POOL_GROUP), f32) * POOL_GROUP ** -0.5
    pool_scale = 1.0 + 0.1 * jax.random.normal(ks[6], (DEPTH, D_POOL), f32)
    w_out = jax.random.normal(ks[7], (DEPTH, D_MIX, D_MODEL), f32) * D_MIX ** -0.5
    post_norm_g = 1.0 + 0.05 * jax.random.normal(ks[8], (DEPTH, D_MODEL), f32)
    return {"x": x, "pre_norm_g": pre_norm_g, "w_in": w_in, "rel_bias": rel_bias,
            "attn_sink": attn_sink, "pool_w": pool_w, "pool_scale": pool_scale,
            "w_out": w_out, "post_norm_g": post_norm_g}


def reference(x, pre_norm_g, w_in, rel_bias, attn_sink, pool_w, pool_scale, w_out, post_norm_g):
    B, S, _ = x.shape
    splits = [D_ATTN, D_ATTN + D_KV, D_ATTN + 2 * D_KV,
              2 * D_ATTN + 2 * D_KV, 2 * D_ATTN + 2 * D_KV + D_POOL]
    for l in range(DEPTH):
        h = rmsnorm(x, pre_norm_g[l])
        proj = jnp.einsum('bsd,de->bse', h, w_in[l])
        q, k, v, g_a, u_p, g_p = jnp.split(proj, splits, axis=-1)
        q = q.reshape(B, S, N_Q_HEADS, HEAD_DIM)
        k = k.reshape(B, S, N_KV_HEADS, HEAD_DIM)
        v = v.reshape(B, S, N_KV_HEADS, HEAD_DIM)
        a = banded_attention(q, k, v, rel_bias, attn_sink[l]) * jax.nn.silu(g_a)
        p = multiscale_pool(u_p, pool_w[l], pool_scale[l]) * jax.nn.silu(g_p)
        mixed = jnp.einsum('bse,ed->bsd', jnp.concatenate([a, p], axis=-1), w_out[l])
        x = x + rmsnorm(mixed, post_norm_g[l])
    return x
```

```python
import functools

import jax
import jax.numpy as jnp
from jax import lax
from jax.experimental import pallas as pl
from jax.experimental.pallas import tpu as pltpu

D_MODEL = 1024
D_ATTN = 512
D_POOL = 512
HEAD_DIM = 64
N_Q_HEADS = 8
N_KV_HEADS = 2
Q_PER_KV = 4
D_KV = 128
WINDOW = 128
BLOCK = 128
N_BUCKETS = 32
POOL_WINDOWS = (2, 4, 8, 16)
POOL_GROUP = 128
D_IN = 2304
EPS = 1e-6
NEG = -1e30

LANES = 128
POOL_HALO = 16
TILE = 512
VMEM_LIMIT_BYTES = 56 * 1024 * 1024

Q0, K0, V0, GA0, U0, GP0 = 0, 512, 640, 768, 1280, 1792

_NT = (((1,), (1,)), ((), ()))


def _rmsnorm_rows(xv, g):
    ms = jnp.mean(xv * xv, axis=-1, keepdims=True)
    return (xv * lax.rsqrt(ms + EPS)) * g


def _t5_bucket_exact(rel):
    nb = N_BUCKETS // 2
    max_exact = nb // 2
    ret = jnp.where(rel > 0, nb, 0)
    n = jnp.abs(rel)
    n2 = n * n
    large = jnp.full(rel.shape, max_exact, jnp.int32)
    for j in range(1, nb - max_exact):
        large = large + jnp.where(n2 >= (max_exact * max_exact) << j, 1, 0)
    return ret + jnp.where(n < max_exact, n, large)


def _layer_kernel(relb_ref, sink_ref,
                  xp_ref, x_ref, xn_ref, gpre_ref, win_ref, poolw_ref, pscale_ref, wout_ref, gpost_ref,
                  out_ref,
                  bias_scr, q_scr, kvar_scr, vvar_scr, u_scr, sga_scr, sgp_scr, mix_scr,
                  *, n_steps):
    i = pl.program_id(0)
    T = TILE
    R = T // BLOCK
    f32, bf16 = jnp.float32, jnp.bfloat16

    @pl.when(i == 0)
    def _():
        qi = lax.broadcasted_iota(jnp.int32, (BLOCK, 3 * BLOCK), 0)
        kj = lax.broadcasted_iota(jnp.int32, (BLOCK, 3 * BLOCK), 1)
        rel = (kj - BLOCK) - qi
        bucket = _t5_bucket_exact(rel)
        in_band = jnp.abs(rel) <= WINDOW
        for h in range(N_Q_HEADS):
            def body(b, acc, h=h):
                return jnp.where(bucket == b, relb_ref[b * N_Q_HEADS + h], acc)
            tbl = lax.fori_loop(0, N_BUCKETS, body, jnp.zeros((BLOCK, 3 * BLOCK), f32))
            bias_scr[0, h] = jnp.where(in_band, tbl, NEG)
            bias_scr[1, h] = jnp.where(in_band & (kj >= BLOCK), tbl, NEG)
            bias_scr[2, h] = jnp.where(in_band & (kj < 2 * BLOCK), tbl, NEG)

    is_first = i == 0
    is_last = i == n_steps - 1

    gpre = gpre_ref[...]
    xn = _rmsnorm_rows(x_ref[...], gpre).astype(bf16)
    xp = _rmsnorm_rows(xp_ref[...], gpre).astype(bf16)
    xx = _rmsnorm_rows(xn_ref[...], gpre).astype(bf16)

    def proj(lhs, c0, width):
        return jnp.dot(lhs, win_ref[:, c0:c0 + width], preferred_element_type=f32)

    q_scr[...] = (proj(xn, Q0, D_ATTN) * (HEAD_DIM ** -0.5)).astype(bf16)

    lane = lax.broadcasted_iota(jnp.int32, (1, LANES), 1)
    lo = lane < HEAD_DIM

    def store_kv(kv, r0, rows):
        for src, dst in ((kv[:, :D_KV], kvar_scr), (kv[:, D_KV:], vvar_scr)):
            rolled = pltpu.roll(src, HEAD_DIM, axis=1)
            dst[0, r0:r0 + rows, :] = jnp.where(lo, src, 0.0).astype(bf16)
            dst[1, r0:r0 + rows, :] = jnp.where(lo, 0.0, rolled).astype(bf16)
            dst[2, r0:r0 + rows, :] = jnp.where(lo, rolled, 0.0).astype(bf16)
            dst[3, r0:r0 + rows, :] = jnp.where(lo, 0.0, src).astype(bf16)

    store_kv(proj(xp, K0, 2 * D_KV), 0, BLOCK)
    store_kv(proj(xn, K0, 2 * D_KV), BLOCK, T)
    store_kv(proj(xx, K0, 2 * D_KV), BLOCK + T, BLOCK)

    sga_scr[...] = jax.nn.silu(proj(xn, GA0, D_ATTN))
    sgp_scr[...] = jax.nn.silu(proj(xn, GP0, D_POOL))

    u_prev = proj(xp[BLOCK - POOL_HALO:, :], U0, D_POOL)
    u_next = proj(xx[:POOL_HALO, :], U0, D_POOL)
    u_scr[0:POOL_HALO, :] = jnp.where(is_first, 0.0, u_prev)
    u_scr[POOL_HALO:POOL_HALO + T, :] = proj(xn, U0, D_POOL)
    u_scr[POOL_HALO + T:, :] = jnp.where(is_last, 0.0, u_next)

    for b in range(R):
        if b == 0:
            tbl = jnp.where(is_first, 1, 0)
        elif b == R - 1:
            tbl = jnp.where(is_last, 2, 0)
        else:
            tbl = 0
        rows = slice(b * BLOCK, (b + 1) * BLOCK)
        win = slice(b * BLOCK, b * BLOCK + 3 * BLOCK)
        for j in range(N_KV_HEADS):
            c0 = j * Q_PER_KV * HEAD_DIM
            q2 = jnp.concatenate([q_scr[rows, c0:c0 + LANES], q_scr[rows, c0 + LANES:c0 + 2 * LANES]], axis=0)
            p_all, l_all = [], []
            for par in range(2):
                s2 = lax.dot_general(q2, kvar_scr[2 * j + par, win, :], _NT, preferred_element_type=f32)
                for half in range(2):
                    h = j * Q_PER_KV + 2 * half + par
                    s = s2[half * BLOCK:(half + 1) * BLOCK, :] + bias_scr[tbl, h]
                    sink = sink_ref[h]
                    m = jnp.maximum(jnp.max(s, axis=-1, keepdims=True), sink)
                    p = jnp.exp(s - m)
                    l_all.append((half, par, jnp.sum(p, axis=-1, keepdims=True) + jnp.exp(sink - m)))
                    p_all.append((half, par, p.astype(bf16)))
            pd = {(half, par): p for half, par, p in p_all}
            ld = {(half, par): l for half, par, l in l_all}
            for half in range(2):
                o = (jnp.dot(pd[(half, 0)], vvar_scr[2 * j, win, :], preferred_element_type=f32)
                     + jnp.dot(pd[(half, 1)], vvar_scr[2 * j + 1, win, :], preferred_element_type=f32))
                denom = jnp.where(lo, ld[(half, 0)], ld[(half, 1)])
                cs = c0 + half * LANES
                a = (o / denom) * sga_scr[rows, cs:cs + LANES]
                mix_scr[rows, cs:cs + LANES] = a.astype(bf16)

    t_glob = i * T + lax.broadcasted_iota(jnp.int32, (T, LANES), 0)
    seq = n_steps * T
    for gi, w in enumerate(POOL_WINDOWS):
        left = w // 2
        right = w - 1 - left
        cols = slice(gi * POOL_GROUP, (gi + 1) * POOL_GROUP)
        usum = None
        for d in range(-left, right + 1):
            piece = u_scr[POOL_HALO + d:POOL_HALO + d + T, cols]
            usum = piece if usum is None else usum + piece
        cnt = (jnp.minimum(t_glob + right + 1, seq) - jnp.maximum(t_glob - left, 0)).astype(f32)
        y = usum / cnt - u_scr[POOL_HALO:POOL_HALO + T, cols]
        yw = jnp.dot(y.astype(bf16), poolw_ref[gi], preferred_element_type=f32)
        pz = (yw * pscale_ref[:, cols]) * sgp_scr[:, cols]
        mix_scr[:, D_ATTN + gi * POOL_GROUP:D_ATTN + (gi + 1) * POOL_GROUP] = pz.astype(bf16)

    mixed = jnp.dot(mix_scr[...], wout_ref[...], preferred_element_type=f32)
    out_ref[...] = x_ref[...] + _rmsnorm_rows(mixed, gpost_ref[...])


def kernel(x, pre_norm_g, w_in, rel_bias, attn_sink, pool_w, pool_scale, w_out, post_norm_g):
    B, S, D = x.shape
    assert B == 1 and D == D_MODEL and S % TILE == 0 and S // BLOCK >= 2
    assert pre_norm_g.shape[0] == 1, "single layer"
    T = TILE
    R = T // BLOCK
    n_steps = S // T
    n_blocks = S // BLOCK
    bf16 = jnp.bfloat16

    x2 = x.reshape(S, D)
    smem = pl.BlockSpec(memory_space=pltpu.SMEM)
    full = lambda shape: pl.BlockSpec(shape, lambda i: (0,) * len(shape))

    out = pl.pallas_call(
        functools.partial(_layer_kernel, n_steps=n_steps),
        grid=(n_steps,),
        in_specs=[
            smem, smem,
            pl.BlockSpec((BLOCK, D), lambda i: (jnp.maximum(i * R - 1, 0), 0)),
            pl.BlockSpec((T, D), lambda i: (i, 0)),
            pl.BlockSpec((BLOCK, D), lambda i: (jnp.minimum((i + 1) * R, n_blocks - 1), 0)),
            full((1, D)),
            full((D, D_IN)),
            full((len(POOL_WINDOWS), POOL_GROUP, POOL_GROUP)),
            full((1, D_POOL)),
            full((D, D)),
            full((1, D)),
        ],
        out_specs=pl.BlockSpec((T, D), lambda i: (i, 0)),
        out_shape=jax.ShapeDtypeStruct((S, D), x.dtype),
        scratch_shapes=[
            pltpu.VMEM((3, N_Q_HEADS, BLOCK, 3 * BLOCK), jnp.float32),
            pltpu.VMEM((T, D_ATTN), bf16),
            pltpu.VMEM((4, T + 2 * BLOCK, LANES), bf16),
            pltpu.VMEM((4, T + 2 * BLOCK, LANES), bf16),
            pltpu.VMEM((T + 2 * POOL_HALO, D_POOL), jnp.float32),
            pltpu.VMEM((T, D_ATTN), jnp.float32),
            pltpu.VMEM((T, D_POOL), jnp.float32),
            pltpu.VMEM((T, D), bf16),
        ],
        compiler_params=pltpu.CompilerParams(
            dimension_semantics=("arbitrary",),
            vmem_limit_bytes=VMEM_LIMIT_BYTES,
        ),
        name="hymba_layer_fused",
    )(
        rel_bias.reshape(-1).astype(jnp.float32), attn_sink[0].astype(jnp.float32),
        x2, x2, x2,
        pre_norm_g[0].reshape(1, D),
        w_in[0].astype(bf16),
        pool_w[0].astype(bf16),
        pool_scale[0].reshape(1, D_POOL),
        w_out[0].astype(bf16),
        post_norm_g[0].reshape(1, D),
    )
    return out.reshape(B, S, D)
```

```python
import functools
import math

import jax
import jax.numpy as jnp
from jax import lax
from jax.experimental import pallas as pl
from jax.experimental.pallas import tpu as pltpu

D_MODEL = 1024
D_ATTN = 512
D_POOL = 512
HEAD_DIM = 64
N_Q_HEADS = 8
N_KV_HEADS = 2
Q_PER_KV = 4
D_KV = 128
WINDOW = 128
BLOCK = 128
N_BUCKETS = 32
POOL_WINDOWS = (2, 4, 8, 16)
POOL_GROUP = 128
D_IN = 2304
EPS = 1e-6
NEG = -1e30
LOG2E = math.log2(math.e)

LANES = 128
CHUNK = 2 * LANES
POOL_HALO = 16
TILE = 512
VMEM_LIMIT_BYTES = 56 * 1024 * 1024

Q0, K0, V0, GA0, U0, GP0 = 0, 512, 640, 768, 1280, 1792

_NT = (((1,), (1,)), ((), ()))


def _rmsnorm_rows(xv, g):
    ms = jnp.mean(xv * xv, axis=-1, keepdims=True)
    return (xv * lax.rsqrt(ms + EPS)) * g


def _t5_bucket_exact(rel):
    nb = N_BUCKETS // 2
    max_exact = nb // 2
    ret = jnp.where(rel > 0, nb, 0)
    n = jnp.abs(rel)
    n2 = n * n
    large = jnp.full(rel.shape, max_exact, jnp.int32)
    for j in range(1, nb - max_exact):
        large = large + jnp.where(n2 >= (max_exact * max_exact) << j, 1, 0)
    return ret + jnp.where(n < max_exact, n, large)


def _window_sums(ug, w):
    n = ug.shape[0]
    if w == 2:
        return ug + pltpu.roll(ug, 1, axis=0)
    acc = ug + pltpu.roll(ug, n - 1, axis=0)
    span = 2
    while span * 2 < w:
        acc = acc + pltpu.roll(acc, n - span, axis=0)
        span *= 2
    return acc + pltpu.roll(acc, span, axis=0)


def _layer_kernel(relb_ref, sink_ref,
                  xa_ref, xb_ref, gpre_ref, win_ref, poolw_ref, pscale_ref, wout_ref, gpost_ref,
                  out_ref,
                  bias_scr, q_scr, k_scr, v_scr, u_scr, sga_scr, sgp_scr, mix_scr,
                  *, n_tiles):
    i = pl.program_id(0)
    T = TILE
    R = T // BLOCK
    D = D_MODEL
    f32, bf16 = jnp.float32, jnp.bfloat16

    @pl.when(i == 0)
    def _():
        qi = lax.broadcasted_iota(jnp.int32, (BLOCK, 3 * BLOCK), 0)
        kj = lax.broadcasted_iota(jnp.int32, (BLOCK, 3 * BLOCK), 1)
        rel = (kj - BLOCK) - qi
        bucket = _t5_bucket_exact(rel)
        in_band = jnp.abs(rel) <= WINDOW
        for h in range(N_Q_HEADS):
            def body(b, acc, h=h):
                return jnp.where(bucket == b, relb_ref[b * N_Q_HEADS + h], acc)
            tbl = lax.fori_loop(0, N_BUCKETS, body, jnp.zeros((BLOCK, 3 * BLOCK), f32)) * LOG2E
            bias_scr[0, h] = jnp.where(in_band, tbl, NEG)
            bias_scr[1, h] = jnp.where(in_band & (kj >= BLOCK), tbl, NEG)
            bias_scr[2, h] = jnp.where(in_band & (kj < 2 * BLOCK), tbl, NEG)
        for ref in (q_scr, k_scr, v_scr, u_scr, sga_scr, sgp_scr):
            ref[...] = jnp.zeros(ref.shape, ref.dtype)

    lane = lax.broadcasted_iota(jnp.int32, (1, LANES), 1)
    lo = lane < HEAD_DIM
    b_is_first = i == 1
    b_is_last = i == n_tiles
    tile_b = jnp.maximum(i - 1, 0)

    def run_step(pa, pb):
        zero = jnp.zeros((), bf16)
        xn = _rmsnorm_rows(xa_ref[...], gpre_ref[...]).astype(bf16)

        def proj(c0):
            return jnp.dot(xn, win_ref[:, c0:c0 + CHUNK], preferred_element_type=f32)

        def a_kv():
            kv = proj(K0)
            k2 = kv[:, :D_KV].astype(bf16)
            kr = pltpu.roll(kv[:, :D_KV], HEAD_DIM, axis=1).astype(bf16)
            kvars = (jnp.where(lo, k2, zero), jnp.where(lo, zero, kr),
                     jnp.where(lo, kr, zero), jnp.where(lo, zero, k2))
            for var, kk in enumerate(kvars):
                k_scr[pa, var, BLOCK:BLOCK + T, :] = kk
                k_scr[pb, var, BLOCK + T:, :] = kk[:BLOCK]
            v2 = kv[:, D_KV:].astype(bf16)
            vr = pltpu.roll(kv[:, D_KV:], HEAD_DIM, axis=1).astype(bf16)
            vvars = ((jnp.where(lo, v2, zero), jnp.where(lo, zero, vr)),
                     (jnp.where(lo, vr, zero), jnp.where(lo, zero, v2)))
            for j in range(N_KV_HEADS):
                for par in range(2):
                    vv = vvars[j][par]
                    for kb in range(R):
                        r0 = (kb + 1) * 2 * BLOCK + par * BLOCK
                        v_scr[pa, j, r0:r0 + BLOCK, :] = vv[kb * BLOCK:(kb + 1) * BLOCK]
                    r0 = (R + 1) * 2 * BLOCK + par * BLOCK
                    v_scr[pb, j, r0:r0 + BLOCK, :] = vv[:BLOCK]

        def a_q(c):
            cs = slice(c * CHUNK, (c + 1) * CHUNK)
            q_scr[pa, :, cs] = (proj(Q0 + cs.start) * (HEAD_DIM ** -0.5 * LOG2E)).astype(bf16)

        def a_gate(c0, dst, c):
            cs = slice(c * CHUNK, (c + 1) * CHUNK)
            dst[pa, :, cs] = jax.nn.silu(proj(c0 + cs.start))

        def a_u(c):
            cs = slice(c * CHUNK, (c + 1) * CHUNK)
            u = proj(U0 + cs.start)
            u_scr[pa, POOL_HALO:POOL_HALO + T, cs] = u
            u_scr[pb, POOL_HALO + T:, cs] = jnp.where(b_is_last, 0.0, u[:POOL_HALO])

        a_chunks = [functools.partial(a_q, 0), functools.partial(a_q, 1),
                    functools.partial(a_gate, GA0, sga_scr, 0), functools.partial(a_gate, GA0, sga_scr, 1),
                    functools.partial(a_u, 0), functools.partial(a_u, 1),
                    functools.partial(a_gate, GP0, sgp_scr, 0), functools.partial(a_gate, GP0, sgp_scr, 1)]

        units = [(b, j) for b in range(R) for j in range(N_KV_HEADS)]
        assert len(units) == len(a_chunks)

        def b_scores(b, j):
            rows = slice(b * BLOCK, (b + 1) * BLOCK)
            win = slice(b * BLOCK, b * BLOCK + 3 * BLOCK)
            c0 = j * Q_PER_KV * HEAD_DIM
            q2 = jnp.concatenate([q_scr[pb, rows, c0:c0 + LANES],
                                  q_scr[pb, rows, c0 + LANES:c0 + 2 * LANES]], axis=0)
            return [lax.dot_general(q2, k_scr[pb, 2 * j + par, win, :], _NT, preferred_element_type=f32)
                    for par in range(2)]

        def b_softmax(b, j, s2s):
            if b == 0:
                tbl = jnp.where(b_is_first, 1, 0)
            elif b == R - 1:
                tbl = jnp.where(b_is_last, 2, 0)
            else:
                tbl = 0
            pd, ld = {}, {}
            for par in range(2):
                for half in range(2):
                    h = j * Q_PER_KV + 2 * half + par
                    s = s2s[par][half * BLOCK:(half + 1) * BLOCK, :] + bias_scr[tbl, h]
                    sink = sink_ref[h] * LOG2E
                    m = jnp.maximum(jnp.max(s, axis=-1, keepdims=True), sink)
                    p = jnp.exp2(s - m)
                    ld[(half, par)] = jnp.sum(p, axis=-1, keepdims=True) + jnp.exp2(sink - m)
                    pd[(half, par)] = p.astype(bf16)
            return pd, ld

        def b_pv(b, j, pd, ld):
            rows = slice(b * BLOCK, (b + 1) * BLOCK)
            c0 = j * Q_PER_KV * HEAD_DIM
            pcat = jnp.concatenate(
                [jnp.concatenate([pd[(half, par)][:, kb * BLOCK:(kb + 1) * BLOCK]
                                  for kb in range(3) for par in range(2)], axis=1)
                 for half in range(2)], axis=0)
            o2 = jnp.dot(pcat, v_scr[pb, j, 2 * b * BLOCK:2 * b * BLOCK + 6 * BLOCK, :],
                         preferred_element_type=f32)
            for half in range(2):
                denom = jnp.where(lo, ld[(half, 0)], ld[(half, 1)])
                cs = c0 + half * LANES
                a = (o2[half * BLOCK:(half + 1) * BLOCK] / denom) * sga_scr[pb, rows, cs:cs + LANES]
                mix_scr[rows, cs:cs + LANES] = a.astype(bf16)

        def b_pool(gi):
            w = POOL_WINDOWS[gi]
            seq = n_tiles * T
            E = POOL_HALO
            left = w // 2
            right = w - 1 - left
            cols = slice(gi * POOL_GROUP, (gi + 1) * POOL_GROUP)
            ug = u_scr[pb, :, cols]
            wsum = _window_sums(ug, w)[E:E + T]
            uc = ug[E:E + T]
            y_mid = wsum * (1.0 / w) - uc
            parts = []
            for r0 in (0, T - E):
                t_glob = tile_b * T + r0 + lax.broadcasted_iota(jnp.int32, (E, LANES), 0)
                cnt = (jnp.minimum(t_glob + right + 1, seq) - jnp.maximum(t_glob - left, 0)).astype(f32)
                parts.append(wsum[r0:r0 + E] / cnt - uc[r0:r0 + E])
            y = jnp.concatenate([parts[0], y_mid[E:T - E], parts[1]], axis=0)
            yw = jnp.dot(y.astype(bf16), poolw_ref[gi], preferred_element_type=f32)
            pz = (yw * pscale_ref[:, cols]) * sgp_scr[pb, :, cols]
            mix_scr[:, D_ATTN + gi * POOL_GROUP:D_ATTN + (gi + 1) * POOL_GROUP] = pz.astype(bf16)

        a_kv()
        s_next = b_scores(*units[0])
        for n, (b, j) in enumerate(units):
            s_cur = s_next
            if n + 1 < len(units):
                s_next = b_scores(*units[n + 1])
            a_chunks[n]()
            pd, ld = b_softmax(b, j, s_cur)
            b_pv(b, j, pd, ld)

        n_out = D // CHUNK
        assert n_out == len(POOL_WINDOWS)
        acc = []
        for c in range(n_out):
            acc.append(jnp.dot(mix_scr[:, :D_ATTN], wout_ref[:D_ATTN, c * CHUNK:(c + 1) * CHUNK],
                               preferred_element_type=f32))
            b_pool(c)
        mixed = jnp.concatenate(
            [acc[c] + jnp.dot(mix_scr[:, D_ATTN:], wout_ref[D_ATTN:, c * CHUNK:(c + 1) * CHUNK],
                              preferred_element_type=f32) for c in range(n_out)], axis=1)
        out_ref[...] = xb_ref[...] + _rmsnorm_rows(mixed, gpost_ref[...])

        k_scr[pb, :, 0:BLOCK, :] = k_scr[pa, :, T:T + BLOCK, :]
        v_scr[pb, :, 0:2 * BLOCK, :] = v_scr[pa, :, R * 2 * BLOCK:(R + 1) * 2 * BLOCK, :]
        u_scr[pb, 0:POOL_HALO, :] = u_scr[pa, T:T + POOL_HALO, :]

    for pa in range(2):
        @pl.when(i % 2 == pa)
        def _(pa=pa):
            run_step(pa, 1 - pa)


def kernel(x, pre_norm_g, w_in, rel_bias, attn_sink, pool_w, pool_scale, w_out, post_norm_g):
    B, S, D = x.shape
    assert B == 1 and D == D_MODEL and S % TILE == 0 and S // BLOCK >= 2
    assert pre_norm_g.shape[0] == 1, "single layer"
    T = TILE
    R = T // BLOCK
    n_tiles = S // T
    bf16 = jnp.bfloat16

    x2 = x.reshape(S, D)
    smem = pl.BlockSpec(memory_space=pltpu.SMEM)
    full = lambda shape: pl.BlockSpec(shape, lambda i: (0,) * len(shape))

    out = pl.pallas_call(
        functools.partial(_layer_kernel, n_tiles=n_tiles),
        grid=(n_tiles + 1,),
        in_specs=[
            smem, smem,
            pl.BlockSpec((T, D), lambda i: (jnp.minimum(i, n_tiles - 1), 0)),
            pl.BlockSpec((T, D), lambda i: (jnp.maximum(i - 1, 0), 0)),
            full((1, D)),
            full((D, D_IN)),
            full((len(POOL_WINDOWS), POOL_GROUP, POOL_GROUP)),
            full((1, D_POOL)),
            full((D, D)),
            full((1, D)),
        ],
        out_specs=pl.BlockSpec((T, D), lambda i: (jnp.maximum(i - 1, 0), 0)),
        out_shape=jax.ShapeDtypeStruct((S, D), x.dtype),
        scratch_shapes=[
            pltpu.VMEM((3, N_Q_HEADS, BLOCK, 3 * BLOCK), jnp.float32),
            pltpu.VMEM((2, T, D_ATTN), bf16),
            pltpu.VMEM((2, 4, T + 2 * BLOCK, LANES), bf16),
            pltpu.VMEM((2, N_KV_HEADS, (R + 2) * 2 * BLOCK, LANES), bf16),
            pltpu.VMEM((2, T + 2 * POOL_HALO, D_POOL), jnp.float32),
            pltpu.VMEM((2, T, D_ATTN), jnp.float32),
            pltpu.VMEM((2, T, D_POOL), jnp.float32),
            pltpu.VMEM((T, D), bf16),
        ],
        compiler_params=pltpu.CompilerParams(
            dimension_semantics=("arbitrary",),
            vmem_limit_bytes=VMEM_LIMIT_BYTES,
        ),
        name="hymba_layer_fused",
    )(
        rel_bias.reshape(-1).astype(jnp.float32), attn_sink[0].astype(jnp.float32),
        x2, x2,
        pre_norm_g[0].reshape(1, D),
        w_in[0].astype(bf16),
        pool_w[0].astype(bf16),
        pool_scale[0].reshape(1, D_POOL),
        w_out[0].astype(bf16),
        post_norm_g[0].reshape(1, D),
    )
    return out.reshape(B, S, D)
```

```python
import functools
import math

import jax
import jax.numpy as jnp
from jax import lax
from jax.experimental import pallas as pl
from jax.experimental.pallas import tpu as pltpu

D_MODEL = 1024
D_ATTN = 512
D_POOL = 512
HEAD_DIM = 64
N_Q_HEADS = 8
N_KV_HEADS = 2
Q_PER_KV = 4
D_KV = 128
WINDOW = 128
BLOCK = 128
N_BUCKETS = 32
POOL_WINDOWS = (2, 4, 8, 16)
POOL_GROUP = 128
D_IN = 2304
EPS = 1e-6
NEG = -1e30
LOG2E = math.log2(math.e)

LANES = 128
CHUNK = 2 * LANES
POOL_HALO = 16
TILE = 512
VMEM_LIMIT_BYTES = 56 * 1024 * 1024

Q0, K0, V0, GA0, U0, GP0 = 0, 512, 640, 768, 1280, 1792

_NT = (((1,), (1,)), ((), ()))


def _rmsnorm_rows(xv, g):
    ms = jnp.mean(xv * xv, axis=-1, keepdims=True)
    return (xv * lax.rsqrt(ms + EPS)) * g


def _t5_bucket_exact(rel):
    nb = N_BUCKETS // 2
    max_exact = nb // 2
    ret = jnp.where(rel > 0, nb, 0)
    n = jnp.abs(rel)
    n2 = n * n
    large = jnp.full(rel.shape, max_exact, jnp.int32)
    for j in range(1, nb - max_exact):
        large = large + jnp.where(n2 >= (max_exact * max_exact) << j, 1, 0)
    return ret + jnp.where(n < max_exact, n, large)


def _window_sums(ug, w):
    n = ug.shape[0]
    if w == 2:
        return ug + pltpu.roll(ug, 1, axis=0)
    acc = ug + pltpu.roll(ug, n - 1, axis=0)
    span = 2
    while span * 2 < w:
        acc = acc + pltpu.roll(acc, n - span, axis=0)
        span *= 2
    return acc + pltpu.roll(acc, span, axis=0)


def _layer_kernel(relb_ref, sink_ref,
                  xa_ref, xb_ref, gpre_ref, win_ref, poolw_ref, pscale_ref, wout_ref, gpost_ref,
                  out_ref,
                  bias_scr, q_scr, k_scr, v_scr, u_scr, sga_scr, sgp_scr, mix_scr, xn_scr, pw2_scr, ones_scr,
                  *, n_tiles):
    i = pl.program_id(0)
    T = TILE
    R = T // BLOCK
    D = D_MODEL
    f32, bf16 = jnp.float32, jnp.bfloat16
    lane = lax.broadcasted_iota(jnp.int32, (1, LANES), 1)
    lo = lane < HEAD_DIM

    @pl.when(i == 0)
    def _():
        qi = lax.broadcasted_iota(jnp.int32, (BLOCK, 3 * BLOCK), 0)
        kj = lax.broadcasted_iota(jnp.int32, (BLOCK, 3 * BLOCK), 1)
        rel = (kj - BLOCK) - qi
        bucket = _t5_bucket_exact(rel)
        in_band = jnp.abs(rel) <= WINDOW
        for h in range(N_Q_HEADS):
            def body(b, acc, h=h):
                return jnp.where(bucket == b, relb_ref[b * N_Q_HEADS + h], acc)
            tbl = lax.fori_loop(0, N_BUCKETS, body, jnp.zeros((BLOCK, 3 * BLOCK), f32)) * LOG2E
            bias_scr[0, h] = jnp.where(in_band, tbl, NEG)
            bias_scr[1, h] = jnp.where(in_band & (kj >= BLOCK), tbl, NEG)
            bias_scr[2, h] = jnp.where(in_band & (kj < 2 * BLOCK), tbl, NEG)
        for ref in (q_scr, k_scr, v_scr, u_scr, sga_scr, sgp_scr, mix_scr, pw2_scr):
            ref[...] = jnp.zeros(ref.shape, ref.dtype)
        for gi in range(len(POOL_WINDOWS)):
            r0 = (gi % 2) * POOL_GROUP
            pw2_scr[gi // 2, r0:r0 + POOL_GROUP, r0:r0 + POOL_GROUP] = poolw_ref[gi]
        row_lo = lax.broadcasted_iota(jnp.int32, (2 * BLOCK, LANES), 0) < BLOCK
        ones_pat = jnp.where(row_lo, jnp.where(lo, 1.0, 0.0), jnp.where(lo, 0.0, 1.0)).astype(bf16)
        for kb in range(3):
            ones_scr[kb * 2 * BLOCK:(kb + 1) * 2 * BLOCK, :] = ones_pat
        xn_scr[0] = _rmsnorm_rows(xb_ref[...], gpre_ref[...]).astype(bf16)

    b_is_first = i == 1
    b_is_last = i == n_tiles
    tile_b = jnp.clip(i - 1, 0, n_tiles - 1)

    def run_step(pa, pb):
        zero = jnp.zeros((), bf16)

        def proj(c0):
            return jnp.dot(xn_scr[pa], win_ref[:, c0:c0 + CHUNK], preferred_element_type=f32)

        def a_kv():
            kv = proj(K0)
            k2 = kv[:, :D_KV].astype(bf16)
            kr = pltpu.roll(kv[:, :D_KV], HEAD_DIM, axis=1).astype(bf16)
            kvars = (jnp.where(lo, k2, zero), jnp.where(lo, zero, kr),
                     jnp.where(lo, kr, zero), jnp.where(lo, zero, k2))
            for var, kk in enumerate(kvars):
                k_scr[pa, var, BLOCK:BLOCK + T, :] = kk
                k_scr[pb, var, BLOCK + T:, :] = kk[:BLOCK]
            v2 = kv[:, D_KV:].astype(bf16)
            vr = pltpu.roll(kv[:, D_KV:], HEAD_DIM, axis=1).astype(bf16)
            vvars = ((jnp.where(lo, v2, zero), jnp.where(lo, zero, vr)),
                     (jnp.where(lo, vr, zero), jnp.where(lo, zero, v2)))
            for j in range(N_KV_HEADS):
                for par in range(2):
                    vv = vvars[j][par]
                    for kb in range(R):
                        r0 = (kb + 1) * 2 * BLOCK + par * BLOCK
                        v_scr[pa, j, r0:r0 + BLOCK, :] = vv[kb * BLOCK:(kb + 1) * BLOCK]
                    r0 = (R + 1) * 2 * BLOCK + par * BLOCK
                    v_scr[pb, j, r0:r0 + BLOCK, :] = vv[:BLOCK]

        def a_q(c):
            cs = slice(c * CHUNK, (c + 1) * CHUNK)
            q_scr[pa, :, cs] = (proj(Q0 + cs.start) * (HEAD_DIM ** -0.5 * LOG2E)).astype(bf16)

        def a_gate(c0, dst, c):
            cs = slice(c * CHUNK, (c + 1) * CHUNK)
            dst[pa, :, cs] = jax.nn.silu(proj(c0 + cs.start))

        def a_u(c):
            cs = slice(c * CHUNK, (c + 1) * CHUNK)
            u = proj(U0 + cs.start)
            u_scr[pa, POOL_HALO:POOL_HALO + T, cs] = u
            u_scr[pb, POOL_HALO + T:, cs] = jnp.where(b_is_last, 0.0, u[:POOL_HALO])

        units = [(b, j) for b in range(R) for j in range(N_KV_HEADS)]
        assert len(units) == 8, "the program order below is written for eight attention units per tile"

        def b_scores(n):
            b, j = units[n]
            rows = slice(b * BLOCK, (b + 1) * BLOCK)
            win = slice(b * BLOCK, b * BLOCK + 3 * BLOCK)
            c0 = j * Q_PER_KV * HEAD_DIM
            q2 = jnp.concatenate([q_scr[pb, rows, c0:c0 + LANES],
                                  q_scr[pb, rows, c0 + LANES:c0 + 2 * LANES]], axis=0)
            return [lax.dot_general(q2, k_scr[pb, 2 * j + par, win, :], _NT, preferred_element_type=f32)
                    for par in range(2)]

        def b_unit(n, s2s):
            b, j = units[n]
            if b == 0:
                tbl = jnp.where(b_is_first, 1, 0)
            elif b == R - 1:
                tbl = jnp.where(b_is_last, 2, 0)
            else:
                tbl = 0
            pd, sink_term = {}, {}
            for par in range(2):
                for half in range(2):
                    h = j * Q_PER_KV + 2 * half + par
                    s = s2s[par][half * BLOCK:(half + 1) * BLOCK, :] + bias_scr[tbl, h]
                    sink = sink_ref[h] * LOG2E
                    m = jnp.maximum(jnp.max(s, axis=-1, keepdims=True), sink)
                    pd[(half, par)] = jnp.exp2(s - m).astype(bf16)
                    sink_term[(half, par)] = jnp.exp2(sink - m)
            rows = slice(b * BLOCK, (b + 1) * BLOCK)
            c0 = j * Q_PER_KV * HEAD_DIM
            pcat = jnp.concatenate(
                [jnp.concatenate([pd[(half, par)][:, kb * BLOCK:(kb + 1) * BLOCK]
                                  for kb in range(3) for par in range(2)], axis=1)
                 for half in range(2)], axis=0)
            vwin = jnp.concatenate([v_scr[pb, j, 2 * b * BLOCK:2 * b * BLOCK + 6 * BLOCK, :], ones_scr[...]],
                                   axis=1)
            o2 = jnp.dot(pcat, vwin, preferred_element_type=f32)
            for half in range(2):
                hr = slice(half * BLOCK, (half + 1) * BLOCK)
                denom = o2[hr, LANES:] + jnp.where(lo, sink_term[(half, 0)], sink_term[(half, 1)])
                cs = c0 + half * LANES
                a = (o2[hr, :LANES] / denom) * sga_scr[pb, rows, cs:cs + LANES]
                mix_scr[pb, rows, cs:cs + LANES] = a.astype(bf16)

        def b_pool_y(gi):
            w = POOL_WINDOWS[gi]
            seq = n_tiles * T
            E = POOL_HALO
            left = w // 2
            right = w - 1 - left
            cols = slice(gi * POOL_GROUP, (gi + 1) * POOL_GROUP)
            ug = u_scr[pb, :, cols]
            wsum = _window_sums(ug, w)[E:E + T]
            uc = ug[E:E + T]
            y_mid = wsum * (1.0 / w) - uc
            parts = []
            for r0 in (0, T - E):
                t_glob = tile_b * T + r0 + lax.broadcasted_iota(jnp.int32, (E, LANES), 0)
                cnt = (jnp.minimum(t_glob + right + 1, seq) - jnp.maximum(t_glob - left, 0)).astype(f32)
                parts.append(wsum[r0:r0 + E] / cnt - uc[r0:r0 + E])
            return jnp.concatenate([parts[0], y_mid[E:T - E], parts[1]], axis=0).astype(bf16)

        def b_pool_pair(pr):
            cols = slice(pr * CHUNK, (pr + 1) * CHUNK)
            y2 = jnp.concatenate([b_pool_y(2 * pr), b_pool_y(2 * pr + 1)], axis=1)
            yw = jnp.dot(y2, pw2_scr[pr], preferred_element_type=f32)
            pz = (yw * pscale_ref[:, cols]) * sgp_scr[pb, :, cols]
            mix_scr[pb, :, D_ATTN + pr * CHUNK:D_ATTN + (pr + 1) * CHUNK] = pz.astype(bf16)

        half_rows = T // 2
        half_cols = D // 2

        def c_out(rh, ch):
            rows = slice(rh * half_rows, (rh + 1) * half_rows)
            return jnp.dot(mix_scr[pa, rows, :], wout_ref[:, ch * half_cols:(ch + 1) * half_cols],
                           preferred_element_type=f32)

        def c_post(rh, m_lo, m_hi):
            rows = slice(rh * half_rows, (rh + 1) * half_rows)
            mixed = jnp.concatenate([m_lo, m_hi], axis=1)
            out_ref[rows, :] = xb_ref[rows, :] + _rmsnorm_rows(mixed, gpost_ref[...])

        def prenorm(rb):
            rows = slice(rb * BLOCK, (rb + 1) * BLOCK)
            xn_scr[pb, rows, :] = _rmsnorm_rows(xa_ref[rows, :], gpre_ref[...]).astype(bf16)

        a_kv()
        s = {0: b_scores(0)}
        s[1] = b_scores(1)
        m00 = c_out(0, 0)
        m01 = c_out(0, 1)
        b_unit(0, s.pop(0))
        prenorm(0)
        prenorm(1)
        s[2] = b_scores(2)
        a_u(0)
        b_unit(1, s.pop(1))
        c_post(0, m00, m01)
        s[3] = b_scores(3)
        a_u(1)
        a_gate(GA0, sga_scr, 0)
        b_unit(2, s.pop(2))
        b_pool_pair(0)
        s[4] = b_scores(4)
        a_gate(GA0, sga_scr, 1)
        b_unit(3, s.pop(3))
        b_pool_pair(1)
        s[5] = b_scores(5)
        m10 = c_out(1, 0)
        m11 = c_out(1, 1)
        b_unit(4, s.pop(4))
        prenorm(2)
        s[6] = b_scores(6)
        a_gate(GP0, sgp_scr, 0)
        b_unit(5, s.pop(5))
        c_post(1, m10, m11)
        prenorm(3)
        s[7] = b_scores(7)
        a_gate(GP0, sgp_scr, 1)
        a_q(0)
        b_unit(6, s.pop(6))
        a_q(1)
        b_unit(7, s.pop(7))

        k_scr[pb, :, 0:BLOCK, :] = k_scr[pa, :, T:T + BLOCK, :]
        v_scr[pb, :, 0:2 * BLOCK, :] = v_scr[pa, :, R * 2 * BLOCK:(R + 1) * 2 * BLOCK, :]
        u_scr[pb, 0:POOL_HALO, :] = u_scr[pa, T:T + POOL_HALO, :]

    for pa in range(2):
        @pl.when(i % 2 == pa)
        def _(pa=pa):
            run_step(pa, 1 - pa)


def kernel(x, pre_norm_g, w_in, rel_bias, attn_sink, pool_w, pool_scale, w_out, post_norm_g):
    B, S, D = x.shape
    assert B == 1 and D == D_MODEL and S % TILE == 0 and S // BLOCK >= 2
    assert pre_norm_g.shape[0] == 1, "single layer"
    T = TILE
    R = T // BLOCK
    n_tiles = S // T
    bf16 = jnp.bfloat16

    x2 = x.reshape(S, D)
    smem = pl.BlockSpec(memory_space=pltpu.SMEM)
    full = lambda shape: pl.BlockSpec(shape, lambda i: (0,) * len(shape))

    out = pl.pallas_call(
        functools.partial(_layer_kernel, n_tiles=n_tiles),
        grid=(n_tiles + 2,),
        in_specs=[
            smem, smem,
            pl.BlockSpec((T, D), lambda i: (jnp.minimum(i + 1, n_tiles - 1), 0)),
            pl.BlockSpec((T, D), lambda i: (jnp.maximum(i - 2, 0), 0)),
            full((1, D)),
            full((D, D_IN)),
            full((len(POOL_WINDOWS), POOL_GROUP, POOL_GROUP)),
            full((1, D_POOL)),
            full((D, D)),
            full((1, D)),
        ],
        out_specs=pl.BlockSpec((T, D), lambda i: (jnp.maximum(i - 2, 0), 0)),
        out_shape=jax.ShapeDtypeStruct((S, D), x.dtype),
        scratch_shapes=[
            pltpu.VMEM((3, N_Q_HEADS, BLOCK, 3 * BLOCK), jnp.float32),
            pltpu.VMEM((2, T, D_ATTN), bf16),
            pltpu.VMEM((2, 4, T + 2 * BLOCK, LANES), bf16),
            pltpu.VMEM((2, N_KV_HEADS, (R + 2) * 2 * BLOCK, LANES), bf16),
            pltpu.VMEM((2, T + 2 * POOL_HALO, D_POOL), jnp.float32),
            pltpu.VMEM((2, T, D_ATTN), jnp.float32),
            pltpu.VMEM((2, T, D_POOL), jnp.float32),
            pltpu.VMEM((2, T, D), bf16),
            pltpu.VMEM((2, T, D), bf16),
            pltpu.VMEM((len(POOL_WINDOWS) // 2, CHUNK, CHUNK), bf16),
            pltpu.VMEM((3 * 2 * BLOCK, LANES), bf16),
        ],
        compiler_params=pltpu.CompilerParams(
            dimension_semantics=("arbitrary",),
            vmem_limit_bytes=VMEM_LIMIT_BYTES,
        ),
        name="hymba_layer_fused",
    )(
        rel_bias.reshape(-1).astype(jnp.float32), attn_sink[0].astype(jnp.float32),
        x2, x2,
        pre_norm_g[0].reshape(1, D),
        w_in[0].astype(bf16),
        pool_w[0].astype(bf16),
        pool_scale[0].reshape(1, D_POOL),
        w_out[0].astype(bf16),
        post_norm_g[0].reshape(1, D),
    )
    return out.reshape(B, S, D)
```

```python
import functools
import math

import jax
import jax.numpy as jnp
from jax import lax
from jax.experimental import pallas as pl
from jax.experimental.pallas import tpu as pltpu

D_MODEL = 1024
D_ATTN = 512
D_POOL = 512
HEAD_DIM = 64
N_Q_HEADS = 8
N_KV_HEADS = 2
Q_PER_KV = 4
D_KV = 128
WINDOW = 128
BLOCK = 128
N_BUCKETS = 32
POOL_WINDOWS = (2, 4, 8, 16)
POOL_GROUP = 128
D_IN = 2304
EPS = 1e-6
NEG = -1e30
LOG2E = math.log2(math.e)

LANES = 128
CHUNK = 2 * LANES
POOL_HALO = 16
TILE = 512
VMEM_LIMIT_BYTES = 56 * 1024 * 1024

Q0, K0, V0, GA0, U0, GP0 = 0, 512, 640, 768, 1280, 1792

_NT = (((1,), (1,)), ((), ()))


def _rmsnorm_rows(xv, g):
    ms = jnp.mean(xv * xv, axis=-1, keepdims=True)
    return (xv * lax.rsqrt(ms + EPS)) * g


def _t5_bucket_exact(rel):
    nb = N_BUCKETS // 2
    max_exact = nb // 2
    ret = jnp.where(rel > 0, nb, 0)
    n = jnp.abs(rel)
    n2 = n * n
    large = jnp.full(rel.shape, max_exact, jnp.int32)
    for j in range(1, nb - max_exact):
        large = large + jnp.where(n2 >= (max_exact * max_exact) << j, 1, 0)
    return ret + jnp.where(n < max_exact, n, large)


def _window_sums(ug, w):
    n = ug.shape[0]
    if w == 2:
        return ug + pltpu.roll(ug, 1, axis=0)
    acc = ug + pltpu.roll(ug, n - 1, axis=0)
    span = 2
    while span * 2 < w:
        acc = acc + pltpu.roll(acc, n - span, axis=0)
        span *= 2
    return acc + pltpu.roll(acc, span, axis=0)


def _layer_kernel(relb_ref, sink_ref,
                  xa_ref, xb_ref, gpre_ref, win_ref, poolw_ref, pscale_ref, wout_ref, gpost_ref,
                  out_ref,
                  bias_scr, q_scr, k_scr, v_scr, u_scr, sga_scr, sgp_scr, mix_scr, xn_scr, pw2_scr, ones_scr,
                  wout_scr,
                  *, n_tiles):
    i = pl.program_id(0)
    T = TILE
    R = T // BLOCK
    D = D_MODEL
    f32, bf16 = jnp.float32, jnp.bfloat16
    lane = lax.broadcasted_iota(jnp.int32, (1, LANES), 1)
    lo = lane < HEAD_DIM

    @pl.when(i == 0)
    def _():
        qi = lax.broadcasted_iota(jnp.int32, (BLOCK, 3 * BLOCK), 0)
        kj = lax.broadcasted_iota(jnp.int32, (BLOCK, 3 * BLOCK), 1)
        rel = (kj - BLOCK) - qi
        bucket = _t5_bucket_exact(rel)
        in_band = jnp.abs(rel) <= WINDOW
        for h in range(N_Q_HEADS):
            def body(b, acc, h=h):
                return jnp.where(bucket == b, relb_ref[b * N_Q_HEADS + h], acc)
            tbl = lax.fori_loop(0, N_BUCKETS, body, jnp.zeros((BLOCK, 3 * BLOCK), f32)) * LOG2E
            bias_scr[0, h] = jnp.where(in_band, tbl, NEG)
            bias_scr[1, h] = jnp.where(in_band & (kj >= BLOCK), tbl, NEG)
            bias_scr[2, h] = jnp.where(in_band & (kj < 2 * BLOCK), tbl, NEG)
        for ref in (q_scr, k_scr, v_scr, u_scr, sga_scr, sgp_scr, mix_scr, pw2_scr):
            ref[...] = jnp.zeros(ref.shape, ref.dtype)
        for gi in range(len(POOL_WINDOWS)):
            r0 = (gi % 2) * POOL_GROUP
            pw2_scr[gi // 2, r0:r0 + POOL_GROUP, r0:r0 + POOL_GROUP] = poolw_ref[gi]
        row_lo = lax.broadcasted_iota(jnp.int32, (2 * BLOCK, LANES), 0) < BLOCK
        ones_pat = jnp.where(row_lo, jnp.where(lo, 1.0, 0.0), jnp.where(lo, 0.0, 1.0)).astype(bf16)
        for kb in range(3):
            ones_scr[kb * 2 * BLOCK:(kb + 1) * 2 * BLOCK, :] = ones_pat
        wout_scr[...] = wout_ref[...]
        xn_scr[0] = _rmsnorm_rows(xb_ref[...], gpre_ref[...]).astype(bf16)

    b_is_first = i == 1
    b_is_last = i == n_tiles
    tile_b = jnp.clip(i - 1, 0, n_tiles - 1)

    def run_step(pa, pb):
        zero = jnp.zeros((), bf16)

        HR = T // 2
        RB = HR // BLOCK

        def proj(c0, rh):
            return jnp.dot(xn_scr[pa, rh * HR:(rh + 1) * HR, :], win_ref[:, c0:c0 + CHUNK],
                           preferred_element_type=f32)

        def a_kv(rh):
            kv = proj(K0, rh)
            k2 = kv[:, :D_KV].astype(bf16)
            kr = pltpu.roll(kv[:, :D_KV], HEAD_DIM, axis=1).astype(bf16)
            kvars = (jnp.where(lo, k2, zero), jnp.where(lo, zero, kr),
                     jnp.where(lo, kr, zero), jnp.where(lo, zero, k2))
            for var, kk in enumerate(kvars):
                k_scr[pa, var, BLOCK + rh * HR:BLOCK + (rh + 1) * HR, :] = kk
                if rh == 0:
                    k_scr[pb, var, BLOCK + T:, :] = kk[:BLOCK]
            v2 = kv[:, D_KV:].astype(bf16)
            vr = pltpu.roll(kv[:, D_KV:], HEAD_DIM, axis=1).astype(bf16)
            vvars = ((jnp.where(lo, v2, zero), jnp.where(lo, zero, vr)),
                     (jnp.where(lo, vr, zero), jnp.where(lo, zero, v2)))
            for j in range(N_KV_HEADS):
                for par in range(2):
                    vv = vvars[j][par]
                    for kb in range(RB):
                        r0 = (rh * RB + kb + 1) * 2 * BLOCK + par * BLOCK
                        v_scr[pa, j, r0:r0 + BLOCK, :] = vv[kb * BLOCK:(kb + 1) * BLOCK]
                    if rh == 0:
                        r0 = (R + 1) * 2 * BLOCK + par * BLOCK
                        v_scr[pb, j, r0:r0 + BLOCK, :] = vv[:BLOCK]

        def a_q(c, rh):
            cs = slice(c * CHUNK, (c + 1) * CHUNK)
            q_scr[pa, rh * HR:(rh + 1) * HR, cs] = (proj(Q0 + cs.start, rh) * (HEAD_DIM ** -0.5 * LOG2E)).astype(bf16)

        def a_gate(c0, dst, c, rh):
            cs = slice(c * CHUNK, (c + 1) * CHUNK)
            dst[pa, rh * HR:(rh + 1) * HR, cs] = jax.nn.silu(proj(c0 + cs.start, rh))

        def a_u(c, rh):
            cs = slice(c * CHUNK, (c + 1) * CHUNK)
            u = proj(U0 + cs.start, rh)
            u_scr[pa, POOL_HALO + rh * HR:POOL_HALO + (rh + 1) * HR, cs] = u
            if rh == 0:
                u_scr[pb, POOL_HALO + T:, cs] = jnp.where(b_is_last, 0.0, u[:POOL_HALO])

        units = [(b, j) for b in range(R) for j in range(N_KV_HEADS)]
        assert len(units) == 8, "the program order below is written for eight attention units per tile"

        def b_scores(n):
            b, j = units[n]
            rows = slice(b * BLOCK, (b + 1) * BLOCK)
            win = slice(b * BLOCK, b * BLOCK + 3 * BLOCK)
            c0 = j * Q_PER_KV * HEAD_DIM
            q2 = jnp.concatenate([q_scr[pb, rows, c0:c0 + LANES],
                                  q_scr[pb, rows, c0 + LANES:c0 + 2 * LANES]], axis=0)
            return [lax.dot_general(q2, k_scr[pb, 2 * j + par, win, :], _NT, preferred_element_type=f32)
                    for par in range(2)]

        def b_soft(n, s2s, par, pd, sink_term):
            b, j = units[n]
            if b == 0:
                tbl = jnp.where(b_is_first, 1, 0)
            elif b == R - 1:
                tbl = jnp.where(b_is_last, 2, 0)
            else:
                tbl = 0
            for half in range(2):
                h = j * Q_PER_KV + 2 * half + par
                s = s2s[par][half * BLOCK:(half + 1) * BLOCK, :] + bias_scr[tbl, h]
                sink = sink_ref[h] * LOG2E
                m = jnp.maximum(jnp.max(s, axis=-1, keepdims=True), sink)
                pd[(half, par)] = jnp.exp2(s - m).astype(bf16)
                sink_term[(half, par)] = jnp.exp2(sink - m)

        def b_pv(n, pd, sink_term):
            b, j = units[n]
            rows = slice(b * BLOCK, (b + 1) * BLOCK)
            c0 = j * Q_PER_KV * HEAD_DIM
            pcat = jnp.concatenate(
                [jnp.concatenate([pd[(half, par)][:, kb * BLOCK:(kb + 1) * BLOCK]
                                  for kb in range(3) for par in range(2)], axis=1)
                 for half in range(2)], axis=0)
            vwin = jnp.concatenate([v_scr[pb, j, 2 * b * BLOCK:2 * b * BLOCK + 6 * BLOCK, :], ones_scr[...]],
                                   axis=1)
            o2 = jnp.dot(pcat, vwin, preferred_element_type=f32)
            for half in range(2):
                hr = slice(half * BLOCK, (half + 1) * BLOCK)
                denom = o2[hr, LANES:] + jnp.where(lo, sink_term[(half, 0)], sink_term[(half, 1)])
                cs = c0 + half * LANES
                a = (o2[hr, :LANES] / denom) * sga_scr[pb, rows, cs:cs + LANES]
                mix_scr[pb, rows, cs:cs + LANES] = a.astype(bf16)

        def b_pool_y(gi):
            w = POOL_WINDOWS[gi]
            seq = n_tiles * T
            E = POOL_HALO
            left = w // 2
            right = w - 1 - left
            cols = slice(gi * POOL_GROUP, (gi + 1) * POOL_GROUP)
            ug = u_scr[pb, :, cols]
            wsum = _window_sums(ug, w)[E:E + T]
            uc = ug[E:E + T]
            y_mid = wsum * (1.0 / w) - uc
            parts = []
            for r0 in (0, T - E):
                t_glob = tile_b * T + r0 + lax.broadcasted_iota(jnp.int32, (E, LANES), 0)
                cnt = (jnp.minimum(t_glob + right + 1, seq) - jnp.maximum(t_glob - left, 0)).astype(f32)
                parts.append(wsum[r0:r0 + E] / cnt - uc[r0:r0 + E])
            return jnp.concatenate([parts[0], y_mid[E:T - E], parts[1]], axis=0).astype(bf16)

        def b_pool_pair(pr):
            cols = slice(pr * CHUNK, (pr + 1) * CHUNK)
            y2 = jnp.concatenate([b_pool_y(2 * pr), b_pool_y(2 * pr + 1)], axis=1)
            yw = jnp.dot(y2, pw2_scr[pr], preferred_element_type=f32)
            pz = (yw * pscale_ref[:, cols]) * sgp_scr[pb, :, cols]
            mix_scr[pb, :, D_ATTN + pr * CHUNK:D_ATTN + (pr + 1) * CHUNK] = pz.astype(bf16)

        n_cq = D // CHUNK

        def c_out(rh, cq):
            return jnp.dot(mix_scr[pa, rh * HR:(rh + 1) * HR, :], wout_scr[:, cq * CHUNK:(cq + 1) * CHUNK],
                           preferred_element_type=f32)

        def c_post(rh, quarters):
            rows = slice(rh * HR, (rh + 1) * HR)
            mixed = jnp.concatenate(quarters, axis=1)
            out_ref[rows, :] = xb_ref[rows, :] + _rmsnorm_rows(mixed, gpost_ref[...])

        def prenorm(rb):
            rows = slice(rb * BLOCK, (rb + 1) * BLOCK)
            xn_scr[pb, rows, :] = _rmsnorm_rows(xa_ref[rows, :], gpre_ref[...]).astype(bf16)

        mixed_q = {0: [], 1: []}
        part = functools.partial
        out_q = lambda rh, cq: part(lambda: mixed_q[rh].append(c_out(rh, cq)))
        ga = lambda c, rh: part(a_gate, GA0, sga_scr, c, rh)
        gp = lambda c, rh: part(a_gate, GP0, sgp_scr, c, rh)
        uu = lambda c, rh: part(a_u, c, rh)
        qq = lambda c, rh: part(a_q, c, rh)
        heavy = [out_q(0, 0), out_q(0, 1), out_q(0, 2),
                 out_q(0, 3), uu(0, 0), uu(0, 1),
                 uu(1, 0), uu(1, 1), ga(0, 0),
                 ga(0, 1), qq(0, 0), qq(0, 1),
                 out_q(1, 0), out_q(1, 1), out_q(1, 2),
                 out_q(1, 3), ga(1, 0), ga(1, 1),
                 gp(0, 0), gp(0, 1), qq(1, 0),
                 gp(1, 0), gp(1, 1), qq(1, 1)]
        assert len(heavy) == 3 * len(units)
        vector_only = {0: [part(prenorm, 0), part(prenorm, 1)],
                       1: [lambda: c_post(0, mixed_q[0])],
                       2: [part(b_pool_pair, 0)],
                       3: [part(b_pool_pair, 1)],
                       4: [part(prenorm, 2), part(prenorm, 3)],
                       5: [lambda: c_post(1, mixed_q[1])]}

        a_kv(0)
        a_kv(1)
        s = {0: b_scores(0)}
        for n in range(len(units)):
            if n + 1 < len(units):
                s[n + 1] = b_scores(n + 1)
            pd, sink_term = {}, {}
            heavy[3 * n]()
            b_soft(n, s[n], 0, pd, sink_term)
            heavy[3 * n + 1]()
            b_soft(n, s.pop(n), 1, pd, sink_term)
            heavy[3 * n + 2]()
            b_pv(n, pd, sink_term)
            for piece in vector_only.get(n, ()):
                piece()

        k_scr[pb, :, 0:BLOCK, :] = k_scr[pa, :, T:T + BLOCK, :]
        v_scr[pb, :, 0:2 * BLOCK, :] = v_scr[pa, :, R * 2 * BLOCK:(R + 1) * 2 * BLOCK, :]
        u_scr[pb, 0:POOL_HALO, :] = u_scr[pa, T:T + POOL_HALO, :]

    for pa in range(2):
        @pl.when(i % 2 == pa)
        def _(pa=pa):
            run_step(pa, 1 - pa)


def kernel(x, pre_norm_g, w_in, rel_bias, attn_sink, pool_w, pool_scale, w_out, post_norm_g):
    B, S, D = x.shape
    assert B == 1 and D == D_MODEL and S % TILE == 0 and S // BLOCK >= 2
    assert pre_norm_g.shape[0] == 1, "single layer"
    T = TILE
    R = T // BLOCK
    n_tiles = S // T
    bf16 = jnp.bfloat16

    x2 = x.reshape(S, D)
    smem = pl.BlockSpec(memory_space=pltpu.SMEM)
    full = lambda shape: pl.BlockSpec(shape, lambda i: (0,) * len(shape))

    out = pl.pallas_call(
        functools.partial(_layer_kernel, n_tiles=n_tiles),
        grid=(n_tiles + 2,),
        in_specs=[
            smem, smem,
            pl.BlockSpec((T, D), lambda i: (jnp.minimum(i + 1, n_tiles - 1), 0)),
            pl.BlockSpec((T, D), lambda i: (jnp.maximum(i - 2, 0), 0)),
            full((1, D)),
            full((D, D_IN)),
            full((len(POOL_WINDOWS), POOL_GROUP, POOL_GROUP)),
            full((1, D_POOL)),
            full((D, D)),
            full((1, D)),
        ],
        out_specs=pl.BlockSpec((T, D), lambda i: (jnp.maximum(i - 2, 0), 0)),
        out_shape=jax.ShapeDtypeStruct((S, D), x.dtype),
        scratch_shapes=[
            pltpu.VMEM((3, N_Q_HEADS, BLOCK, 3 * BLOCK), jnp.float32),
            pltpu.VMEM((2, T, D_ATTN), bf16),
            pltpu.VMEM((2, 4, T + 2 * BLOCK, LANES), bf16),
            pltpu.VMEM((2, N_KV_HEADS, (R + 2) * 2 * BLOCK, LANES), bf16),
            pltpu.VMEM((2, T + 2 * POOL_HALO, D_POOL), jnp.float32),
            pltpu.VMEM((2, T, D_ATTN), jnp.float32),
            pltpu.VMEM((2, T, D_POOL), jnp.float32),
            pltpu.VMEM((2, T, D), bf16),
            pltpu.VMEM((2, T, D), bf16),
            pltpu.VMEM((len(POOL_WINDOWS) // 2, CHUNK, CHUNK), bf16),
            pltpu.VMEM((3 * 2 * BLOCK, LANES), bf16),
            pltpu.VMEM((D, D), bf16),
        ],
        compiler_params=pltpu.CompilerParams(
            dimension_semantics=("arbitrary",),
            vmem_limit_bytes=VMEM_LIMIT_BYTES,
        ),
        name="hymba_layer_fused",
    )(
        rel_bias.reshape(-1).astype(jnp.float32), attn_sink[0].astype(jnp.float32),
        x2, x2,
        pre_norm_g[0].reshape(1, D),
        w_in[0].astype(bf16),
        pool_w[0].astype(bf16),
        pool_scale[0].reshape(1, D_POOL),
        w_out[0].astype(bf16),
        post_norm_g[0].reshape(1, D),
    )
    return out.reshape(B, S, D)
```

```python
import functools
import math

import jax
import jax.numpy as jnp
from jax import lax
from jax.experimental import pallas as pl
from jax.experimental.pallas import tpu as pltpu

D_MODEL = 1024
D_ATTN = 512
D_POOL = 512
HEAD_DIM = 64
N_Q_HEADS = 8
N_KV_HEADS = 2
Q_PER_KV = 4
D_KV = 128
WINDOW = 128
BLOCK = 128
N_BUCKETS = 32
POOL_WINDOWS = (2, 4, 8, 16)
POOL_GROUP = 128
D_IN = 2304
EPS = 1e-6
NEG = -1e30
LOG2E = math.log2(math.e)

LANES = 128
CHUNK = 2 * LANES
POOL_HALO = 16
TILE = 512
VMEM_LIMIT_BYTES = 56 * 1024 * 1024

Q0, K0, V0, GA0, U0, GP0 = 0, 512, 640, 768, 1280, 1792

_NT = (((1,), (1,)), ((), ()))


def _rmsnorm_rows(xv, g):
    ms = jnp.mean(xv * xv, axis=-1, keepdims=True)
    return (xv * lax.rsqrt(ms + EPS)) * g


def _t5_bucket_exact(rel):
    nb = N_BUCKETS // 2
    max_exact = nb // 2
    ret = jnp.where(rel > 0, nb, 0)
    n = jnp.abs(rel)
    n2 = n * n
    large = jnp.full(rel.shape, max_exact, jnp.int32)
    for j in range(1, nb - max_exact):
        large = large + jnp.where(n2 >= (max_exact * max_exact) << j, 1, 0)
    return ret + jnp.where(n < max_exact, n, large)


def _window_sums(ug, w):
    n = ug.shape[0]
    if w == 2:
        return ug + pltpu.roll(ug, 1, axis=0)
    acc = ug + pltpu.roll(ug, n - 1, axis=0)
    span = 2
    while span * 2 < w:
        acc = acc + pltpu.roll(acc, n - span, axis=0)
        span *= 2
    return acc + pltpu.roll(acc, span, axis=0)


def _layer_kernel(relb_ref, sink_ref,
                  xa_ref, xb_ref, gpre_ref, win_ref, poolw_ref, pscale_ref, wout_ref, gpost_ref,
                  out_ref,
                  bias_scr, q_scr, k_scr, v_scr, u_scr, sga_scr, sgp_scr, mix_scr, xn_scr, pw2_scr, ones_scr,
                  wout_scr,
                  *, n_tiles):
    i = pl.program_id(0)
    T = TILE
    R = T // BLOCK
    D = D_MODEL
    f32, bf16 = jnp.float32, jnp.bfloat16
    lane = lax.broadcasted_iota(jnp.int32, (1, LANES), 1)
    lo = lane < HEAD_DIM

    @pl.when(i == 0)
    def _():
        qi = lax.broadcasted_iota(jnp.int32, (BLOCK, 3 * BLOCK), 0)
        kj = lax.broadcasted_iota(jnp.int32, (BLOCK, 3 * BLOCK), 1)
        rel = (kj - BLOCK) - qi
        bucket = _t5_bucket_exact(rel)
        in_band = jnp.abs(rel) <= WINDOW
        for h in range(N_Q_HEADS):
            def body(b, acc, h=h):
                return jnp.where(bucket == b, relb_ref[b * N_Q_HEADS + h], acc)
            tbl = lax.fori_loop(0, N_BUCKETS, body, jnp.zeros((BLOCK, 3 * BLOCK), f32)) * LOG2E
            bias_scr[0, h] = jnp.where(in_band, tbl, NEG)
            bias_scr[1, h] = jnp.where(in_band & (kj >= BLOCK), tbl, NEG)
            bias_scr[2, h] = jnp.where(in_band & (kj < 2 * BLOCK), tbl, NEG)
        k_scr[0, :, 0:BLOCK, :] = jnp.zeros((4, BLOCK, LANES), bf16)
        v_scr[0, :, 0:2 * BLOCK, :] = jnp.zeros((N_KV_HEADS, 2 * BLOCK, LANES), bf16)
        u_scr[0, 0:POOL_HALO, :] = jnp.zeros((POOL_HALO, D_POOL), f32)
        pw2_scr[...] = jnp.zeros(pw2_scr.shape, bf16)
        for gi in range(len(POOL_WINDOWS)):
            r0 = (gi % 2) * POOL_GROUP
            pw2_scr[gi // 2, r0:r0 + POOL_GROUP, r0:r0 + POOL_GROUP] = poolw_ref[gi]
        row_lo = lax.broadcasted_iota(jnp.int32, (2 * BLOCK, LANES), 0) < BLOCK
        ones_pat = jnp.where(row_lo, jnp.where(lo, 1.0, 0.0), jnp.where(lo, 0.0, 1.0)).astype(bf16)
        for kb in range(3):
            ones_scr[kb * 2 * BLOCK:(kb + 1) * 2 * BLOCK, :] = ones_pat
        wout_scr[...] = wout_ref[...]
        xn_scr[0] = _rmsnorm_rows(xb_ref[...], gpre_ref[...]).astype(bf16)

    b_is_first = i == 1
    b_is_last = i == n_tiles
    tile_b = jnp.clip(i - 1, 0, n_tiles - 1)

    def run_step(pa, pb, do_a=True, do_b=True, do_c=True, do_prenorm=True):
        enabled = {"a": do_a, "b": do_b, "c": do_c, "n": do_prenorm}
        zero = jnp.zeros((), bf16)

        HR = T // 2
        RB = HR // BLOCK

        def proj(c0, rh):
            return jnp.dot(xn_scr[pa, rh * HR:(rh + 1) * HR, :], win_ref[:, c0:c0 + CHUNK],
                           preferred_element_type=f32)

        def a_kv(rh):
            kv = proj(K0, rh)
            k2 = kv[:, :D_KV].astype(bf16)
            kr = pltpu.roll(kv[:, :D_KV], HEAD_DIM, axis=1).astype(bf16)
            kvars = (jnp.where(lo, k2, zero), jnp.where(lo, zero, kr),
                     jnp.where(lo, kr, zero), jnp.where(lo, zero, k2))
            for var, kk in enumerate(kvars):
                k_scr[pa, var, BLOCK + rh * HR:BLOCK + (rh + 1) * HR, :] = kk
                if rh == 0:
                    k_scr[pb, var, BLOCK + T:, :] = kk[:BLOCK]
            v2 = kv[:, D_KV:].astype(bf16)
            vr = pltpu.roll(kv[:, D_KV:], HEAD_DIM, axis=1).astype(bf16)
            vvars = ((jnp.where(lo, v2, zero), jnp.where(lo, zero, vr)),
                     (jnp.where(lo, vr, zero), jnp.where(lo, zero, v2)))
            for j in range(N_KV_HEADS):
                for par in range(2):
                    vv = vvars[j][par]
                    for kb in range(RB):
                        r0 = (rh * RB + kb + 1) * 2 * BLOCK + par * BLOCK
                        v_scr[pa, j, r0:r0 + BLOCK, :] = vv[kb * BLOCK:(kb + 1) * BLOCK]
                    if rh == 0:
                        r0 = (R + 1) * 2 * BLOCK + par * BLOCK
                        v_scr[pb, j, r0:r0 + BLOCK, :] = vv[:BLOCK]

        def a_q(c, rh):
            cs = slice(c * CHUNK, (c + 1) * CHUNK)
            q_scr[pa, rh * HR:(rh + 1) * HR, cs] = (proj(Q0 + cs.start, rh) * (HEAD_DIM ** -0.5 * LOG2E)).astype(bf16)

        def a_gate(c0, dst, c, rh):
            cs = slice(c * CHUNK, (c + 1) * CHUNK)
            dst[pa, rh * HR:(rh + 1) * HR, cs] = jax.nn.silu(proj(c0 + cs.start, rh))

        def a_u(c, rh):
            cs = slice(c * CHUNK, (c + 1) * CHUNK)
            u = proj(U0 + cs.start, rh)
            u_scr[pa, POOL_HALO + rh * HR:POOL_HALO + (rh + 1) * HR, cs] = u
            if rh == 0:
                u_scr[pb, POOL_HALO + T:, cs] = jnp.where(b_is_last, 0.0, u[:POOL_HALO])

        units = [(b, j) for b in range(R) for j in range(N_KV_HEADS)]
        assert len(units) == 8, "the program order below is written for eight attention units per tile"

        def b_scores(n):
            b, j = units[n]
            rows = slice(b * BLOCK, (b + 1) * BLOCK)
            win = slice(b * BLOCK, b * BLOCK + 3 * BLOCK)
            c0 = j * Q_PER_KV * HEAD_DIM
            q2 = jnp.concatenate([q_scr[pb, rows, c0:c0 + LANES],
                                  q_scr[pb, rows, c0 + LANES:c0 + 2 * LANES]], axis=0)
            return [lax.dot_general(q2, k_scr[pb, 2 * j + par, win, :], _NT, preferred_element_type=f32)
                    for par in range(2)]

        def b_soft(n, s2s, par, pd, sink_term):
            b, j = units[n]
            if b == 0:
                tbl = jnp.where(b_is_first, 1, 0)
            elif b == R - 1:
                tbl = jnp.where(b_is_last, 2, 0)
            else:
                tbl = 0
            for half in range(2):
                h = j * Q_PER_KV + 2 * half + par
                s = s2s[par][half * BLOCK:(half + 1) * BLOCK, :] + bias_scr[tbl, h]
                sink = sink_ref[h] * LOG2E
                m = jnp.maximum(jnp.max(s, axis=-1, keepdims=True), sink)
                pd[(half, par)] = jnp.exp2(s - m).astype(bf16)
                sink_term[(half, par)] = jnp.exp2(sink - m)

        def b_pv(n, pd, sink_term):
            b, j = units[n]
            rows = slice(b * BLOCK, (b + 1) * BLOCK)
            c0 = j * Q_PER_KV * HEAD_DIM
            pcat = jnp.concatenate(
                [jnp.concatenate([pd[(half, par)][:, kb * BLOCK:(kb + 1) * BLOCK]
                                  for kb in range(3) for par in range(2)], axis=1)
                 for half in range(2)], axis=0)
            vwin = jnp.concatenate([v_scr[pb, j, 2 * b * BLOCK:2 * b * BLOCK + 6 * BLOCK, :], ones_scr[...]],
                                   axis=1)
            o2 = jnp.dot(pcat, vwin, preferred_element_type=f32)
            for half in range(2):
                hr = slice(half * BLOCK, (half + 1) * BLOCK)
                denom = o2[hr, LANES:] + jnp.where(lo, sink_term[(half, 0)], sink_term[(half, 1)])
                cs = c0 + half * LANES
                a = (o2[hr, :LANES] / denom) * sga_scr[pb, rows, cs:cs + LANES]
                mix_scr[pb, rows, cs:cs + LANES] = a.astype(bf16)

        def b_pool_y(gi):
            w = POOL_WINDOWS[gi]
            seq = n_tiles * T
            E = POOL_HALO
            left = w // 2
            right = w - 1 - left
            cols = slice(gi * POOL_GROUP, (gi + 1) * POOL_GROUP)
            ug = u_scr[pb, :, cols]
            wsum = _window_sums(ug, w)[E:E + T]
            uc = ug[E:E + T]
            y_mid = wsum * (1.0 / w) - uc
            parts = []
            for r0 in (0, T - E):
                t_glob = tile_b * T + r0 + lax.broadcasted_iota(jnp.int32, (E, LANES), 0)
                cnt = (jnp.minimum(t_glob + right + 1, seq) - jnp.maximum(t_glob - left, 0)).astype(f32)
                parts.append(wsum[r0:r0 + E] / cnt - uc[r0:r0 + E])
            return jnp.concatenate([parts[0], y_mid[E:T - E], parts[1]], axis=0).astype(bf16)

        def b_pool_pair(pr):
            cols = slice(pr * CHUNK, (pr + 1) * CHUNK)
            y2 = jnp.concatenate([b_pool_y(2 * pr), b_pool_y(2 * pr + 1)], axis=1)
            yw = jnp.dot(y2, pw2_scr[pr], preferred_element_type=f32)
            pz = (yw * pscale_ref[:, cols]) * sgp_scr[pb, :, cols]
            mix_scr[pb, :, D_ATTN + pr * CHUNK:D_ATTN + (pr + 1) * CHUNK] = pz.astype(bf16)

        n_cq = D // CHUNK

        def c_out(rh, cq):
            return jnp.dot(mix_scr[pa, rh * HR:(rh + 1) * HR, :], wout_scr[:, cq * CHUNK:(cq + 1) * CHUNK],
                           preferred_element_type=f32)

        def c_post(rh, quarters):
            rows = slice(rh * HR, (rh + 1) * HR)
            mixed = jnp.concatenate(quarters, axis=1)
            out_ref[rows, :] = xb_ref[rows, :] + _rmsnorm_rows(mixed, gpost_ref[...])

        def prenorm(rb):
            rows = slice(rb * BLOCK, (rb + 1) * BLOCK)
            xn_scr[pb, rows, :] = _rmsnorm_rows(xa_ref[rows, :], gpre_ref[...]).astype(bf16)

        mixed_q = {0: [], 1: []}
        part = functools.partial
        out_q = lambda rh, cq: ("c", lambda: mixed_q[rh].append(c_out(rh, cq)))
        ga = lambda c, rh: ("a", part(a_gate, GA0, sga_scr, c, rh))
        gp = lambda c, rh: ("a", part(a_gate, GP0, sgp_scr, c, rh))
        uu = lambda c, rh: ("a", part(a_u, c, rh))
        qq = lambda c, rh: ("a", part(a_q, c, rh))
        heavy = [out_q(0, 0), out_q(0, 1), out_q(0, 2),
                 out_q(0, 3), uu(0, 0), uu(0, 1),
                 uu(1, 0), uu(1, 1), ga(0, 0),
                 ga(0, 1), qq(0, 0), qq(0, 1),
                 out_q(1, 0), out_q(1, 1), out_q(1, 2),
                 out_q(1, 3), ga(1, 0), ga(1, 1),
                 gp(0, 0), gp(0, 1), qq(1, 0),
                 gp(1, 0), gp(1, 1), qq(1, 1)]
        assert len(heavy) == 3 * len(units)
        vector_only = {0: [("n", part(prenorm, 0)), ("n", part(prenorm, 1))],
                       1: [("c", lambda: c_post(0, mixed_q[0]))],
                       2: [("b", part(b_pool_pair, 0))],
                       3: [("b", part(b_pool_pair, 1))],
                       4: [("n", part(prenorm, 2)), ("n", part(prenorm, 3))],
                       5: [("c", lambda: c_post(1, mixed_q[1]))]}

        def run(piece):
            stage, fn = piece
            if enabled[stage]:
                fn()

        if do_a:
            a_kv(0)
            a_kv(1)
        elif do_b:
            u_scr[pb, POOL_HALO + T:, :] = jnp.zeros((POOL_HALO, D_POOL), f32)
        s = {0: b_scores(0)} if do_b else {}
        for n in range(len(units)):
            if do_b and n + 1 < len(units):
                s[n + 1] = b_scores(n + 1)
            pd, sink_term = {}, {}
            run(heavy[3 * n])
            if do_b:
                b_soft(n, s[n], 0, pd, sink_term)
            run(heavy[3 * n + 1])
            if do_b:
                b_soft(n, s.pop(n), 1, pd, sink_term)
            run(heavy[3 * n + 2])
            if do_b:
                b_pv(n, pd, sink_term)
            for piece in vector_only.get(n, ()):
                run(piece)

        if do_a:
            k_scr[pb, :, 0:BLOCK, :] = k_scr[pa, :, T:T + BLOCK, :]
            v_scr[pb, :, 0:2 * BLOCK, :] = v_scr[pa, :, R * 2 * BLOCK:(R + 1) * 2 * BLOCK, :]
            u_scr[pb, 0:POOL_HALO, :] = u_scr[pa, T:T + POOL_HALO, :]

    steady = (i >= 2) & (i < n_tiles)
    for pa in range(2):
        @pl.when(steady & (i % 2 == pa))
        def _(pa=pa):
            run_step(pa, 1 - pa)

    @pl.when(i == 0)
    def _():
        run_step(0, 1, do_b=False, do_c=False)

    @pl.when(i == 1)
    def _():
        run_step(1, 0, do_c=False)

    @pl.when(i == n_tiles)
    def _():
        run_step(n_tiles % 2, 1 - n_tiles % 2, do_a=False, do_prenorm=False)

    @pl.when(i == n_tiles + 1)
    def _():
        run_step((n_tiles + 1) % 2, n_tiles % 2, do_a=False, do_b=False, do_prenorm=False)


def kernel(x, pre_norm_g, w_in, rel_bias, attn_sink, pool_w, pool_scale, w_out, post_norm_g):
    B, S, D = x.shape
    assert B == 1 and D == D_MODEL and S % TILE == 0 and S // TILE >= 3
    assert pre_norm_g.shape[0] == 1, "single layer"
    T = TILE
    R = T // BLOCK
    n_tiles = S // T
    bf16 = jnp.bfloat16

    x2 = x.reshape(S, D)
    smem = pl.BlockSpec(memory_space=pltpu.SMEM)
    full = lambda shape: pl.BlockSpec(shape, lambda i: (0,) * len(shape))

    out = pl.pallas_call(
        functools.partial(_layer_kernel, n_tiles=n_tiles),
        grid=(n_tiles + 2,),
        in_specs=[
            smem, smem,
            pl.BlockSpec((T, D), lambda i: (jnp.minimum(i + 1, n_tiles - 1), 0)),
            pl.BlockSpec((T, D), lambda i: (jnp.maximum(i - 2, 0), 0)),
            full((1, D)),
            full((D, D_IN)),
            full((len(POOL_WINDOWS), POOL_GROUP, POOL_GROUP)),
            full((1, D_POOL)),
            full((D, D)),
            full((1, D)),
        ],
        out_specs=pl.BlockSpec((T, D), lambda i: (jnp.maximum(i - 2, 0), 0)),
        out_shape=jax.ShapeDtypeStruct((S, D), x.dtype),
        scratch_shapes=[
            pltpu.VMEM((3, N_Q_HEADS, BLOCK, 3 * BLOCK), jnp.float32),
            pltpu.VMEM((2, T, D_ATTN), bf16),
            pltpu.VMEM((2, 4, T + 2 * BLOCK, LANES), bf16),
            pltpu.VMEM((2, N_KV_HEADS, (R + 2) * 2 * BLOCK, LANES), bf16),
            pltpu.VMEM((2, T + 2 * POOL_HALO, D_POOL), jnp.float32),
            pltpu.VMEM((2, T, D_ATTN), jnp.float32),
            pltpu.VMEM((2, T, D_POOL), jnp.float32),
            pltpu.VMEM((2, T, D), bf16),
            pltpu.VMEM((2, T, D), bf16),
            pltpu.VMEM((len(POOL_WINDOWS) // 2, CHUNK, CHUNK), bf16),
            pltpu.VMEM((3 * 2 * BLOCK, LANES), bf16),
            pltpu.VMEM((D, D), bf16),
        ],
        compiler_params=pltpu.CompilerParams(
            dimension_semantics=("arbitrary",),
            vmem_limit_bytes=VMEM_LIMIT_BYTES,
        ),
        name="hymba_layer_fused",
    )(
        rel_bias.reshape(-1).astype(jnp.float32), attn_sink[0].astype(jnp.float32),
        x2, x2,
        pre_norm_g[0].reshape(1, D),
        w_in[0].astype(bf16),
        pool_w[0].astype(bf16),
        pool_scale[0].reshape(1, D_POOL),
        w_out[0].astype(bf16),
        post_norm_g[0].reshape(1, D),
    )
    return out.reshape(B, S, D)
```

```python
import functools
import math

import jax
import jax.numpy as jnp
from jax import lax
from jax.experimental import pallas as pl
from jax.experimental.pallas import tpu as pltpu

D_MODEL = 1024
D_ATTN = 512
D_POOL = 512
HEAD_DIM = 64
N_Q_HEADS = 8
N_KV_HEADS = 2
Q_PER_KV = 4
D_KV = 128
WINDOW = 128
BLOCK = 128
N_BUCKETS = 32
POOL_WINDOWS = (2, 4, 8, 16)
POOL_GROUP = 128
D_IN = 2304
EPS = 1e-6
NEG = -1e30
LOG2E = math.log2(math.e)

LANES = 128
CHUNK = 2 * LANES
POOL_HALO = 16
TILE = 512
VMEM_LIMIT_BYTES = 62 * 1024 * 1024

Q0, K0, V0, GA0, U0, GP0 = 0, 512, 640, 768, 1280, 1792

_NT = (((1,), (1,)), ((), ()))


def _rmsnorm_rows(xv, g):
    ms = jnp.mean(xv * xv, axis=-1, keepdims=True)
    return (xv * lax.rsqrt(ms + EPS)) * g


def _t5_bucket_exact(rel):
    nb = N_BUCKETS // 2
    max_exact = nb // 2
    ret = jnp.where(rel > 0, nb, 0)
    n = jnp.abs(rel)
    n2 = n * n
    large = jnp.full(rel.shape, max_exact, jnp.int32)
    for j in range(1, nb - max_exact):
        large = large + jnp.where(n2 >= (max_exact * max_exact) << j, 1, 0)
    return ret + jnp.where(n < max_exact, n, large)


def _window_sums(ug, w):
    n = ug.shape[0]
    if w == 2:
        return ug + pltpu.roll(ug, 1, axis=0)
    acc = ug + pltpu.roll(ug, n - 1, axis=0)
    span = 2
    while span * 2 < w:
        acc = acc + pltpu.roll(acc, n - span, axis=0)
        span *= 2
    return acc + pltpu.roll(acc, span, axis=0)


def _layer_kernel(relb_ref, sink_ref,
                  xa_ref, xb_ref, gpre_ref, win_ref, poolw_ref, pscale_ref, wout_ref, gpost_ref,
                  out_ref,
                  bias_scr, q_scr, k_scr, v_scr, u_scr, sga_scr, sgp_scr, mix_scr, xn_scr, pw2_scr, ones_scr,
                  wout_scr, win_scr,
                  *, n_tiles):
    i = pl.program_id(0)
    T = TILE
    R = T // BLOCK
    D = D_MODEL
    f32, bf16 = jnp.float32, jnp.bfloat16
    lane = lax.broadcasted_iota(jnp.int32, (1, LANES), 1)
    lo = lane < HEAD_DIM

    @pl.when(i == 0)
    def _():
        qi = lax.broadcasted_iota(jnp.int32, (BLOCK, 3 * BLOCK), 0)
        kj = lax.broadcasted_iota(jnp.int32, (BLOCK, 3 * BLOCK), 1)
        rel = (kj - BLOCK) - qi
        bucket = _t5_bucket_exact(rel)
        in_band = jnp.abs(rel) <= WINDOW
        for h in range(N_Q_HEADS):
            def body(b, acc, h=h):
                return jnp.where(bucket == b, relb_ref[b * N_Q_HEADS + h], acc)
            tbl = lax.fori_loop(0, N_BUCKETS, body, jnp.zeros((BLOCK, 3 * BLOCK), f32)) * LOG2E
            bias_scr[0, h] = jnp.where(in_band, tbl, NEG)
            bias_scr[1, h] = jnp.where(in_band & (kj >= BLOCK), tbl, NEG)
            bias_scr[2, h] = jnp.where(in_band & (kj < 2 * BLOCK), tbl, NEG)
        k_scr[0, :, 0:BLOCK, :] = jnp.zeros((4, BLOCK, LANES), bf16)
        v_scr[0, :, 0:2 * BLOCK, :] = jnp.zeros((N_KV_HEADS, 2 * BLOCK, LANES), bf16)
        u_scr[0, 0:POOL_HALO, :] = jnp.zeros((POOL_HALO, D_POOL), f32)
        pw2_scr[...] = jnp.zeros(pw2_scr.shape, bf16)
        for gi in range(len(POOL_WINDOWS)):
            r0 = (gi % 2) * POOL_GROUP
            pw2_scr[gi // 2, r0:r0 + POOL_GROUP, r0:r0 + POOL_GROUP] = poolw_ref[gi]
        row_lo = lax.broadcasted_iota(jnp.int32, (2 * BLOCK, LANES), 0) < BLOCK
        ones_pat = jnp.where(row_lo, jnp.where(lo, 1.0, 0.0), jnp.where(lo, 0.0, 1.0)).astype(bf16)
        for kb in range(3):
            ones_scr[kb * 2 * BLOCK:(kb + 1) * 2 * BLOCK, :] = ones_pat
        for c in range(D_IN // CHUNK):
            win_scr[:, c * CHUNK:(c + 1) * CHUNK] = win_ref[:, c * CHUNK:(c + 1) * CHUNK].astype(bf16)
        for c in range(D // CHUNK):
            wout_scr[:, c * CHUNK:(c + 1) * CHUNK] = wout_ref[:, c * CHUNK:(c + 1) * CHUNK].astype(bf16)
        xn_scr[0] = _rmsnorm_rows(xb_ref[...], gpre_ref[...]).astype(bf16)

    b_is_first = i == 1
    b_is_last = i == n_tiles
    tile_b = jnp.clip(i - 1, 0, n_tiles - 1)

    def run_step(pa, pb, do_a=True, do_b=True, do_c=True, do_prenorm=True):
        enabled = {"a": do_a, "b": do_b, "c": do_c, "n": do_prenorm}
        zero = jnp.zeros((), bf16)

        HR = T // 2
        RB = HR // BLOCK

        def proj(c0, rh):
            return jnp.dot(xn_scr[pa, rh * HR:(rh + 1) * HR, :], win_scr[:, c0:c0 + CHUNK],
                           preferred_element_type=f32)

        def a_kv(rh):
            kv = proj(K0, rh)
            k2 = kv[:, :D_KV].astype(bf16)
            kr = pltpu.roll(kv[:, :D_KV], HEAD_DIM, axis=1).astype(bf16)
            kvars = (jnp.where(lo, k2, zero), jnp.where(lo, zero, kr),
                     jnp.where(lo, kr, zero), jnp.where(lo, zero, k2))
            for var, kk in enumerate(kvars):
                k_scr[pa, var, BLOCK + rh * HR:BLOCK + (rh + 1) * HR, :] = kk
                if rh == 0:
                    k_scr[pb, var, BLOCK + T:, :] = kk[:BLOCK]
            v2 = kv[:, D_KV:].astype(bf16)
            vr = pltpu.roll(kv[:, D_KV:], HEAD_DIM, axis=1).astype(bf16)
            vvars = ((jnp.where(lo, v2, zero), jnp.where(lo, zero, vr)),
                     (jnp.where(lo, vr, zero), jnp.where(lo, zero, v2)))
            for j in range(N_KV_HEADS):
                for par in range(2):
                    vv = vvars[j][par]
                    for kb in range(RB):
                        r0 = (rh * RB + kb + 1) * 2 * BLOCK + par * BLOCK
                        v_scr[pa, j, r0:r0 + BLOCK, :] = vv[kb * BLOCK:(kb + 1) * BLOCK]
                    if rh == 0:
                        r0 = (R + 1) * 2 * BLOCK + par * BLOCK
                        v_scr[pb, j, r0:r0 + BLOCK, :] = vv[:BLOCK]

        def a_q(c, rh):
            cs = slice(c * CHUNK, (c + 1) * CHUNK)
            q_scr[pa, rh * HR:(rh + 1) * HR, cs] = (proj(Q0 + cs.start, rh) * (HEAD_DIM ** -0.5 * LOG2E)).astype(bf16)

        def a_gate(c0, dst, c, rh):
            cs = slice(c * CHUNK, (c + 1) * CHUNK)
            dst[pa, rh * HR:(rh + 1) * HR, cs] = jax.nn.silu(proj(c0 + cs.start, rh))

        def a_u(c, rh):
            cs = slice(c * CHUNK, (c + 1) * CHUNK)
            u = proj(U0 + cs.start, rh)
            u_scr[pa, POOL_HALO + rh * HR:POOL_HALO + (rh + 1) * HR, cs] = u
            if rh == 0:
                u_scr[pb, POOL_HALO + T:, cs] = jnp.where(b_is_last, 0.0, u[:POOL_HALO])

        units = [(b, j) for b in range(R) for j in range(N_KV_HEADS)]
        assert len(units) == 8, "the program order below is written for eight attention units per tile"

        def b_scores(n):
            b, j = units[n]
            rows = slice(b * BLOCK, (b + 1) * BLOCK)
            win = slice(b * BLOCK, b * BLOCK + 3 * BLOCK)
            c0 = j * Q_PER_KV * HEAD_DIM
            q2 = jnp.concatenate([q_scr[pb, rows, c0:c0 + LANES],
                                  q_scr[pb, rows, c0 + LANES:c0 + 2 * LANES]], axis=0)
            return [lax.dot_general(q2, k_scr[pb, 2 * j + par, win, :], _NT, preferred_element_type=f32)
                    for par in range(2)]

        def b_soft(n, s2s, par, pd, sink_term):
            b, j = units[n]
            if b == 0:
                tbl = jnp.where(b_is_first, 1, 0)
            elif b == R - 1:
                tbl = jnp.where(b_is_last, 2, 0)
            else:
                tbl = 0
            for half in range(2):
                h = j * Q_PER_KV + 2 * half + par
                s = s2s[par][half * BLOCK:(half + 1) * BLOCK, :] + bias_scr[tbl, h]
                sink = sink_ref[h] * LOG2E
                m = jnp.maximum(jnp.max(s, axis=-1, keepdims=True), sink)
                pd[(half, par)] = jnp.exp2(s - m).astype(bf16)
                sink_term[(half, par)] = jnp.exp2(sink - m)

        def b_pv(n, pd, sink_term):
            b, j = units[n]
            rows = slice(b * BLOCK, (b + 1) * BLOCK)
            c0 = j * Q_PER_KV * HEAD_DIM
            pcat = jnp.concatenate(
                [jnp.concatenate([pd[(half, par)][:, kb * BLOCK:(kb + 1) * BLOCK]
                                  for kb in range(3) for par in range(2)], axis=1)
                 for half in range(2)], axis=0)
            vwin = jnp.concatenate([v_scr[pb, j, 2 * b * BLOCK:2 * b * BLOCK + 6 * BLOCK, :], ones_scr[...]],
                                   axis=1)
            o2 = jnp.dot(pcat, vwin, preferred_element_type=f32)
            for half in range(2):
                hr = slice(half * BLOCK, (half + 1) * BLOCK)
                denom = o2[hr, LANES:] + jnp.where(lo, sink_term[(half, 0)], sink_term[(half, 1)])
                cs = c0 + half * LANES
                a = (o2[hr, :LANES] / denom) * sga_scr[pb, rows, cs:cs + LANES]
                mix_scr[pb, rows, cs:cs + LANES] = a.astype(bf16)

        def b_pool_y(gi):
            w = POOL_WINDOWS[gi]
            seq = n_tiles * T
            E = POOL_HALO
            left = w // 2
            right = w - 1 - left
            cols = slice(gi * POOL_GROUP, (gi + 1) * POOL_GROUP)
            ug = u_scr[pb, :, cols]
            wsum = _window_sums(ug, w)[E:E + T]
            uc = ug[E:E + T]
            y_mid = wsum * (1.0 / w) - uc
            parts = []
            for r0 in (0, T - E):
                t_glob = tile_b * T + r0 + lax.broadcasted_iota(jnp.int32, (E, LANES), 0)
                cnt = (jnp.minimum(t_glob + right + 1, seq) - jnp.maximum(t_glob - left, 0)).astype(f32)
                parts.append(wsum[r0:r0 + E] / cnt - uc[r0:r0 + E])
            return jnp.concatenate([parts[0], y_mid[E:T - E], parts[1]], axis=0).astype(bf16)

        def b_pool_pair(pr):
            cols = slice(pr * CHUNK, (pr + 1) * CHUNK)
            y2 = jnp.concatenate([b_pool_y(2 * pr), b_pool_y(2 * pr + 1)], axis=1)
            yw = jnp.dot(y2, pw2_scr[pr], preferred_element_type=f32)
            pz = (yw * pscale_ref[:, cols]) * sgp_scr[pb, :, cols]
            mix_scr[pb, :, D_ATTN + pr * CHUNK:D_ATTN + (pr + 1) * CHUNK] = pz.astype(bf16)

        n_cq = D // CHUNK

        def c_out(rh, cq):
            return jnp.dot(mix_scr[pa, rh * HR:(rh + 1) * HR, :], wout_scr[:, cq * CHUNK:(cq + 1) * CHUNK],
                           preferred_element_type=f32)

        def c_post(rh, quarters):
            rows = slice(rh * HR, (rh + 1) * HR)
            mixed = jnp.concatenate(quarters, axis=1)
            out_ref[rows, :] = xb_ref[rows, :] + _rmsnorm_rows(mixed, gpost_ref[...])

        def prenorm(rb):
            rows = slice(rb * BLOCK, (rb + 1) * BLOCK)
            xn_scr[pb, rows, :] = _rmsnorm_rows(xa_ref[rows, :], gpre_ref[...]).astype(bf16)

        mixed_q = {0: [], 1: []}
        part = functools.partial
        out_q = lambda rh, cq: ("c", lambda: mixed_q[rh].append(c_out(rh, cq)))
        ga = lambda c, rh: ("a", part(a_gate, GA0, sga_scr, c, rh))
        gp = lambda c, rh: ("a", part(a_gate, GP0, sgp_scr, c, rh))
        uu = lambda c, rh: ("a", part(a_u, c, rh))
        qq = lambda c, rh: ("a", part(a_q, c, rh))
        heavy = [out_q(0, 0), out_q(0, 1), out_q(0, 2),
                 out_q(0, 3), uu(0, 0), uu(0, 1),
                 uu(1, 0), uu(1, 1), ga(0, 0),
                 ga(0, 1), qq(0, 0), qq(0, 1),
                 out_q(1, 0), out_q(1, 1), out_q(1, 2),
                 out_q(1, 3), ga(1, 0), ga(1, 1),
                 gp(0, 0), gp(0, 1), qq(1, 0),
                 gp(1, 0), gp(1, 1), qq(1, 1)]
        assert len(heavy) == 3 * len(units)
        vector_only = {0: [("n", part(prenorm, 0)), ("n", part(prenorm, 1))],
                       1: [("c", lambda: c_post(0, mixed_q[0]))],
                       2: [("b", part(b_pool_pair, 0))],
                       3: [("b", part(b_pool_pair, 1))],
                       4: [("n", part(prenorm, 2)), ("n", part(prenorm, 3))],
                       5: [("c", lambda: c_post(1, mixed_q[1]))]}

        def run(piece):
            stage, fn = piece
            if enabled[stage]:
                fn()

        if do_a:
            a_kv(0)
            a_kv(1)
        elif do_b:
            u_scr[pb, POOL_HALO + T:, :] = jnp.zeros((POOL_HALO, D_POOL), f32)
        s = {0: b_scores(0)} if do_b else {}
        for n in range(len(units)):
            if do_b and n + 1 < len(units):
                s[n + 1] = b_scores(n + 1)
            pd, sink_term = {}, {}
            run(heavy[3 * n])
            if do_b:
                b_soft(n, s[n], 0, pd, sink_term)
            run(heavy[3 * n + 1])
            if do_b:
                b_soft(n, s.pop(n), 1, pd, sink_term)
            run(heavy[3 * n + 2])
            if do_b:
                b_pv(n, pd, sink_term)
            for piece in vector_only.get(n, ()):
                run(piece)

        if do_a:
            k_scr[pb, :, 0:BLOCK, :] = k_scr[pa, :, T:T + BLOCK, :]
            v_scr[pb, :, 0:2 * BLOCK, :] = v_scr[pa, :, R * 2 * BLOCK:(R + 1) * 2 * BLOCK, :]
            u_scr[pb, 0:POOL_HALO, :] = u_scr[pa, T:T + POOL_HALO, :]

    steady = (i >= 2) & (i < n_tiles)
    for pa in range(2):
        @pl.when(steady & (i % 2 == pa))
        def _(pa=pa):
            run_step(pa, 1 - pa)

    @pl.when(i == 0)
    def _():
        run_step(0, 1, do_b=False, do_c=False)

    @pl.when(i == 1)
    def _():
        run_step(1, 0, do_c=False)

    @pl.when(i == n_tiles)
    def _():
        run_step(n_tiles % 2, 1 - n_tiles % 2, do_a=False, do_prenorm=False)

    @pl.when(i == n_tiles + 1)
    def _():
        run_step((n_tiles + 1) % 2, n_tiles % 2, do_a=False, do_b=False, do_prenorm=False)


def kernel(x, pre_norm_g, w_in, rel_bias, attn_sink, pool_w, pool_scale, w_out, post_norm_g):
    B, S, D = x.shape
    assert B == 1 and D == D_MODEL and S % TILE == 0 and S // TILE >= 3
    assert pre_norm_g.shape[0] == 1, "single layer"
    T = TILE
    R = T // BLOCK
    n_tiles = S // T
    bf16 = jnp.bfloat16

    x2 = x.reshape(S, D)
    smem = pl.BlockSpec(memory_space=pltpu.SMEM)
    full = lambda shape: pl.BlockSpec(shape, lambda i: (0,) * len(shape))

    out = pl.pallas_call(
        functools.partial(_layer_kernel, n_tiles=n_tiles),
        grid=(n_tiles + 2,),
        in_specs=[
            smem, smem,
            pl.BlockSpec((T, D), lambda i: (jnp.minimum(i + 1, n_tiles - 1), 0)),
            pl.BlockSpec((T, D), lambda i: (jnp.maximum(i - 2, 0), 0)),
            full((1, D)),
            full((D, D_IN)),
            full((len(POOL_WINDOWS), POOL_GROUP, POOL_GROUP)),
            full((1, D_POOL)),
            full((D, D)),
            full((1, D)),
        ],
        out_specs=pl.BlockSpec((T, D), lambda i: (jnp.maximum(i - 2, 0), 0)),
        out_shape=jax.ShapeDtypeStruct((S, D), x.dtype),
        scratch_shapes=[
            pltpu.VMEM((3, N_Q_HEADS, BLOCK, 3 * BLOCK), jnp.float32),
            pltpu.VMEM((2, T, D_ATTN), bf16),
            pltpu.VMEM((2, 4, T + 2 * BLOCK, LANES), bf16),
            pltpu.VMEM((2, N_KV_HEADS, (R + 2) * 2 * BLOCK, LANES), bf16),
            pltpu.VMEM((2, T + 2 * POOL_HALO, D_POOL), jnp.float32),
            pltpu.VMEM((2, T, D_ATTN), jnp.float32),
            pltpu.VMEM((2, T, D_POOL), jnp.float32),
            pltpu.VMEM((2, T, D), bf16),
            pltpu.VMEM((2, T, D), bf16),
            pltpu.VMEM((len(POOL_WINDOWS) // 2, CHUNK, CHUNK), bf16),
            pltpu.VMEM((3 * 2 * BLOCK, LANES), bf16),
            pltpu.VMEM((D, D), bf16),
            pltpu.VMEM((D, D_IN), bf16),
        ],
        compiler_params=pltpu.CompilerParams(
            dimension_semantics=("arbitrary",),
            vmem_limit_bytes=VMEM_LIMIT_BYTES,
        ),
        name="hymba_layer_fused",
    )(
        rel_bias.reshape(-1).astype(jnp.float32), attn_sink[0].astype(jnp.float32),
        x2, x2,
        pre_norm_g[0].reshape(1, D),
        w_in[0],
        pool_w[0].astype(bf16),
        pool_scale[0].reshape(1, D_POOL),
        w_out[0],
        post_norm_g[0].reshape(1, D),
    )
    return out.reshape(B, S, D)
```

```python
import functools
import math

import jax
import jax.numpy as jnp
from jax import lax
from jax.experimental import pallas as pl
from jax.experimental.pallas import tpu as pltpu

D_MODEL = 1024
D_ATTN = 512
D_POOL = 512
HEAD_DIM = 64
N_Q_HEADS = 8
N_KV_HEADS = 2
Q_PER_KV = 4
D_KV = 128
WINDOW = 128
BLOCK = 128
N_BUCKETS = 32
POOL_WINDOWS = (2, 4, 8, 16)
POOL_GROUP = 128
D_IN = 2304
EPS = 1e-6
NEG = -1e30
LOG2E = math.log2(math.e)

LANES = 128
CHUNK = 2 * LANES
POOL_HALO = 16
TILE = 512
VMEM_LIMIT_BYTES = 62 * 1024 * 1024

Q0, K0, V0, GA0, U0, GP0 = 0, 512, 640, 768, 1280, 1792

_NT = (((1,), (1,)), ((), ()))


def _rmsnorm_rows(xv, g):
    ms = jnp.mean(xv * xv, axis=-1, keepdims=True)
    return (xv * lax.rsqrt(ms + EPS)) * g


def _t5_bucket_exact(rel):
    nb = N_BUCKETS // 2
    max_exact = nb // 2
    ret = jnp.where(rel > 0, nb, 0)
    n = jnp.abs(rel)
    n2 = n * n
    large = jnp.full(rel.shape, max_exact, jnp.int32)
    for j in range(1, nb - max_exact):
        large = large + jnp.where(n2 >= (max_exact * max_exact) << j, 1, 0)
    return ret + jnp.where(n < max_exact, n, large)


def _window_sums(ug, w):
    n = ug.shape[0]
    if w == 2:
        return ug + pltpu.roll(ug, 1, axis=0)
    acc = ug + pltpu.roll(ug, n - 1, axis=0)
    span = 2
    while span * 2 < w:
        acc = acc + pltpu.roll(acc, n - span, axis=0)
        span *= 2
    return acc + pltpu.roll(acc, span, axis=0)


def _layer_kernel(relb_ref, sink_ref,
                  xa_ref, xb_ref, gpre_ref, win_ref, poolw_ref, pscale_ref, wout_ref, gpost_ref,
                  out_ref,
                  bias_scr, q_scr, k_scr, v_scr, u_scr, sga_scr, sgp_scr, mix_scr, xn_scr, pw2_scr, ones_scr,
                  wout_scr, win_scr,
                  *, n_tiles):
    i = pl.program_id(0)
    T = TILE
    R = T // BLOCK
    D = D_MODEL
    f32, bf16 = jnp.float32, jnp.bfloat16
    lane = lax.broadcasted_iota(jnp.int32, (1, LANES), 1)
    lo = lane < HEAD_DIM

    @pl.when(i == 0)
    def _():
        qi = lax.broadcasted_iota(jnp.int32, (BLOCK, 3 * BLOCK), 0)
        kj = lax.broadcasted_iota(jnp.int32, (BLOCK, 3 * BLOCK), 1)
        rel = (kj - BLOCK) - qi
        bucket = _t5_bucket_exact(rel)
        in_band = jnp.abs(rel) <= WINDOW
        for h in range(N_Q_HEADS):
            def body(b, acc, h=h):
                return jnp.where(bucket == b, relb_ref[b, h], acc)
            tbl = lax.fori_loop(0, N_BUCKETS, body, jnp.zeros((BLOCK, 3 * BLOCK), f32)) * LOG2E
            bias_scr[0, h] = jnp.where(in_band, tbl, NEG)
            bias_scr[1, h] = jnp.where(in_band & (kj >= BLOCK), tbl, NEG)
            bias_scr[2, h] = jnp.where(in_band & (kj < 2 * BLOCK), tbl, NEG)
        k_scr[0, :, 0:BLOCK, :] = jnp.zeros((4, BLOCK, LANES), bf16)
        v_scr[0, :, 0:2 * BLOCK, :] = jnp.zeros((N_KV_HEADS, 2 * BLOCK, LANES), bf16)
        u_scr[0, 0:POOL_HALO, :] = jnp.zeros((POOL_HALO, D_POOL), f32)
        pw2_scr[...] = jnp.zeros(pw2_scr.shape, bf16)
        for gi in range(len(POOL_WINDOWS)):
            r0 = (gi % 2) * POOL_GROUP
            pw2_scr[gi // 2, r0:r0 + POOL_GROUP, r0:r0 + POOL_GROUP] = poolw_ref[gi].astype(bf16)
        row_lo = lax.broadcasted_iota(jnp.int32, (2 * BLOCK, LANES), 0) < BLOCK
        ones_pat = jnp.where(row_lo, jnp.where(lo, 1.0, 0.0), jnp.where(lo, 0.0, 1.0)).astype(bf16)
        for kb in range(3):
            ones_scr[kb * 2 * BLOCK:(kb + 1) * 2 * BLOCK, :] = ones_pat
        for c in range(D_IN // CHUNK):
            win_scr[:, c * CHUNK:(c + 1) * CHUNK] = win_ref[:, c * CHUNK:(c + 1) * CHUNK].astype(bf16)
        for c in range(D // CHUNK):
            wout_scr[:, c * CHUNK:(c + 1) * CHUNK] = wout_ref[:, c * CHUNK:(c + 1) * CHUNK].astype(bf16)
        xn_scr[0] = _rmsnorm_rows(xb_ref[...], gpre_ref[...]).astype(bf16)

    b_is_first = i == 1
    b_is_last = i == n_tiles
    tile_b = jnp.clip(i - 1, 0, n_tiles - 1)

    def run_step(pa, pb, do_a=True, do_b=True, do_c=True, do_prenorm=True):
        enabled = {"a": do_a, "b": do_b, "c": do_c, "n": do_prenorm}
        zero = jnp.zeros((), bf16)

        HR = T // 2
        RB = HR // BLOCK

        def proj(c0, rh):
            return jnp.dot(xn_scr[pa, rh * HR:(rh + 1) * HR, :], win_scr[:, c0:c0 + CHUNK],
                           preferred_element_type=f32)

        def a_kv(rh):
            kv = proj(K0, rh)
            k2 = kv[:, :D_KV].astype(bf16)
            kr = pltpu.roll(kv[:, :D_KV], HEAD_DIM, axis=1).astype(bf16)
            kvars = (jnp.where(lo, k2, zero), jnp.where(lo, zero, kr),
                     jnp.where(lo, kr, zero), jnp.where(lo, zero, k2))
            for var, kk in enumerate(kvars):
                k_scr[pa, var, BLOCK + rh * HR:BLOCK + (rh + 1) * HR, :] = kk
                if rh == 0:
                    k_scr[pb, var, BLOCK + T:, :] = kk[:BLOCK]
            v2 = kv[:, D_KV:].astype(bf16)
            vr = pltpu.roll(kv[:, D_KV:], HEAD_DIM, axis=1).astype(bf16)
            vvars = ((jnp.where(lo, v2, zero), jnp.where(lo, zero, vr)),
                     (jnp.where(lo, vr, zero), jnp.where(lo, zero, v2)))
            for j in range(N_KV_HEADS):
                for par in range(2):
                    vv = vvars[j][par]
                    for kb in range(RB):
                        r0 = (rh * RB + kb + 1) * 2 * BLOCK + par * BLOCK
                        v_scr[pa, j, r0:r0 + BLOCK, :] = vv[kb * BLOCK:(kb + 1) * BLOCK]
                    if rh == 0:
                        r0 = (R + 1) * 2 * BLOCK + par * BLOCK
                        v_scr[pb, j, r0:r0 + BLOCK, :] = vv[:BLOCK]

        def a_q(c, rh):
            cs = slice(c * CHUNK, (c + 1) * CHUNK)
            q_scr[pa, rh * HR:(rh + 1) * HR, cs] = (proj(Q0 + cs.start, rh) * (HEAD_DIM ** -0.5 * LOG2E)).astype(bf16)

        def a_gate(c0, dst, c, rh):
            cs = slice(c * CHUNK, (c + 1) * CHUNK)
            dst[pa, rh * HR:(rh + 1) * HR, cs] = jax.nn.silu(proj(c0 + cs.start, rh))

        def a_u(c, rh):
            cs = slice(c * CHUNK, (c + 1) * CHUNK)
            u = proj(U0 + cs.start, rh)
            u_scr[pa, POOL_HALO + rh * HR:POOL_HALO + (rh + 1) * HR, cs] = u
            if rh == 0:
                u_scr[pb, POOL_HALO + T:, cs] = jnp.where(b_is_last, 0.0, u[:POOL_HALO])

        units = [(b, j) for b in range(R) for j in range(N_KV_HEADS)]
        assert len(units) == 8, "the program order below is written for eight attention units per tile"

        def b_scores(n):
            b, j = units[n]
            rows = slice(b * BLOCK, (b + 1) * BLOCK)
            win = slice(b * BLOCK, b * BLOCK + 3 * BLOCK)
            c0 = j * Q_PER_KV * HEAD_DIM
            q2 = jnp.concatenate([q_scr[pb, rows, c0:c0 + LANES],
                                  q_scr[pb, rows, c0 + LANES:c0 + 2 * LANES]], axis=0)
            return [lax.dot_general(q2, k_scr[pb, 2 * j + par, win, :], _NT, preferred_element_type=f32)
                    for par in range(2)]

        def b_soft(n, s2s, par, pd, sink_term):
            b, j = units[n]
            if b == 0:
                tbl = jnp.where(b_is_first, 1, 0)
            elif b == R - 1:
                tbl = jnp.where(b_is_last, 2, 0)
            else:
                tbl = 0
            for half in range(2):
                h = j * Q_PER_KV + 2 * half + par
                s = s2s[par][half * BLOCK:(half + 1) * BLOCK, :] + bias_scr[tbl, h]
                sink = sink_ref[0, h] * LOG2E
                m = jnp.maximum(jnp.max(s, axis=-1, keepdims=True), sink)
                pd[(half, par)] = jnp.exp2(s - m).astype(bf16)
                sink_term[(half, par)] = jnp.exp2(sink - m)

        def b_pv(n, pd, sink_term):
            b, j = units[n]
            rows = slice(b * BLOCK, (b + 1) * BLOCK)
            c0 = j * Q_PER_KV * HEAD_DIM
            pcat = jnp.concatenate(
                [jnp.concatenate([pd[(half, par)][:, kb * BLOCK:(kb + 1) * BLOCK]
                                  for kb in range(3) for par in range(2)], axis=1)
                 for half in range(2)], axis=0)
            vwin = jnp.concatenate([v_scr[pb, j, 2 * b * BLOCK:2 * b * BLOCK + 6 * BLOCK, :], ones_scr[...]],
                                   axis=1)
            o2 = jnp.dot(pcat, vwin, preferred_element_type=f32)
            for half in range(2):
                hr = slice(half * BLOCK, (half + 1) * BLOCK)
                denom = o2[hr, LANES:] + jnp.where(lo, sink_term[(half, 0)], sink_term[(half, 1)])
                cs = c0 + half * LANES
                a = (o2[hr, :LANES] / denom) * sga_scr[pb, rows, cs:cs + LANES]
                mix_scr[pb, rows, cs:cs + LANES] = a.astype(bf16)

        def b_pool_y(gi):
            w = POOL_WINDOWS[gi]
            seq = n_tiles * T
            E = POOL_HALO
            left = w // 2
            right = w - 1 - left
            cols = slice(gi * POOL_GROUP, (gi + 1) * POOL_GROUP)
            ug = u_scr[pb, :, cols]
            wsum = _window_sums(ug, w)[E:E + T]
            uc = ug[E:E + T]
            y_mid = wsum * (1.0 / w) - uc
            parts = []
            for r0 in (0, T - E):
                t_glob = tile_b * T + r0 + lax.broadcasted_iota(jnp.int32, (E, LANES), 0)
                cnt = (jnp.minimum(t_glob + right + 1, seq) - jnp.maximum(t_glob - left, 0)).astype(f32)
                parts.append(wsum[r0:r0 + E] / cnt - uc[r0:r0 + E])
            return jnp.concatenate([parts[0], y_mid[E:T - E], parts[1]], axis=0).astype(bf16)

        def b_pool_pair(pr):
            cols = slice(pr * CHUNK, (pr + 1) * CHUNK)
            y2 = jnp.concatenate([b_pool_y(2 * pr), b_pool_y(2 * pr + 1)], axis=1)
            yw = jnp.dot(y2, pw2_scr[pr], preferred_element_type=f32)
            pz = (yw * pscale_ref[:, cols]) * sgp_scr[pb, :, cols]
            mix_scr[pb, :, D_ATTN + pr * CHUNK:D_ATTN + (pr + 1) * CHUNK] = pz.astype(bf16)

        n_cq = D // CHUNK

        def c_out(rh, cq):
            m = jnp.dot(mix_scr[pa, rh * HR:(rh + 1) * HR, :], wout_scr[:, cq * CHUNK:(cq + 1) * CHUNK],
                        preferred_element_type=f32)
            return m, jnp.sum(m * m, axis=-1, keepdims=True)

        def c_post(rh, rb, quarters):
            blk = slice(rb * BLOCK, (rb + 1) * BLOCK)
            rows = slice(rh * HR + rb * BLOCK, rh * HR + (rb + 1) * BLOCK)
            mixed = jnp.concatenate([m[blk] for m, _ in quarters], axis=1)
            ssq = quarters[0][1][blk]
            for _, part_ssq in quarters[1:]:
                ssq = ssq + part_ssq[blk]
            y = (mixed * lax.rsqrt(ssq * (1.0 / D) + EPS)) * gpost_ref[...]
            out_ref[rows, :] = xb_ref[rows, :] + y

        PN = BLOCK // 2

        def prenorm(r8):
            rows = slice(r8 * PN, (r8 + 1) * PN)
            xn_scr[pb, rows, :] = _rmsnorm_rows(xa_ref[rows, :], gpre_ref[...]).astype(bf16)

        mixed_q = {0: [], 1: []}
        part = functools.partial
        out_q = lambda rh, cq: ("c", lambda: mixed_q[rh].append(c_out(rh, cq)))
        ga = lambda c, rh: ("a", part(a_gate, GA0, sga_scr, c, rh))
        gp = lambda c, rh: ("a", part(a_gate, GP0, sgp_scr, c, rh))
        uu = lambda c, rh: ("a", part(a_u, c, rh))
        qq = lambda c, rh: ("a", part(a_q, c, rh))
        heavy = [out_q(0, 0), out_q(0, 1), out_q(0, 2),
                 out_q(0, 3), uu(0, 0), uu(0, 1),
                 uu(1, 0), uu(1, 1), ga(0, 0),
                 ga(0, 1), qq(0, 0), qq(0, 1),
                 out_q(1, 0), out_q(1, 1), out_q(1, 2),
                 out_q(1, 3), ga(1, 0), ga(1, 1),
                 gp(0, 0), gp(0, 1), qq(1, 0),
                 gp(1, 0), gp(1, 1), qq(1, 1)]
        assert len(heavy) == 3 * len(units)
        vector_only = {0: [("n", part(prenorm, 0))],
                       1: [("n", part(prenorm, 1)), ("c", lambda: c_post(0, 0, mixed_q[0]))],
                       2: [("n", part(prenorm, 2)), ("c", lambda: c_post(0, 1, mixed_q[0]))],
                       3: [("n", part(prenorm, 3)), ("b", part(b_pool_pair, 0))],
                       4: [("n", part(prenorm, 4)), ("b", part(b_pool_pair, 1))],
                       5: [("n", part(prenorm, 5)), ("c", lambda: c_post(1, 0, mixed_q[1]))],
                       6: [("n", part(prenorm, 6)), ("c", lambda: c_post(1, 1, mixed_q[1]))],
                       7: [("n", part(prenorm, 7))]}
        assert T // PN == len(units)

        def run(piece):
            stage, fn = piece
            if enabled[stage]:
                fn()

        if do_a:
            a_kv(0)
            a_kv(1)
        elif do_b:
            u_scr[pb, POOL_HALO + T:, :] = jnp.zeros((POOL_HALO, D_POOL), f32)
        s = {0: b_scores(0)} if do_b else {}
        for n in range(len(units)):
            if do_b and n + 1 < len(units):
                s[n + 1] = b_scores(n + 1)
            pd, sink_term = {}, {}
            run(heavy[3 * n])
            if do_b:
                b_soft(n, s[n], 0, pd, sink_term)
            run(heavy[3 * n + 1])
            if do_b:
                b_soft(n, s.pop(n), 1, pd, sink_term)
            run(heavy[3 * n + 2])
            if do_b:
                b_pv(n, pd, sink_term)
            for piece in vector_only.get(n, ()):
                run(piece)

        if do_a:
            k_scr[pb, :, 0:BLOCK, :] = k_scr[pa, :, T:T + BLOCK, :]
            v_scr[pb, :, 0:2 * BLOCK, :] = v_scr[pa, :, R * 2 * BLOCK:(R + 1) * 2 * BLOCK, :]
            u_scr[pb, 0:POOL_HALO, :] = u_scr[pa, T:T + POOL_HALO, :]

    steady = (i >= 2) & (i < n_tiles)
    for pa in range(2):
        @pl.when(steady & (i % 2 == pa))
        def _(pa=pa):
            run_step(pa, 1 - pa)

    @pl.when(i == 0)
    def _():
        run_step(0, 1, do_b=False, do_c=False)

    @pl.when(i == 1)
    def _():
        run_step(1, 0, do_c=False)

    @pl.when(i == n_tiles)
    def _():
        run_step(n_tiles % 2, 1 - n_tiles % 2, do_a=False, do_prenorm=False)

    @pl.when(i == n_tiles + 1)
    def _():
        run_step((n_tiles + 1) % 2, n_tiles % 2, do_a=False, do_b=False, do_prenorm=False)


def kernel(x, pre_norm_g, w_in, rel_bias, attn_sink, pool_w, pool_scale, w_out, post_norm_g):
    B, S, D = x.shape
    assert B == 1 and D == D_MODEL and S % TILE == 0 and S // TILE >= 3
    assert pre_norm_g.shape == (1, D) and attn_sink.shape == (1, N_Q_HEADS), "single layer"
    assert rel_bias.shape == (N_BUCKETS, N_Q_HEADS) and rel_bias.dtype == attn_sink.dtype == jnp.float32
    T = TILE
    R = T // BLOCK
    n_tiles = S // T
    bf16 = jnp.bfloat16

    x2 = x.reshape(S, D)
    smem = pl.BlockSpec(memory_space=pltpu.SMEM)
    full = lambda shape: pl.BlockSpec(shape, lambda i: (0,) * len(shape))

    out = pl.pallas_call(
        functools.partial(_layer_kernel, n_tiles=n_tiles),
        grid=(n_tiles + 2,),
        in_specs=[
            smem, smem,
            pl.BlockSpec((T, D), lambda i: (jnp.minimum(i + 1, n_tiles - 1), 0)),
            pl.BlockSpec((T, D), lambda i: (jnp.maximum(i - 2, 0), 0)),
            full((1, D)),
            full((D, D_IN)),
            full((len(POOL_WINDOWS), POOL_GROUP, POOL_GROUP)),
            full((1, D_POOL)),
            full((D, D)),
            full((1, D)),
        ],
        out_specs=pl.BlockSpec((T, D), lambda i: (jnp.maximum(i - 2, 0), 0)),
        out_shape=jax.ShapeDtypeStruct((S, D), x.dtype),
        scratch_shapes=[
            pltpu.VMEM((3, N_Q_HEADS, BLOCK, 3 * BLOCK), jnp.float32),
            pltpu.VMEM((2, T, D_ATTN), bf16),
            pltpu.VMEM((2, 4, T + 2 * BLOCK, LANES), bf16),
            pltpu.VMEM((2, N_KV_HEADS, (R + 2) * 2 * BLOCK, LANES), bf16),
            pltpu.VMEM((2, T + 2 * POOL_HALO, D_POOL), jnp.float32),
            pltpu.VMEM((2, T, D_ATTN), jnp.float32),
            pltpu.VMEM((2, T, D_POOL), jnp.float32),
            pltpu.VMEM((2, T, D), bf16),
            pltpu.VMEM((2, T, D), bf16),
            pltpu.VMEM((len(POOL_WINDOWS) // 2, CHUNK, CHUNK), bf16),
            pltpu.VMEM((3 * 2 * BLOCK, LANES), bf16),
            pltpu.VMEM((D, D), bf16),
            pltpu.VMEM((D, D_IN), bf16),
        ],
        compiler_params=pltpu.CompilerParams(
            dimension_semantics=("arbitrary",),
            vmem_limit_bytes=VMEM_LIMIT_BYTES,
        ),
        name="hymba_layer_fused",
    )(
        rel_bias, attn_sink,
        x2, x2,
        pre_norm_g,
        w_in.reshape(D, D_IN),
        pool_w.reshape(len(POOL_WINDOWS), POOL_GROUP, POOL_GROUP),
        pool_scale,
        w_out.reshape(D, D),
        post_norm_g,
    )
    return out.reshape(B, S, D)
```

```python
import functools
import math

import jax
import jax.numpy as jnp
from jax import lax
from jax.experimental import pallas as pl
from jax.experimental.pallas import tpu as pltpu

D_MODEL = 1024
D_ATTN = 512
D_POOL = 512
HEAD_DIM = 64
N_Q_HEADS = 8
N_KV_HEADS = 2
Q_PER_KV = 4
D_KV = 128
WINDOW = 128
BLOCK = 128
N_BUCKETS = 32
POOL_WINDOWS = (2, 4, 8, 16)
POOL_GROUP = 128
D_IN = 2304
EPS = 1e-6
NEG = -1e30
LOG2E = math.log2(math.e)

LANES = 128
CHUNK = 2 * LANES
POOL_HALO = 16
TILE = 512
VMEM_LIMIT_BYTES = 62 * 1024 * 1024

Q0, K0, V0, GA0, U0, GP0 = 0, 512, 640, 768, 1280, 1792

_NT = (((1,), (1,)), ((), ()))


def _rmsnorm_rows(xv, g):
    ms = jnp.mean(xv * xv, axis=-1, keepdims=True)
    return (xv * lax.rsqrt(ms + EPS)) * g


def _t5_bucket_exact(rel):
    nb = N_BUCKETS // 2
    max_exact = nb // 2
    ret = jnp.where(rel > 0, nb, 0)
    n = jnp.abs(rel)
    n2 = n * n
    large = jnp.full(rel.shape, max_exact, jnp.int32)
    for j in range(1, nb - max_exact):
        large = large + jnp.where(n2 >= (max_exact * max_exact) << j, 1, 0)
    return ret + jnp.where(n < max_exact, n, large)


def _window_sums(ug, w):
    n = ug.shape[0]
    if w == 2:
        return ug + pltpu.roll(ug, 1, axis=0)
    acc = ug + pltpu.roll(ug, n - 1, axis=0)
    span = 2
    while span * 2 < w:
        acc = acc + pltpu.roll(acc, n - span, axis=0)
        span *= 2
    return acc + pltpu.roll(acc, span, axis=0)


def _layer_kernel(relb_ref, sink_ref,
                  xa_ref, xb_ref, gpre_ref, win_ref, poolw_ref, pscale_ref, wout_ref, gpost_ref,
                  out_ref,
                  bias_scr, q_scr, k_scr, v_scr, u_scr, sga_scr, sgp_scr, mix_scr, xn_scr, pw2_scr, ones_scr,
                  wout_scr, win_scr,
                  *, n_tiles):
    i = pl.program_id(0)
    T = TILE
    R = T // BLOCK
    D = D_MODEL
    f32, bf16 = jnp.float32, jnp.bfloat16
    lane = lax.broadcasted_iota(jnp.int32, (1, LANES), 1)
    lo = lane < HEAD_DIM

    @pl.when(i == 0)
    def _():
        qi = lax.broadcasted_iota(jnp.int32, (BLOCK, 3 * BLOCK), 0)
        kj = lax.broadcasted_iota(jnp.int32, (BLOCK, 3 * BLOCK), 1)
        rel = (kj - BLOCK) - qi
        bucket = _t5_bucket_exact(rel)
        in_band = jnp.abs(rel) <= WINDOW
        for h in range(N_Q_HEADS):
            def body(b, acc, h=h):
                return jnp.where(bucket == b, relb_ref[b, h], acc)
            tbl = lax.fori_loop(0, N_BUCKETS, body, jnp.zeros((BLOCK, 3 * BLOCK), f32)) * LOG2E
            bias_scr[0, h] = jnp.where(in_band, tbl, NEG)
            bias_scr[1, h] = jnp.where(in_band & (kj >= BLOCK), tbl, NEG)
            bias_scr[2, h] = jnp.where(in_band & (kj < 2 * BLOCK), tbl, NEG)
        k_scr[0, :, 0:BLOCK, :] = jnp.zeros((4, BLOCK, LANES), bf16)
        v_scr[0, :, 0:2 * BLOCK, :] = jnp.zeros((N_KV_HEADS, 2 * BLOCK, LANES), bf16)
        u_scr[0, 0:POOL_HALO, :] = jnp.zeros((POOL_HALO, D_POOL), f32)
        pw2_scr[...] = jnp.zeros(pw2_scr.shape, bf16)
        for gi in range(len(POOL_WINDOWS)):
            r0 = (gi % 2) * POOL_GROUP
            pw2_scr[gi // 2, r0:r0 + POOL_GROUP, r0:r0 + POOL_GROUP] = poolw_ref[gi].astype(bf16)
        row_lo = lax.broadcasted_iota(jnp.int32, (2 * BLOCK, LANES), 0) < BLOCK
        ones_pat = jnp.where(row_lo, jnp.where(lo, 1.0, 0.0), jnp.where(lo, 0.0, 1.0)).astype(bf16)
        for kb in range(3):
            ones_scr[kb * 2 * BLOCK:(kb + 1) * 2 * BLOCK, :] = ones_pat
        for c in range(D_IN // CHUNK):
            win_scr[:, c * CHUNK:(c + 1) * CHUNK] = win_ref[:, c * CHUNK:(c + 1) * CHUNK].astype(bf16)
        for c in range(D // CHUNK):
            wout_scr[:, c * CHUNK:(c + 1) * CHUNK] = wout_ref[:, c * CHUNK:(c + 1) * CHUNK].astype(bf16)

    b_is_first = i == 1
    b_is_last = i == n_tiles
    tile_b = jnp.clip(i - 1, 0, n_tiles - 1)

    def run_step(pa, pb, do_a=True, do_b=True, do_c=True):
        enabled = {"a": do_a, "b": do_b, "c": do_c}
        zero = jnp.zeros((), bf16)

        HR = T // 2
        RB = HR // BLOCK

        def proj(c0, rh):
            return jnp.dot(xn_scr[rh * HR:(rh + 1) * HR, :], win_scr[:, c0:c0 + CHUNK],
                           preferred_element_type=f32)

        def a_kv(rh):
            kv = proj(K0, rh)
            k2 = kv[:, :D_KV].astype(bf16)
            kr = pltpu.roll(kv[:, :D_KV], HEAD_DIM, axis=1).astype(bf16)
            kvars = (jnp.where(lo, k2, zero), jnp.where(lo, zero, kr),
                     jnp.where(lo, kr, zero), jnp.where(lo, zero, k2))
            for var, kk in enumerate(kvars):
                k_scr[pa, var, BLOCK + rh * HR:BLOCK + (rh + 1) * HR, :] = kk
                if rh == 0:
                    k_scr[pb, var, BLOCK + T:, :] = kk[:BLOCK]
            v2 = kv[:, D_KV:].astype(bf16)
            vr = pltpu.roll(kv[:, D_KV:], HEAD_DIM, axis=1).astype(bf16)
            vvars = ((jnp.where(lo, v2, zero), jnp.where(lo, zero, vr)),
                     (jnp.where(lo, vr, zero), jnp.where(lo, zero, v2)))
            for j in range(N_KV_HEADS):
                for par in range(2):
                    vv = vvars[j][par]
                    for kb in range(RB):
                        r0 = (rh * RB + kb + 1) * 2 * BLOCK + par * BLOCK
                        v_scr[pa, j, r0:r0 + BLOCK, :] = vv[kb * BLOCK:(kb + 1) * BLOCK]
                    if rh == 0:
                        r0 = (R + 1) * 2 * BLOCK + par * BLOCK
                        v_scr[pb, j, r0:r0 + BLOCK, :] = vv[:BLOCK]

        def a_q(c, rh):
            cs = slice(c * CHUNK, (c + 1) * CHUNK)
            q_scr[pa, rh * HR:(rh + 1) * HR, cs] = (proj(Q0 + cs.start, rh) * (HEAD_DIM ** -0.5 * LOG2E)).astype(bf16)

        def a_gate(c0, dst, c, rh):
            cs = slice(c * CHUNK, (c + 1) * CHUNK)
            dst[pa, rh * HR:(rh + 1) * HR, cs] = jax.nn.silu(proj(c0 + cs.start, rh))

        def a_u(c, rh):
            cs = slice(c * CHUNK, (c + 1) * CHUNK)
            u = proj(U0 + cs.start, rh)
            u_scr[pa, POOL_HALO + rh * HR:POOL_HALO + (rh + 1) * HR, cs] = u
            if rh == 0:
                u_scr[pb, POOL_HALO + T:, cs] = jnp.where(b_is_last, 0.0, u[:POOL_HALO])

        units = [(b, j) for b in range(R) for j in range(N_KV_HEADS)]
        assert len(units) == 8, "the program order below is written for eight attention units per tile"

        def b_scores(n):
            b, j = units[n]
            rows = slice(b * BLOCK, (b + 1) * BLOCK)
            win = slice(b * BLOCK, b * BLOCK + 3 * BLOCK)
            c0 = j * Q_PER_KV * HEAD_DIM
            q2 = jnp.concatenate([q_scr[pb, rows, c0:c0 + LANES],
                                  q_scr[pb, rows, c0 + LANES:c0 + 2 * LANES]], axis=0)
            return [lax.dot_general(q2, k_scr[pb, 2 * j + par, win, :], _NT, preferred_element_type=f32)
                    for par in range(2)]

        def b_soft(n, s2s, par, pd, sink_term):
            b, j = units[n]
            if b == 0:
                tbl = jnp.where(b_is_first, 1, 0)
            elif b == R - 1:
                tbl = jnp.where(b_is_last, 2, 0)
            else:
                tbl = 0
            for half in range(2):
                h = j * Q_PER_KV + 2 * half + par
                s = s2s[par][half * BLOCK:(half + 1) * BLOCK, :] + bias_scr[tbl, h]
                sink = sink_ref[0, h] * LOG2E
                m = jnp.maximum(jnp.max(s, axis=-1, keepdims=True), sink)
                pd[(half, par)] = jnp.exp2(s - m).astype(bf16)
                sink_term[(half, par)] = jnp.exp2(sink - m)

        def b_pv(n, pd, sink_term):
            b, j = units[n]
            rows = slice(b * BLOCK, (b + 1) * BLOCK)
            c0 = j * Q_PER_KV * HEAD_DIM
            pcat = jnp.concatenate(
                [jnp.concatenate([pd[(half, par)][:, kb * BLOCK:(kb + 1) * BLOCK]
                                  for kb in range(3) for par in range(2)], axis=1)
                 for half in range(2)], axis=0)
            vwin = jnp.concatenate([v_scr[pb, j, 2 * b * BLOCK:2 * b * BLOCK + 6 * BLOCK, :], ones_scr[...]],
                                   axis=1)
            o2 = jnp.dot(pcat, vwin, preferred_element_type=f32)
            for half in range(2):
                hr = slice(half * BLOCK, (half + 1) * BLOCK)
                denom = o2[hr, LANES:] + jnp.where(lo, sink_term[(half, 0)], sink_term[(half, 1)])
                cs = c0 + half * LANES
                a = (o2[hr, :LANES] / denom) * sga_scr[pb, rows, cs:cs + LANES]
                mix_scr[pb, rows, cs:cs + LANES] = a.astype(bf16)

        def b_pool_y(gi):
            w = POOL_WINDOWS[gi]
            seq = n_tiles * T
            E = POOL_HALO
            left = w // 2
            right = w - 1 - left
            cols = slice(gi * POOL_GROUP, (gi + 1) * POOL_GROUP)
            ug = u_scr[pb, :, cols]
            wsum = _window_sums(ug, w)[E:E + T]
            uc = ug[E:E + T]
            y_mid = wsum * (1.0 / w) - uc
            parts = []
            for r0 in (0, T - E):
                t_glob = tile_b * T + r0 + lax.broadcasted_iota(jnp.int32, (E, LANES), 0)
                cnt = (jnp.minimum(t_glob + right + 1, seq) - jnp.maximum(t_glob - left, 0)).astype(f32)
                parts.append(wsum[r0:r0 + E] / cnt - uc[r0:r0 + E])
            return jnp.concatenate([parts[0], y_mid[E:T - E], parts[1]], axis=0).astype(bf16)

        def b_pool_pair(pr):
            cols = slice(pr * CHUNK, (pr + 1) * CHUNK)
            y2 = jnp.concatenate([b_pool_y(2 * pr), b_pool_y(2 * pr + 1)], axis=1)
            yw = jnp.dot(y2, pw2_scr[pr], preferred_element_type=f32)
            pz = (yw * pscale_ref[:, cols]) * sgp_scr[pb, :, cols]
            mix_scr[pb, :, D_ATTN + pr * CHUNK:D_ATTN + (pr + 1) * CHUNK] = pz.astype(bf16)

        n_cq = D // CHUNK

        def c_out(rh, cq):
            m = jnp.dot(mix_scr[pa, rh * HR:(rh + 1) * HR, :], wout_scr[:, cq * CHUNK:(cq + 1) * CHUNK],
                        preferred_element_type=f32)
            return m, jnp.sum(m * m, axis=-1, keepdims=True)

        def c_post(rh, rb, quarters):
            blk = slice(rb * BLOCK, (rb + 1) * BLOCK)
            rows = slice(rh * HR + rb * BLOCK, rh * HR + (rb + 1) * BLOCK)
            mixed = jnp.concatenate([m[blk] for m, _ in quarters], axis=1)
            ssq = quarters[0][1][blk]
            for _, part_ssq in quarters[1:]:
                ssq = ssq + part_ssq[blk]
            y = (mixed * lax.rsqrt(ssq * (1.0 / D) + EPS)) * gpost_ref[...]
            out_ref[rows, :] = xb_ref[rows, :] + y

        PN = BLOCK // 2

        def prenorm(r8):
            rows = slice(r8 * PN, (r8 + 1) * PN)
            xn_scr[rows, :] = _rmsnorm_rows(xa_ref[rows, :], gpre_ref[...]).astype(bf16)

        mixed_q = {0: [], 1: []}
        part = functools.partial
        out_q = lambda rh, cq: ("c", lambda: mixed_q[rh].append(c_out(rh, cq)))
        ga = lambda c, rh: ("a", part(a_gate, GA0, sga_scr, c, rh))
        gp = lambda c, rh: ("a", part(a_gate, GP0, sgp_scr, c, rh))
        uu = lambda c, rh: ("a", part(a_u, c, rh))
        qq = lambda c, rh: ("a", part(a_q, c, rh))
        kv = lambda rh: ("a", part(a_kv, rh))
        pn = lambda r8: ("a", part(prenorm, r8))
        heavy = [out_q(0, 0), out_q(0, 1), out_q(0, 2),
                 out_q(0, 3), out_q(1, 0), out_q(1, 1),
                 out_q(1, 2), out_q(1, 3), uu(0, 0),
                 uu(1, 0), uu(0, 1), uu(1, 1),
                 qq(0, 0), qq(0, 1), qq(1, 0),
                 qq(1, 1), ga(0, 0), ga(0, 1),
                 ga(1, 0), ga(1, 1), gp(0, 0),
                 gp(0, 1), gp(1, 0), gp(1, 1)]
        assert len(heavy) == 3 * len(units)
        slot_end = {0: [pn(4), pn(5), kv(0)],
                    1: [pn(6), pn(7), kv(1), ("c", lambda: c_post(0, 0, mixed_q[0]))],
                    2: [("c", lambda: c_post(0, 1, mixed_q[0]))],
                    3: [("c", lambda: c_post(1, 0, mixed_q[1])), ("b", part(b_pool_pair, 0))],
                    4: [("c", lambda: c_post(1, 1, mixed_q[1])), ("b", part(b_pool_pair, 1))]}
        assert T // PN == 8

        def run(piece):
            stage, fn = piece
            if enabled[stage]:
                fn()

        if do_a:
            for r8 in range(4):
                prenorm(r8)
        elif do_b:
            u_scr[pb, POOL_HALO + T:, :] = jnp.zeros((POOL_HALO, D_POOL), f32)
        s = {0: b_scores(0)} if do_b else {}
        for n in range(len(units)):
            if do_b and n + 1 < len(units):
                s[n + 1] = b_scores(n + 1)
            pd, sink_term = {}, {}
            run(heavy[3 * n])
            if do_b:
                b_soft(n, s[n], 0, pd, sink_term)
            run(heavy[3 * n + 1])
            if do_b:
                b_soft(n, s.pop(n), 1, pd, sink_term)
            run(heavy[3 * n + 2])
            if do_b:
                b_pv(n, pd, sink_term)
            for piece in slot_end.get(n, ()):
                run(piece)

        if do_a:
            k_scr[pb, :, 0:BLOCK, :] = k_scr[pa, :, T:T + BLOCK, :]
            v_scr[pb, :, 0:2 * BLOCK, :] = v_scr[pa, :, R * 2 * BLOCK:(R + 1) * 2 * BLOCK, :]
            u_scr[pb, 0:POOL_HALO, :] = u_scr[pa, T:T + POOL_HALO, :]

    steady = (i >= 2) & (i < n_tiles)
    for pa in range(2):
        @pl.when(steady & (i % 2 == pa))
        def _(pa=pa):
            run_step(pa, 1 - pa)

    @pl.when(i == 0)
    def _():
        run_step(0, 1, do_b=False, do_c=False)

    @pl.when(i == 1)
    def _():
        run_step(1, 0, do_c=False)

    @pl.when(i == n_tiles)
    def _():
        run_step(n_tiles % 2, 1 - n_tiles % 2, do_a=False)

    @pl.when(i == n_tiles + 1)
    def _():
        run_step((n_tiles + 1) % 2, n_tiles % 2, do_a=False, do_b=False)


def kernel(x, pre_norm_g, w_in, rel_bias, attn_sink, pool_w, pool_scale, w_out, post_norm_g):
    B, S, D = x.shape
    assert B == 1 and D == D_MODEL and S % TILE == 0 and S // TILE >= 3
    assert pre_norm_g.shape == (1, D) and attn_sink.shape == (1, N_Q_HEADS), "single layer"
    assert rel_bias.shape == (N_BUCKETS, N_Q_HEADS) and rel_bias.dtype == attn_sink.dtype == jnp.float32
    T = TILE
    R = T // BLOCK
    n_tiles = S // T
    bf16 = jnp.bfloat16

    x2 = x.reshape(S, D)
    smem = pl.BlockSpec(memory_space=pltpu.SMEM)
    full = lambda shape: pl.BlockSpec(shape, lambda i: (0,) * len(shape))

    out = pl.pallas_call(
        functools.partial(_layer_kernel, n_tiles=n_tiles),
        grid=(n_tiles + 2,),
        in_specs=[
            smem, smem,
            pl.BlockSpec((T, D), lambda i: (jnp.minimum(i, n_tiles - 1), 0)),
            pl.BlockSpec((T, D), lambda i: (jnp.maximum(i - 2, 0), 0)),
            full((1, D)),
            full((D, D_IN)),
            full((len(POOL_WINDOWS), POOL_GROUP, POOL_GROUP)),
            full((1, D_POOL)),
            full((D, D)),
            full((1, D)),
        ],
        out_specs=pl.BlockSpec((T, D), lambda i: (jnp.maximum(i - 2, 0), 0)),
        out_shape=jax.ShapeDtypeStruct((S, D), x.dtype),
        scratch_shapes=[
            pltpu.VMEM((3, N_Q_HEADS, BLOCK, 3 * BLOCK), jnp.float32),
            pltpu.VMEM((2, T, D_ATTN), bf16),
            pltpu.VMEM((2, 4, T + 2 * BLOCK, LANES), bf16),
            pltpu.VMEM((2, N_KV_HEADS, (R + 2) * 2 * BLOCK, LANES), bf16),
            pltpu.VMEM((2, T + 2 * POOL_HALO, D_POOL), jnp.float32),
            pltpu.VMEM((2, T, D_ATTN), jnp.float32),
            pltpu.VMEM((2, T, D_POOL), jnp.float32),
            pltpu.VMEM((2, T, D), bf16),
            pltpu.VMEM((T, D), bf16),
            pltpu.VMEM((len(POOL_WINDOWS) // 2, CHUNK, CHUNK), bf16),
            pltpu.VMEM((3 * 2 * BLOCK, LANES), bf16),
            pltpu.VMEM((D, D), bf16),
            pltpu.VMEM((D, D_IN), bf16),
        ],
        compiler_params=pltpu.CompilerParams(
            dimension_semantics=("arbitrary",),
            vmem_limit_bytes=VMEM_LIMIT_BYTES,
        ),
        name="hymba_layer_fused",
    )(
        rel_bias, attn_sink,
        x2, x2,
        pre_norm_g,
        w_in.reshape(D, D_IN),
        pool_w.reshape(len(POOL_WINDOWS), POOL_GROUP, POOL_GROUP),
        pool_scale,
        w_out.reshape(D, D),
        post_norm_g,
    )
    return out.reshape(B, S, D)
```

```python
import functools
import math

import jax
import jax.numpy as jnp
from jax import lax
from jax.experimental import pallas as pl
from jax.experimental.pallas import tpu as pltpu

D_MODEL = 1024
D_ATTN = 512
D_POOL = 512
HEAD_DIM = 64
N_Q_HEADS = 8
N_KV_HEADS = 2
Q_PER_KV = 4
D_KV = 128
WINDOW = 128
BLOCK = 128
N_BUCKETS = 32
POOL_WINDOWS = (2, 4, 8, 16)
POOL_GROUP = 128
D_IN = 2304
EPS = 1e-6
NEG = -1e30
LOG2E = math.log2(math.e)

LANES = 128
CHUNK = 2 * LANES
POOL_HALO = 16
TILE = 512
VMEM_LIMIT_BYTES = 62 * 1024 * 1024

Q0, K0, V0, GA0, U0, GP0 = 0, 512, 640, 768, 1280, 1792

_NT = (((1,), (1,)), ((), ()))


def _silu(x):
    h = 0.5 * x
    return h + h * jnp.tanh(h)


def _t5_bucket_exact(rel):
    nb = N_BUCKETS // 2
    max_exact = nb // 2
    ret = jnp.where(rel > 0, nb, 0)
    n = jnp.abs(rel)
    n2 = n * n
    large = jnp.full(rel.shape, max_exact, jnp.int32)
    for j in range(1, nb - max_exact):
        large = large + jnp.where(n2 >= (max_exact * max_exact) << j, 1, 0)
    return ret + jnp.where(n < max_exact, n, large)


def _window_sums(ug, w):
    n = ug.shape[0]
    if w == 2:
        return ug + pltpu.roll(ug, 1, axis=0)
    acc = ug + pltpu.roll(ug, n - 1, axis=0)
    span = 2
    while span * 2 < w:
        acc = acc + pltpu.roll(acc, n - span, axis=0)
        span *= 2
    return acc + pltpu.roll(acc, span, axis=0)


def _layer_kernel(relb_ref, sink_ref,
                  xa_ref, xb_ref, gpre_ref, win_ref, poolw_ref, pscale_ref, wout_ref, gpost_ref,
                  out_ref,
                  bias_scr, q_scr, k_scr, v_scr, u_scr, sga_scr, sgp_scr, mix_scr, xn_scr, pw2_scr, ones_scr,
                  wout_scr, win_scr,
                  *, n_tiles):
    i = pl.program_id(0)
    T = TILE
    R = T // BLOCK
    D = D_MODEL
    f32, bf16 = jnp.float32, jnp.bfloat16
    lane = lax.broadcasted_iota(jnp.int32, (1, LANES), 1)
    lo = lane < HEAD_DIM

    @pl.when(i == 0)
    def _():
        qi = lax.broadcasted_iota(jnp.int32, (BLOCK, 3 * BLOCK), 0)
        kj = lax.broadcasted_iota(jnp.int32, (BLOCK, 3 * BLOCK), 1)
        rel = (kj - BLOCK) - qi
        bucket = _t5_bucket_exact(rel)
        in_band = jnp.abs(rel) <= WINDOW
        for h in range(N_Q_HEADS):
            def body(b, acc, h=h):
                return jnp.where(bucket == b, relb_ref[b, h], acc)
            tbl = lax.fori_loop(0, N_BUCKETS, body, jnp.zeros((BLOCK, 3 * BLOCK), f32)) * LOG2E
            bias_scr[0, h] = jnp.where(in_band, tbl, NEG)
            bias_scr[1, h] = jnp.where(in_band & (kj >= BLOCK), tbl, NEG)
            bias_scr[2, h] = jnp.where(in_band & (kj < 2 * BLOCK), tbl, NEG)
        k_scr[0, :, 0:BLOCK, :] = jnp.zeros((4, BLOCK, LANES), bf16)
        v_scr[0, :, 0:2 * BLOCK, :] = jnp.zeros((N_KV_HEADS, 2 * BLOCK, LANES), bf16)
        u_scr[0, 0:POOL_HALO, :] = jnp.zeros((POOL_HALO, D_POOL), f32)
        pw2_scr[...] = jnp.zeros(pw2_scr.shape, bf16)
        for gi in range(len(POOL_WINDOWS)):
            r0 = (gi % 2) * POOL_GROUP
            pw2_scr[gi // 2, r0:r0 + POOL_GROUP, r0:r0 + POOL_GROUP] = poolw_ref[gi].astype(bf16)
        row_lo = lax.broadcasted_iota(jnp.int32, (2 * BLOCK, LANES), 0) < BLOCK
        ones_pat = jnp.where(row_lo, jnp.where(lo, 1.0, 0.0), jnp.where(lo, 0.0, 1.0)).astype(bf16)
        for kb in range(3):
            ones_scr[kb * 2 * BLOCK:(kb + 1) * 2 * BLOCK, :] = ones_pat
        g_rows = jnp.broadcast_to(gpre_ref[...], (LANES, D))
        for kb in range(D // LANES):
            rows = slice(kb * LANES, (kb + 1) * LANES)
            g_col = jnp.transpose(g_rows[:, rows])
            g_col = jnp.concatenate([g_col, g_col], axis=1)
            for c in range(D_IN // CHUNK):
                cs = slice(c * CHUNK, (c + 1) * CHUNK)
                win_scr[rows, cs] = (win_ref[rows, cs] * g_col).astype(bf16)
        for c in range(D // CHUNK):
            wout_scr[:, c * CHUNK:(c + 1) * CHUNK] = wout_ref[:, c * CHUNK:(c + 1) * CHUNK].astype(bf16)

    b_is_first = i == 1
    b_is_last = i == n_tiles
    tile_b = jnp.clip(i - 1, 0, n_tiles - 1)

    def run_step(pa, pb, do_a=True, do_b=True, do_c=True):
        enabled = {"a": do_a, "b": do_b, "c": do_c}
        zero = jnp.zeros((), bf16)

        HR = T // 2
        RB = HR // BLOCK

        def proj(c0, rh):
            return jnp.dot(xn_scr[rh * HR:(rh + 1) * HR, :], win_scr[:, c0:c0 + CHUNK],
                           preferred_element_type=f32)

        def a_kv(rh):
            kv = proj(K0, rh)
            k2 = kv[:, :D_KV].astype(bf16)
            kr = pltpu.roll(kv[:, :D_KV], HEAD_DIM, axis=1).astype(bf16)
            kvars = (jnp.where(lo, k2, zero), jnp.where(lo, zero, kr),
                     jnp.where(lo, kr, zero), jnp.where(lo, zero, k2))
            for var, kk in enumerate(kvars):
                k_scr[pa, var, BLOCK + rh * HR:BLOCK + (rh + 1) * HR, :] = kk
                if rh == 0:
                    k_scr[pb, var, BLOCK + T:, :] = kk[:BLOCK]
            v2 = kv[:, D_KV:].astype(bf16)
            vr = pltpu.roll(kv[:, D_KV:], HEAD_DIM, axis=1).astype(bf16)
            vvars = ((jnp.where(lo, v2, zero), jnp.where(lo, zero, vr)),
                     (jnp.where(lo, vr, zero), jnp.where(lo, zero, v2)))
            for j in range(N_KV_HEADS):
                for par in range(2):
                    vv = vvars[j][par]
                    for kb in range(RB):
                        r0 = (rh * RB + kb + 1) * 2 * BLOCK + par * BLOCK
                        v_scr[pa, j, r0:r0 + BLOCK, :] = vv[kb * BLOCK:(kb + 1) * BLOCK]
                    if rh == 0:
                        r0 = (R + 1) * 2 * BLOCK + par * BLOCK
                        v_scr[pb, j, r0:r0 + BLOCK, :] = vv[:BLOCK]

        def a_q(c, rh):
            cs = slice(c * CHUNK, (c + 1) * CHUNK)
            q_scr[pa, rh * HR:(rh + 1) * HR, cs] = (proj(Q0 + cs.start, rh) * (HEAD_DIM ** -0.5 * LOG2E)).astype(bf16)

        def a_gate(c0, dst, c, rh):
            cs = slice(c * CHUNK, (c + 1) * CHUNK)
            dst[pa, rh * HR:(rh + 1) * HR, cs] = _silu(proj(c0 + cs.start, rh))

        def a_u(c, rh):
            cs = slice(c * CHUNK, (c + 1) * CHUNK)
            u = proj(U0 + cs.start, rh)
            u_scr[pa, POOL_HALO + rh * HR:POOL_HALO + (rh + 1) * HR, cs] = u
            if rh == 0:
                u_scr[pb, POOL_HALO + T:, cs] = jnp.where(b_is_last, 0.0, u[:POOL_HALO])

        units = [(b, j) for b in range(R) for j in range(N_KV_HEADS)]
        assert len(units) == 8, "the program order below is written for eight attention units per tile"

        def b_scores(n):
            b, j = units[n]
            rows = slice(b * BLOCK, (b + 1) * BLOCK)
            win = slice(b * BLOCK, b * BLOCK + 3 * BLOCK)
            c0 = j * Q_PER_KV * HEAD_DIM
            q2 = jnp.concatenate([q_scr[pb, rows, c0:c0 + LANES],
                                  q_scr[pb, rows, c0 + LANES:c0 + 2 * LANES]], axis=0)
            return [lax.dot_general(q2, k_scr[pb, 2 * j + par, win, :], _NT, preferred_element_type=f32)
                    for par in range(2)]

        def b_soft(n, s2s, par, pd, sink_term):
            b, j = units[n]
            if b == 0:
                tbl = jnp.where(b_is_first, 1, 0)
            elif b == R - 1:
                tbl = jnp.where(b_is_last, 2, 0)
            else:
                tbl = 0
            for half in range(2):
                h = j * Q_PER_KV + 2 * half + par
                s = s2s[par][half * BLOCK:(half + 1) * BLOCK, :] + bias_scr[tbl, h]
                sink = sink_ref[0, h] * LOG2E
                m = jnp.maximum(jnp.max(s, axis=-1, keepdims=True), sink)
                pd[(half, par)] = jnp.exp2(s - m).astype(bf16)
                sink_term[(half, par)] = jnp.exp2(sink - m)

        def b_pv(n, pd, sink_term):
            b, j = units[n]
            rows = slice(b * BLOCK, (b + 1) * BLOCK)
            c0 = j * Q_PER_KV * HEAD_DIM
            pcat = jnp.concatenate(
                [jnp.concatenate([pd[(half, par)][:, kb * BLOCK:(kb + 1) * BLOCK]
                                  for kb in range(3) for par in range(2)], axis=1)
                 for half in range(2)], axis=0)
            vwin = jnp.concatenate([v_scr[pb, j, 2 * b * BLOCK:2 * b * BLOCK + 6 * BLOCK, :], ones_scr[...]],
                                   axis=1)
            o2 = jnp.dot(pcat, vwin, preferred_element_type=f32)
            for half in range(2):
                hr = slice(half * BLOCK, (half + 1) * BLOCK)
                denom = o2[hr, LANES:] + jnp.where(lo, sink_term[(half, 0)], sink_term[(half, 1)])
                cs = c0 + half * LANES
                a = (o2[hr, :LANES] / denom) * sga_scr[pb, rows, cs:cs + LANES]
                mix_scr[pb, rows, cs:cs + LANES] = a.astype(bf16)

        def b_pool_y(gi):
            w = POOL_WINDOWS[gi]
            seq = n_tiles * T
            E = POOL_HALO
            left = w // 2
            right = w - 1 - left
            cols = slice(gi * POOL_GROUP, (gi + 1) * POOL_GROUP)
            ug = u_scr[pb, :, cols]
            wsum = _window_sums(ug, w)[E:E + T]
            uc = ug[E:E + T]
            y_mid = wsum * (1.0 / w) - uc
            parts = []
            for r0 in (0, T - E):
                t_glob = tile_b * T + r0 + lax.broadcasted_iota(jnp.int32, (E, LANES), 0)
                cnt = (jnp.minimum(t_glob + right + 1, seq) - jnp.maximum(t_glob - left, 0)).astype(f32)
                parts.append(wsum[r0:r0 + E] / cnt - uc[r0:r0 + E])
            return jnp.concatenate([parts[0], y_mid[E:T - E], parts[1]], axis=0).astype(bf16)

        def b_pool_pair(pr):
            cols = slice(pr * CHUNK, (pr + 1) * CHUNK)
            y2 = jnp.concatenate([b_pool_y(2 * pr), b_pool_y(2 * pr + 1)], axis=1)
            yw = jnp.dot(y2, pw2_scr[pr], preferred_element_type=f32)
            pz = (yw * pscale_ref[:, cols]) * sgp_scr[pb, :, cols]
            mix_scr[pb, :, D_ATTN + pr * CHUNK:D_ATTN + (pr + 1) * CHUNK] = pz.astype(bf16)

        n_cq = D // CHUNK

        def c_out(rh, cq):
            m = jnp.dot(mix_scr[pa, rh * HR:(rh + 1) * HR, :], wout_scr[:, cq * CHUNK:(cq + 1) * CHUNK],
                        preferred_element_type=f32)
            return m, jnp.sum(m * m, axis=-1, keepdims=True)

        def c_post(rh, rb, quarters):
            blk = slice(rb * BLOCK, (rb + 1) * BLOCK)
            rows = slice(rh * HR + rb * BLOCK, rh * HR + (rb + 1) * BLOCK)
            mixed = jnp.concatenate([m[blk] for m, _ in quarters], axis=1)
            ssq = quarters[0][1][blk]
            for _, part_ssq in quarters[1:]:
                ssq = ssq + part_ssq[blk]
            y = (mixed * lax.rsqrt(ssq * (1.0 / D) + EPS)) * gpost_ref[...]
            out_ref[rows, :] = xb_ref[rows, :] + y

        PN = BLOCK // 2

        def prenorm(r8):
            rows = slice(r8 * PN, (r8 + 1) * PN)
            xv = xa_ref[rows, :]
            ms = jnp.mean(xv * xv, axis=-1, keepdims=True)
            xn_scr[rows, :] = (xv * lax.rsqrt(ms + EPS)).astype(bf16)

        mixed_q = {0: [], 1: []}
        part = functools.partial
        out_q = lambda rh, cq: ("c", lambda: mixed_q[rh].append(c_out(rh, cq)))
        ga = lambda c, rh: ("a", part(a_gate, GA0, sga_scr, c, rh))
        gp = lambda c, rh: ("a", part(a_gate, GP0, sgp_scr, c, rh))
        uu = lambda c, rh: ("a", part(a_u, c, rh))
        qq = lambda c, rh: ("a", part(a_q, c, rh))
        kv = lambda rh: ("a", part(a_kv, rh))
        pn = lambda r8: ("a", part(prenorm, r8))
        heavy = [out_q(0, 0), out_q(0, 1), out_q(0, 2),
                 out_q(0, 3), out_q(1, 0), out_q(1, 1),
                 out_q(1, 2), out_q(1, 3), uu(0, 0),
                 uu(1, 0), uu(0, 1), uu(1, 1),
                 qq(0, 0), qq(0, 1), qq(1, 0),
                 qq(1, 1), ga(0, 0), ga(0, 1),
                 ga(1, 0), ga(1, 1), gp(0, 0),
                 gp(0, 1), gp(1, 0), gp(1, 1)]
        assert len(heavy) == 3 * len(units)
        slot_end = {0: [pn(4), pn(5), kv(0)],
                    1: [pn(6), pn(7), kv(1), ("c", lambda: c_post(0, 0, mixed_q[0]))],
                    2: [("c", lambda: c_post(0, 1, mixed_q[0]))],
                    3: [("c", lambda: c_post(1, 0, mixed_q[1])), ("b", part(b_pool_pair, 0))],
                    4: [("c", lambda: c_post(1, 1, mixed_q[1])), ("b", part(b_pool_pair, 1))]}
        assert T // PN == 8

        def run(piece):
            stage, fn = piece
            if enabled[stage]:
                fn()

        if do_a:
            for r8 in range(4):
                prenorm(r8)
        elif do_b:
            u_scr[pb, POOL_HALO + T:, :] = jnp.zeros((POOL_HALO, D_POOL), f32)
        s = {0: b_scores(0)} if do_b else {}
        for n in range(len(units)):
            if do_b and n + 1 < len(units):
                s[n + 1] = b_scores(n + 1)
            pd, sink_term = {}, {}
            run(heavy[3 * n])
            if do_b:
                b_soft(n, s[n], 0, pd, sink_term)
            run(heavy[3 * n + 1])
            if do_b:
                b_soft(n, s.pop(n), 1, pd, sink_term)
            run(heavy[3 * n + 2])
            if do_b:
                b_pv(n, pd, sink_term)
            for piece in slot_end.get(n, ()):
                run(piece)

        if do_a:
            k_scr[pb, :, 0:BLOCK, :] = k_scr[pa, :, T:T + BLOCK, :]
            v_scr[pb, :, 0:2 * BLOCK, :] = v_scr[pa, :, R * 2 * BLOCK:(R + 1) * 2 * BLOCK, :]
            u_scr[pb, 0:POOL_HALO, :] = u_scr[pa, T:T + POOL_HALO, :]

    steady = (i >= 2) & (i < n_tiles)
    for pa in range(2):
        @pl.when(steady & (i % 2 == pa))
        def _(pa=pa):
            run_step(pa, 1 - pa)

    @pl.when(i == 0)
    def _():
        run_step(0, 1, do_b=False, do_c=False)

    @pl.when(i == 1)
    def _():
        run_step(1, 0, do_c=False)

    @pl.when(i == n_tiles)
    def _():
        run_step(n_tiles % 2, 1 - n_tiles % 2, do_a=False)

    @pl.when(i == n_tiles + 1)
    def _():
        run_step((n_tiles + 1) % 2, n_tiles % 2, do_a=False, do_b=False)


def kernel(x, pre_norm_g, w_in, rel_bias, attn_sink, pool_w, pool_scale, w_out, post_norm_g):
    B, S, D = x.shape
    assert B == 1 and D == D_MODEL and S % TILE == 0 and S // TILE >= 3
    assert pre_norm_g.shape == (1, D) and attn_sink.shape == (1, N_Q_HEADS), "single layer"
    assert rel_bias.shape == (N_BUCKETS, N_Q_HEADS) and rel_bias.dtype == attn_sink.dtype == jnp.float32
    T = TILE
    R = T // BLOCK
    n_tiles = S // T
    bf16 = jnp.bfloat16

    x2 = x.reshape(S, D)
    smem = pl.BlockSpec(memory_space=pltpu.SMEM)
    full = lambda shape: pl.BlockSpec(shape, lambda i: (0,) * len(shape))

    out = pl.pallas_call(
        functools.partial(_layer_kernel, n_tiles=n_tiles),
        grid=(n_tiles + 2,),
        in_specs=[
            smem, smem,
            pl.BlockSpec((T, D), lambda i: (jnp.minimum(i, n_tiles - 1), 0)),
            pl.BlockSpec((T, D), lambda i: (jnp.maximum(i - 2, 0), 0)),
            full((1, D)),
            full((D, D_IN)),
            full((len(POOL_WINDOWS), POOL_GROUP, POOL_GROUP)),
            full((1, D_POOL)),
            full((D, D)),
            full((1, D)),
        ],
        out_specs=pl.BlockSpec((T, D), lambda i: (jnp.maximum(i - 2, 0), 0)),
        out_shape=jax.ShapeDtypeStruct((S, D), x.dtype),
        scratch_shapes=[
            pltpu.VMEM((3, N_Q_HEADS, BLOCK, 3 * BLOCK), jnp.float32),
            pltpu.VMEM((2, T, D_ATTN), bf16),
            pltpu.VMEM((2, 4, T + 2 * BLOCK, LANES), bf16),
            pltpu.VMEM((2, N_KV_HEADS, (R + 2) * 2 * BLOCK, LANES), bf16),
            pltpu.VMEM((2, T + 2 * POOL_HALO, D_POOL), jnp.float32),
            pltpu.VMEM((2, T, D_ATTN), jnp.float32),
            pltpu.VMEM((2, T, D_POOL), jnp.float32),
            pltpu.VMEM((2, T, D), bf16),
            pltpu.VMEM((T, D), bf16),
            pltpu.VMEM((len(POOL_WINDOWS) // 2, CHUNK, CHUNK), bf16),
            pltpu.VMEM((3 * 2 * BLOCK, LANES), bf16),
            pltpu.VMEM((D, D), bf16),
            pltpu.VMEM((D, D_IN), bf16),
        ],
        compiler_params=pltpu.CompilerParams(
            dimension_semantics=("arbitrary",),
            vmem_limit_bytes=VMEM_LIMIT_BYTES,
        ),
        name="hymba_layer_fused",
    )(
        rel_bias, attn_sink,
        x2, x2,
        pre_norm_g,
        w_in.reshape(D, D_IN),
        pool_w.reshape(len(POOL_WINDOWS), POOL_GROUP, POOL_GROUP),
        pool_scale,
        w_out.reshape(D, D),
        post_norm_g,
    )
    return out.reshape(B, S, D)
```

```python
import functools
import math

import jax
import jax.numpy as jnp
from jax import lax
from jax.experimental import pallas as pl
from jax.experimental.pallas import tpu as pltpu

D_MODEL = 1024
D_ATTN = 512
D_POOL = 512
HEAD_DIM = 64
N_Q_HEADS = 8
N_KV_HEADS = 2
Q_PER_KV = 4
D_KV = 128
WINDOW = 128
BLOCK = 128
N_BUCKETS = 32
POOL_WINDOWS = (2, 4, 8, 16)
POOL_GROUP = 128
D_IN = 2304
EPS = 1e-6
NEG = -1e30
LOG2E = math.log2(math.e)

LANES = 128
CHUNK = 2 * LANES
POOL_HALO = 16
TILE = 512
VMEM_LIMIT_BYTES = 62 * 1024 * 1024

Q0, K0, V0, GA0, U0, GP0 = 0, 512, 640, 768, 1280, 1792

_NT = (((1,), (1,)), ((), ()))


def _silu(x):
    h = 0.5 * x
    return h + h * jnp.tanh(h)


def _t5_bucket_exact(rel):
    nb = N_BUCKETS // 2
    max_exact = nb // 2
    ret = jnp.where(rel > 0, nb, 0)
    n = jnp.abs(rel)
    n2 = n * n
    large = jnp.full(rel.shape, max_exact, jnp.int32)
    for j in range(1, nb - max_exact):
        large = large + jnp.where(n2 >= (max_exact * max_exact) << j, 1, 0)
    return ret + jnp.where(n < max_exact, n, large)


def _window_sums(ug, w):
    n = ug.shape[0]
    if w == 2:
        return ug + pltpu.roll(ug, 1, axis=0)
    acc = ug + pltpu.roll(ug, n - 1, axis=0)
    span = 2
    while span * 2 < w:
        acc = acc + pltpu.roll(acc, n - span, axis=0)
        span *= 2
    return acc + pltpu.roll(acc, span, axis=0)


def _layer_kernel(relb_ref, sink_ref,
                  xa_ref, xb_ref, gpre_ref, win_ref, poolw_ref, pscale_ref, wout_ref, gpost_ref,
                  out_ref,
                  bias_scr, q_scr, k_scr, v_scr, u_scr, sga_scr, sgp_scr, mix_scr, xn_scr, pw2_scr, ones_scr,
                  wout_scr, win_scr,
                  *, n_tiles):
    i = pl.program_id(0)
    T = TILE
    R = T // BLOCK
    D = D_MODEL
    f32, bf16 = jnp.float32, jnp.bfloat16
    lane = lax.broadcasted_iota(jnp.int32, (1, LANES), 1)
    lo = lane < HEAD_DIM

    @pl.when(i == 0)
    def _():
        TOEP = 4 * BLOCK
        rel_row = lax.broadcasted_iota(jnp.int32, (8, TOEP), 1) - (TOEP // 2 - 1)
        bucket_row = _t5_bucket_exact(rel_row)
        band_row = jnp.abs(rel_row) <= WINDOW
        kj = lax.broadcasted_iota(jnp.int32, (BLOCK, 3 * BLOCK), 1)
        for h in range(N_Q_HEADS):
            def body(b, acc, h=h):
                return jnp.where(bucket_row == b, relb_ref[b, h], acc)
            row = lax.fori_loop(0, N_BUCKETS, body, jnp.zeros((8, TOEP), f32)) * LOG2E
            row = jnp.where(band_row, row, NEG)[0:1, :]
            tbl = pltpu.roll(jnp.broadcast_to(row, (BLOCK, TOEP)), BLOCK + TOEP // 2 + 1, axis=1,
                             stride=1, stride_axis=0)[:, :3 * BLOCK]
            bias_scr[0, h] = tbl
            bias_scr[1, h] = jnp.where(kj >= BLOCK, tbl, NEG)
            bias_scr[2, h] = jnp.where(kj < 2 * BLOCK, tbl, NEG)
        k_scr[0, :, 0:BLOCK, :] = jnp.zeros((4, BLOCK, LANES), bf16)
        v_scr[0, :, 0:2 * BLOCK, :] = jnp.zeros((N_KV_HEADS, 2 * BLOCK, LANES), bf16)
        u_scr[0, 0:POOL_HALO, :] = jnp.zeros((POOL_HALO, D_POOL), f32)
        pw2_scr[...] = jnp.zeros(pw2_scr.shape, bf16)
        for gi in range(len(POOL_WINDOWS)):
            r0 = (gi % 2) * POOL_GROUP
            pw2_scr[gi // 2, r0:r0 + POOL_GROUP, r0:r0 + POOL_GROUP] = poolw_ref[gi].astype(bf16)
        row_lo = lax.broadcasted_iota(jnp.int32, (2 * BLOCK, LANES), 0) < BLOCK
        ones_pat = jnp.where(row_lo, jnp.where(lo, 1.0, 0.0), jnp.where(lo, 0.0, 1.0)).astype(bf16)
        for kb in range(3):
            ones_scr[kb * 2 * BLOCK:(kb + 1) * 2 * BLOCK, :] = ones_pat
        g_rows = jnp.broadcast_to(gpre_ref[...], (LANES, D))
        for kb in range(D // LANES):
            rows = slice(kb * LANES, (kb + 1) * LANES)
            g_col = jnp.transpose(g_rows[:, rows])
            g_col = jnp.concatenate([g_col, g_col], axis=1)
            for c in range(D_IN // CHUNK):
                cs = slice(c * CHUNK, (c + 1) * CHUNK)
                win_scr[rows, cs] = (win_ref[rows, cs] * g_col).astype(bf16)
        for c in range(D // CHUNK):
            wout_scr[:, c * CHUNK:(c + 1) * CHUNK] = wout_ref[:, c * CHUNK:(c + 1) * CHUNK].astype(bf16)

    b_is_first = i == 1
    b_is_last = i == n_tiles
    tile_b = jnp.clip(i - 1, 0, n_tiles - 1)

    def run_step(pa, pb, do_a=True, do_b=True, do_c=True):
        enabled = {"a": do_a, "b": do_b, "c": do_c}
        zero = jnp.zeros((), bf16)

        HR = T // 2
        RB = HR // BLOCK

        def proj(c0, rh):
            return jnp.dot(xn_scr[rh * HR:(rh + 1) * HR, :], win_scr[:, c0:c0 + CHUNK],
                           preferred_element_type=f32)

        def a_kv(rh):
            kv = proj(K0, rh)
            k2 = kv[:, :D_KV].astype(bf16)
            kr = pltpu.roll(kv[:, :D_KV], HEAD_DIM, axis=1).astype(bf16)
            kvars = (jnp.where(lo, k2, zero), jnp.where(lo, zero, kr),
                     jnp.where(lo, kr, zero), jnp.where(lo, zero, k2))
            for var, kk in enumerate(kvars):
                k_scr[pa, var, BLOCK + rh * HR:BLOCK + (rh + 1) * HR, :] = kk
                if rh == 0:
                    k_scr[pb, var, BLOCK + T:, :] = kk[:BLOCK]
            v2 = kv[:, D_KV:].astype(bf16)
            vr = pltpu.roll(kv[:, D_KV:], HEAD_DIM, axis=1).astype(bf16)
            vvars = ((jnp.where(lo, v2, zero), jnp.where(lo, zero, vr)),
                     (jnp.where(lo, vr, zero), jnp.where(lo, zero, v2)))
            for j in range(N_KV_HEADS):
                for par in range(2):
                    vv = vvars[j][par]
                    for kb in range(RB):
                        r0 = (rh * RB + kb + 1) * 2 * BLOCK + par * BLOCK
                        v_scr[pa, j, r0:r0 + BLOCK, :] = vv[kb * BLOCK:(kb + 1) * BLOCK]
                    if rh == 0:
                        r0 = (R + 1) * 2 * BLOCK + par * BLOCK
                        v_scr[pb, j, r0:r0 + BLOCK, :] = vv[:BLOCK]

        def a_q(c, rh):
            cs = slice(c * CHUNK, (c + 1) * CHUNK)
            q_scr[pa, rh * HR:(rh + 1) * HR, cs] = (proj(Q0 + cs.start, rh) * (HEAD_DIM ** -0.5 * LOG2E)).astype(bf16)

        def a_gate(c0, dst, c, rh):
            cs = slice(c * CHUNK, (c + 1) * CHUNK)
            dst[pa, rh * HR:(rh + 1) * HR, cs] = _silu(proj(c0 + cs.start, rh))

        def a_u(c, rh):
            cs = slice(c * CHUNK, (c + 1) * CHUNK)
            u = proj(U0 + cs.start, rh)
            u_scr[pa, POOL_HALO + rh * HR:POOL_HALO + (rh + 1) * HR, cs] = u
            if rh == 0:
                u_scr[pb, POOL_HALO + T:, cs] = jnp.where(b_is_last, 0.0, u[:POOL_HALO])

        units = [(b, j) for b in range(R) for j in range(N_KV_HEADS)]
        assert len(units) == 8, "the program order below is written for eight attention units per tile"

        def b_scores(n):
            b, j = units[n]
            rows = slice(b * BLOCK, (b + 1) * BLOCK)
            win = slice(b * BLOCK, b * BLOCK + 3 * BLOCK)
            c0 = j * Q_PER_KV * HEAD_DIM
            q2 = jnp.concatenate([q_scr[pb, rows, c0:c0 + LANES],
                                  q_scr[pb, rows, c0 + LANES:c0 + 2 * LANES]], axis=0)
            return [lax.dot_general(q2, k_scr[pb, 2 * j + par, win, :], _NT, preferred_element_type=f32)
                    for par in range(2)]

        def b_soft(n, s2s, par, pd, sink_term):
            b, j = units[n]
            if b == 0:
                tbl = jnp.where(b_is_first, 1, 0)
            elif b == R - 1:
                tbl = jnp.where(b_is_last, 2, 0)
            else:
                tbl = 0
            for half in range(2):
                h = j * Q_PER_KV + 2 * half + par
                s = s2s[par][half * BLOCK:(half + 1) * BLOCK, :] + bias_scr[tbl, h]
                sink = sink_ref[0, h] * LOG2E
                m = jnp.maximum(jnp.max(s, axis=-1, keepdims=True), sink)
                pd[(half, par)] = jnp.exp2(s - m).astype(bf16)
                sink_term[(half, par)] = jnp.exp2(sink - m)

        def b_pv(n, pd, sink_term):
            b, j = units[n]
            rows = slice(b * BLOCK, (b + 1) * BLOCK)
            c0 = j * Q_PER_KV * HEAD_DIM
            pcat = jnp.concatenate(
                [jnp.concatenate([pd[(half, par)][:, kb * BLOCK:(kb + 1) * BLOCK]
                                  for kb in range(3) for par in range(2)], axis=1)
                 for half in range(2)], axis=0)
            vwin = jnp.concatenate([v_scr[pb, j, 2 * b * BLOCK:2 * b * BLOCK + 6 * BLOCK, :], ones_scr[...]],
                                   axis=1)
            o2 = jnp.dot(pcat, vwin, preferred_element_type=f32)
            for half in range(2):
                hr = slice(half * BLOCK, (half + 1) * BLOCK)
                denom = o2[hr, LANES:] + jnp.where(lo, sink_term[(half, 0)], sink_term[(half, 1)])
                cs = c0 + half * LANES
                a = (o2[hr, :LANES] / denom) * sga_scr[pb, rows, cs:cs + LANES]
                mix_scr[pb, rows, cs:cs + LANES] = a.astype(bf16)

        def b_pool_y(gi):
            w = POOL_WINDOWS[gi]
            seq = n_tiles * T
            E = POOL_HALO
            left = w // 2
            right = w - 1 - left
            cols = slice(gi * POOL_GROUP, (gi + 1) * POOL_GROUP)
            ug = u_scr[pb, :, cols]
            wsum = _window_sums(ug, w)[E:E + T]
            uc = ug[E:E + T]
            y_mid = wsum * (1.0 / w) - uc
            parts = []
            for r0 in (0, T - E):
                t_glob = tile_b * T + r0 + lax.broadcasted_iota(jnp.int32, (E, LANES), 0)
                cnt = (jnp.minimum(t_glob + right + 1, seq) - jnp.maximum(t_glob - left, 0)).astype(f32)
                parts.append(wsum[r0:r0 + E] / cnt - uc[r0:r0 + E])
            return jnp.concatenate([parts[0], y_mid[E:T - E], parts[1]], axis=0).astype(bf16)

        def b_pool_pair(pr):
            cols = slice(pr * CHUNK, (pr + 1) * CHUNK)
            y2 = jnp.concatenate([b_pool_y(2 * pr), b_pool_y(2 * pr + 1)], axis=1)
            yw = jnp.dot(y2, pw2_scr[pr], preferred_element_type=f32)
            pz = (yw * pscale_ref[:, cols]) * sgp_scr[pb, :, cols]
            mix_scr[pb, :, D_ATTN + pr * CHUNK:D_ATTN + (pr + 1) * CHUNK] = pz.astype(bf16)

        n_cq = D // CHUNK

        def c_out(rh, cq):
            m = jnp.dot(mix_scr[pa, rh * HR:(rh + 1) * HR, :], wout_scr[:, cq * CHUNK:(cq + 1) * CHUNK],
                        preferred_element_type=f32)
            return m, jnp.sum(m * m, axis=-1, keepdims=True)

        def c_post(rh, rb, quarters):
            blk = slice(rb * BLOCK, (rb + 1) * BLOCK)
            rows = slice(rh * HR + rb * BLOCK, rh * HR + (rb + 1) * BLOCK)
            mixed = jnp.concatenate([m[blk] for m, _ in quarters], axis=1)
            ssq = quarters[0][1][blk]
            for _, part_ssq in quarters[1:]:
                ssq = ssq + part_ssq[blk]
            y = (mixed * lax.rsqrt(ssq * (1.0 / D) + EPS)) * gpost_ref[...]
            out_ref[rows, :] = xb_ref[rows, :] + y

        PN = BLOCK // 2

        def prenorm(r8):
            rows = slice(r8 * PN, (r8 + 1) * PN)
            xv = xa_ref[rows, :]
            ms = jnp.mean(xv * xv, axis=-1, keepdims=True)
            xn_scr[rows, :] = (xv * lax.rsqrt(ms + EPS)).astype(bf16)

        mixed_q = {0: [], 1: []}
        part = functools.partial
        out_q = lambda rh, cq: ("c", lambda: mixed_q[rh].append(c_out(rh, cq)))
        ga = lambda c, rh: ("a", part(a_gate, GA0, sga_scr, c, rh))
        gp = lambda c, rh: ("a", part(a_gate, GP0, sgp_scr, c, rh))
        uu = lambda c, rh: ("a", part(a_u, c, rh))
        qq = lambda c, rh: ("a", part(a_q, c, rh))
        kv = lambda rh: ("a", part(a_kv, rh))
        pn = lambda r8: ("a", part(prenorm, r8))
        heavy = [out_q(0, 0), out_q(0, 1), out_q(0, 2),
                 out_q(0, 3), out_q(1, 0), out_q(1, 1),
                 out_q(1, 2), out_q(1, 3), uu(0, 0),
                 uu(1, 0), uu(0, 1), uu(1, 1),
                 qq(0, 0), qq(0, 1), qq(1, 0),
                 qq(1, 1), ga(0, 0), ga(0, 1),
                 ga(1, 0), ga(1, 1), gp(0, 0),
                 gp(0, 1), gp(1, 0), gp(1, 1)]
        assert len(heavy) == 3 * len(units)
        slot_end = {0: [pn(4), pn(5), kv(0)],
                    1: [pn(6), pn(7), kv(1), ("c", lambda: c_post(0, 0, mixed_q[0]))],
                    2: [("c", lambda: c_post(0, 1, mixed_q[0]))],
                    3: [("c", lambda: c_post(1, 0, mixed_q[1])), ("b", part(b_pool_pair, 0))],
                    4: [("c", lambda: c_post(1, 1, mixed_q[1])), ("b", part(b_pool_pair, 1))]}
        assert T // PN == 8

        def run(piece):
            stage, fn = piece
            if enabled[stage]:
                fn()

        if do_a:
            for r8 in range(4):
                prenorm(r8)
        elif do_b:
            u_scr[pb, POOL_HALO + T:, :] = jnp.zeros((POOL_HALO, D_POOL), f32)
        s = {0: b_scores(0)} if do_b else {}
        for n in range(len(units)):
            if do_b and n + 1 < len(units):
                s[n + 1] = b_scores(n + 1)
            pd, sink_term = {}, {}
            run(heavy[3 * n])
            if do_b:
                b_soft(n, s[n], 0, pd, sink_term)
            run(heavy[3 * n + 1])
            if do_b:
                b_soft(n, s.pop(n), 1, pd, sink_term)
            run(heavy[3 * n + 2])
            if do_b:
                b_pv(n, pd, sink_term)
            for piece in slot_end.get(n, ()):
                run(piece)

        if do_a:
            k_scr[pb, :, 0:BLOCK, :] = k_scr[pa, :, T:T + BLOCK, :]
            v_scr[pb, :, 0:2 * BLOCK, :] = v_scr[pa, :, R * 2 * BLOCK:(R + 1) * 2 * BLOCK, :]
            u_scr[pb, 0:POOL_HALO, :] = u_scr[pa, T:T + POOL_HALO, :]

    steady = (i >= 2) & (i < n_tiles)
    for pa in range(2):
        @pl.when(steady & (i % 2 == pa))
        def _(pa=pa):
            run_step(pa, 1 - pa)

    @pl.when(i == 0)
    def _():
        run_step(0, 1, do_b=False, do_c=False)

    @pl.when(i == 1)
    def _():
        run_step(1, 0, do_c=False)

    @pl.when(i == n_tiles)
    def _():
        run_step(n_tiles % 2, 1 - n_tiles % 2, do_a=False)

    @pl.when(i == n_tiles + 1)
    def _():
        run_step((n_tiles + 1) % 2, n_tiles % 2, do_a=False, do_b=False)


def kernel(x, pre_norm_g, w_in, rel_bias, attn_sink, pool_w, pool_scale, w_out, post_norm_g):
    B, S, D = x.shape
    assert B == 1 and D == D_MODEL and S % TILE == 0 and S // TILE >= 3
    assert pre_norm_g.shape == (1, D) and attn_sink.shape == (1, N_Q_HEADS), "single layer"
    assert rel_bias.shape == (N_BUCKETS, N_Q_HEADS) and rel_bias.dtype == attn_sink.dtype == jnp.float32
    T = TILE
    R = T // BLOCK
    n_tiles = S // T
    bf16 = jnp.bfloat16

    x2 = x.reshape(S, D)
    smem = pl.BlockSpec(memory_space=pltpu.SMEM)
    full = lambda shape: pl.BlockSpec(shape, lambda i: (0,) * len(shape))

    out = pl.pallas_call(
        functools.partial(_layer_kernel, n_tiles=n_tiles),
        grid=(n_tiles + 2,),
        in_specs=[
            smem, smem,
            pl.BlockSpec((T, D), lambda i: (jnp.minimum(i, n_tiles - 1), 0)),
            pl.BlockSpec((T, D), lambda i: (jnp.maximum(i - 2, 0), 0)),
            full((1, D)),
            full((D, D_IN)),
            full((len(POOL_WINDOWS), POOL_GROUP, POOL_GROUP)),
            full((1, D_POOL)),
            full((D, D)),
            full((1, D)),
        ],
        out_specs=pl.BlockSpec((T, D), lambda i: (jnp.maximum(i - 2, 0), 0)),
        out_shape=jax.ShapeDtypeStruct((S, D), x.dtype),
        scratch_shapes=[
            pltpu.VMEM((3, N_Q_HEADS, BLOCK, 3 * BLOCK), jnp.float32),
            pltpu.VMEM((2, T, D_ATTN), bf16),
            pltpu.VMEM((2, 4, T + 2 * BLOCK, LANES), bf16),
            pltpu.VMEM((2, N_KV_HEADS, (R + 2) * 2 * BLOCK, LANES), bf16),
            pltpu.VMEM((2, T + 2 * POOL_HALO, D_POOL), jnp.float32),
            pltpu.VMEM((2, T, D_ATTN), jnp.float32),
            pltpu.VMEM((2, T, D_POOL), jnp.float32),
            pltpu.VMEM((2, T, D), bf16),
            pltpu.VMEM((T, D), bf16),
            pltpu.VMEM((len(POOL_WINDOWS) // 2, CHUNK, CHUNK), bf16),
            pltpu.VMEM((3 * 2 * BLOCK, LANES), bf16),
            pltpu.VMEM((D, D), bf16),
            pltpu.VMEM((D, D_IN), bf16),
        ],
        compiler_params=pltpu.CompilerParams(
            dimension_semantics=("arbitrary",),
            vmem_limit_bytes=VMEM_LIMIT_BYTES,
        ),
        name="hymba_layer_fused",
    )(
        rel_bias, attn_sink,
        x2, x2,
        pre_norm_g,
        w_in.reshape(D, D_IN),
        pool_w.reshape(len(POOL_WINDOWS), POOL_GROUP, POOL_GROUP),
        pool_scale,
        w_out.reshape(D, D),
        post_norm_g,
    )
    return out.reshape(B, S, D)
```

```python
import functools
import math

import jax
import jax.numpy as jnp
from jax import lax
from jax.experimental import pallas as pl
from jax.experimental.pallas import tpu as pltpu

D_MODEL = 1024
D_ATTN = 512
D_POOL = 512
HEAD_DIM = 64
N_Q_HEADS = 8
N_KV_HEADS = 2
Q_PER_KV = 4
D_KV = 128
WINDOW = 128
BLOCK = 128
N_BUCKETS = 32
POOL_WINDOWS = (2, 4, 8, 16)
POOL_GROUP = 128
D_IN = 2304
EPS = 1e-6
NEG = -1e30
LOG2E = math.log2(math.e)

LANES = 128
CHUNK = 2 * LANES
POOL_HALO = 16
TILE = 512
VMEM_LIMIT_BYTES = 62 * 1024 * 1024

Q0, K0, V0, GA0, U0, GP0 = 0, 512, 640, 768, 1280, 1792

_NT = (((1,), (1,)), ((), ()))


def _silu(x):
    h = 0.5 * x
    return h + h * jnp.tanh(h)


def _t5_bucket_exact(rel):
    nb = N_BUCKETS // 2
    max_exact = nb // 2
    ret = jnp.where(rel > 0, nb, 0)
    n = jnp.abs(rel)
    n2 = n * n
    large = jnp.full(rel.shape, max_exact, jnp.int32)
    for j in range(1, nb - max_exact):
        large = large + jnp.where(n2 >= (max_exact * max_exact) << j, 1, 0)
    return ret + jnp.where(n < max_exact, n, large)


def _window_sums(ug, w):
    n = ug.shape[0]
    if w == 2:
        return ug + pltpu.roll(ug, 1, axis=0)
    acc = ug + pltpu.roll(ug, n - 1, axis=0)
    span = 2
    while span * 2 < w:
        acc = acc + pltpu.roll(acc, n - span, axis=0)
        span *= 2
    return acc + pltpu.roll(acc, span, axis=0)


def _layer_kernel(relb_ref, sink_ref,
                  xa_ref, xb_ref, gpre_ref, win_ref, poolw_ref, pscale_ref, wout_ref, gpost_ref,
                  out_ref,
                  bias_scr, q_scr, k_scr, v_scr, u_scr, sga_scr, sgp_scr, mix_scr, xn_scr, pw2_scr, ones_scr,
                  wout_scr, win_scr,
                  *, n_tiles):
    i = pl.program_id(0)
    T = TILE
    R = T // BLOCK
    D = D_MODEL
    f32, bf16 = jnp.float32, jnp.bfloat16
    lane = lax.broadcasted_iota(jnp.int32, (1, LANES), 1)
    lo = lane < HEAD_DIM

    @pl.when(i == 0)
    def _():
        TOEP = 4 * BLOCK
        rel_row = lax.broadcasted_iota(jnp.int32, (8, TOEP), 1) - (TOEP // 2 - 1)
        bucket_row = _t5_bucket_exact(rel_row)
        band_row = jnp.abs(rel_row) <= WINDOW
        kj = lax.broadcasted_iota(jnp.int32, (BLOCK, 3 * BLOCK), 1)
        for h in range(N_Q_HEADS):
            def body(b, acc, h=h):
                return jnp.where(bucket_row == b, relb_ref[b, h], acc)
            row = lax.fori_loop(0, N_BUCKETS, body, jnp.zeros((8, TOEP), f32)) * LOG2E
            row = jnp.where(band_row, row, NEG)[0:1, :]
            tbl = pltpu.roll(jnp.broadcast_to(row, (BLOCK, TOEP)), BLOCK + TOEP // 2 + 1, axis=1,
                             stride=1, stride_axis=0)[:, :3 * BLOCK]
            bias_scr[0, h] = tbl
            bias_scr[1, h] = jnp.where(kj >= BLOCK, tbl, NEG)
            bias_scr[2, h] = jnp.where(kj < 2 * BLOCK, tbl, NEG)
        k_scr[0, :, 0:BLOCK, :] = jnp.zeros((4, BLOCK, LANES), bf16)
        v_scr[0, :, 0:2 * BLOCK, :] = jnp.zeros((N_KV_HEADS, 2 * BLOCK, LANES), bf16)
        u_scr[0, 0:POOL_HALO, :] = jnp.zeros((POOL_HALO, D_POOL), f32)
        pw2_scr[...] = jnp.zeros(pw2_scr.shape, bf16)
        for gi in range(len(POOL_WINDOWS)):
            r0 = (gi % 2) * POOL_GROUP
            pw2_scr[gi // 2, r0:r0 + POOL_GROUP, r0:r0 + POOL_GROUP] = poolw_ref[gi].astype(bf16)
        row_lo = lax.broadcasted_iota(jnp.int32, (2 * BLOCK, LANES), 0) < BLOCK
        ones_pat = jnp.where(row_lo, jnp.where(lo, 1.0, 0.0), jnp.where(lo, 0.0, 1.0)).astype(bf16)
        for kb in range(3):
            ones_scr[kb * 2 * BLOCK:(kb + 1) * 2 * BLOCK, :] = ones_pat
        g_rows = jnp.broadcast_to(gpre_ref[...], (LANES, D))
        for kb in range(D // LANES):
            rows = slice(kb * LANES, (kb + 1) * LANES)
            g_col = jnp.transpose(g_rows[:, rows])
            g_col = jnp.concatenate([g_col, g_col], axis=1)
            for c in range(D_IN // CHUNK):
                cs = slice(c * CHUNK, (c + 1) * CHUNK)
                win_scr[rows, cs] = (win_ref[rows, cs] * g_col).astype(bf16)
        for c in range(D // CHUNK):
            wout_scr[:, c * CHUNK:(c + 1) * CHUNK] = wout_ref[:, c * CHUNK:(c + 1) * CHUNK].astype(bf16)

    b_is_first = i == 1
    b_is_last = i == n_tiles
    tile_b = jnp.clip(i - 1, 0, n_tiles - 1)

    def run_step(pa, pb, do_a=True, do_b=True, do_c=True):
        enabled = {"a": do_a, "b": do_b, "c": do_c}
        zero = jnp.zeros((), bf16)

        HR = T
        RB = HR // BLOCK

        PC = 2 * CHUNK

        def proj(c0, rh, width=PC):
            return jnp.dot(xn_scr[rh * HR:(rh + 1) * HR, :], win_scr[:, c0:c0 + width],
                           preferred_element_type=f32)

        def a_kv(rh):
            kv = proj(K0, rh, 2 * D_KV)
            k2 = kv[:, :D_KV].astype(bf16)
            kr = pltpu.roll(kv[:, :D_KV], HEAD_DIM, axis=1).astype(bf16)
            kvars = (jnp.where(lo, k2, zero), jnp.where(lo, zero, kr),
                     jnp.where(lo, kr, zero), jnp.where(lo, zero, k2))
            for var, kk in enumerate(kvars):
                k_scr[pa, var, BLOCK + rh * HR:BLOCK + (rh + 1) * HR, :] = kk
                if rh == 0:
                    k_scr[pb, var, BLOCK + T:, :] = kk[:BLOCK]
            v2 = kv[:, D_KV:].astype(bf16)
            vr = pltpu.roll(kv[:, D_KV:], HEAD_DIM, axis=1).astype(bf16)
            vvars = ((jnp.where(lo, v2, zero), jnp.where(lo, zero, vr)),
                     (jnp.where(lo, vr, zero), jnp.where(lo, zero, v2)))
            for j in range(N_KV_HEADS):
                for par in range(2):
                    vv = vvars[j][par]
                    for kb in range(RB):
                        r0 = (rh * RB + kb + 1) * 2 * BLOCK + par * BLOCK
                        v_scr[pa, j, r0:r0 + BLOCK, :] = vv[kb * BLOCK:(kb + 1) * BLOCK]
                    if rh == 0:
                        r0 = (R + 1) * 2 * BLOCK + par * BLOCK
                        v_scr[pb, j, r0:r0 + BLOCK, :] = vv[:BLOCK]

        def a_q(c, rh):
            cs = slice(c * PC, (c + 1) * PC)
            q_scr[pa, rh * HR:(rh + 1) * HR, cs] = (proj(Q0 + cs.start, rh) * (HEAD_DIM ** -0.5 * LOG2E)).astype(bf16)

        def a_gate(c0, dst, c, rh):
            cs = slice(c * PC, (c + 1) * PC)
            dst[pa, rh * HR:(rh + 1) * HR, cs] = _silu(proj(c0 + cs.start, rh))

        def a_u(c, rh):
            cs = slice(c * PC, (c + 1) * PC)
            u = proj(U0 + cs.start, rh)
            u_scr[pa, POOL_HALO + rh * HR:POOL_HALO + (rh + 1) * HR, cs] = u
            if rh == 0:
                u_scr[pb, POOL_HALO + T:, cs] = jnp.where(b_is_last, 0.0, u[:POOL_HALO])

        units = [(b, j) for b in range(R) for j in range(N_KV_HEADS)]
        assert len(units) == 8, "the program order below is written for eight attention units per tile"

        def b_scores(n):
            b, j = units[n]
            rows = slice(b * BLOCK, (b + 1) * BLOCK)
            win = slice(b * BLOCK, b * BLOCK + 3 * BLOCK)
            c0 = j * Q_PER_KV * HEAD_DIM
            q2 = jnp.concatenate([q_scr[pb, rows, c0:c0 + LANES],
                                  q_scr[pb, rows, c0 + LANES:c0 + 2 * LANES]], axis=0)
            return [lax.dot_general(q2, k_scr[pb, 2 * j + par, win, :], _NT, preferred_element_type=f32)
                    for par in range(2)]

        def b_soft(n, s2s, par, pd, sink_term):
            b, j = units[n]
            if b == 0:
                tbl = jnp.where(b_is_first, 1, 0)
            elif b == R - 1:
                tbl = jnp.where(b_is_last, 2, 0)
            else:
                tbl = 0
            for half in range(2):
                h = j * Q_PER_KV + 2 * half + par
                s = s2s[par][half * BLOCK:(half + 1) * BLOCK, :] + bias_scr[tbl, h]
                sink = sink_ref[0, h] * LOG2E
                m = jnp.maximum(jnp.max(s, axis=-1, keepdims=True), sink)
                pd[(half, par)] = jnp.exp2(s - m).astype(bf16)
                sink_term[(half, par)] = jnp.exp2(sink - m)

        def b_pv(n, pd, sink_term):
            b, j = units[n]
            rows = slice(b * BLOCK, (b + 1) * BLOCK)
            c0 = j * Q_PER_KV * HEAD_DIM
            pcat = jnp.concatenate(
                [jnp.concatenate([pd[(half, par)][:, kb * BLOCK:(kb + 1) * BLOCK]
                                  for kb in range(3) for par in range(2)], axis=1)
                 for half in range(2)], axis=0)
            vwin = jnp.concatenate([v_scr[pb, j, 2 * b * BLOCK:2 * b * BLOCK + 6 * BLOCK, :], ones_scr[...]],
                                   axis=1)
            o2 = jnp.dot(pcat, vwin, preferred_element_type=f32)
            for half in range(2):
                hr = slice(half * BLOCK, (half + 1) * BLOCK)
                denom = o2[hr, LANES:] + jnp.where(lo, sink_term[(half, 0)], sink_term[(half, 1)])
                cs = c0 + half * LANES
                a = (o2[hr, :LANES] / denom) * sga_scr[pb, rows, cs:cs + LANES]
                mix_scr[pb, rows, cs:cs + LANES] = a.astype(bf16)

        def b_pool_y(gi):
            w = POOL_WINDOWS[gi]
            seq = n_tiles * T
            E = POOL_HALO
            left = w // 2
            right = w - 1 - left
            cols = slice(gi * POOL_GROUP, (gi + 1) * POOL_GROUP)
            ug = u_scr[pb, :, cols]
            wsum = _window_sums(ug, w)[E:E + T]
            uc = ug[E:E + T]
            y_mid = wsum * (1.0 / w) - uc
            parts = []
            for r0 in (0, T - E):
                t_glob = tile_b * T + r0 + lax.broadcasted_iota(jnp.int32, (E, LANES), 0)
                cnt = (jnp.minimum(t_glob + right + 1, seq) - jnp.maximum(t_glob - left, 0)).astype(f32)
                parts.append(wsum[r0:r0 + E] / cnt - uc[r0:r0 + E])
            return jnp.concatenate([parts[0], y_mid[E:T - E], parts[1]], axis=0).astype(bf16)

        def b_pool_pair(pr):
            cols = slice(pr * CHUNK, (pr + 1) * CHUNK)
            y2 = jnp.concatenate([b_pool_y(2 * pr), b_pool_y(2 * pr + 1)], axis=1)
            yw = jnp.dot(y2, pw2_scr[pr], preferred_element_type=f32)
            pz = (yw * pscale_ref[:, cols]) * sgp_scr[pb, :, cols]
            mix_scr[pb, :, D_ATTN + pr * CHUNK:D_ATTN + (pr + 1) * CHUNK] = pz.astype(bf16)

        n_cq = D // CHUNK

        def c_out(rh, cq):
            m = jnp.dot(mix_scr[pa, rh * HR:(rh + 1) * HR, :], wout_scr[:, cq * PC:(cq + 1) * PC],
                        preferred_element_type=f32)
            return m, jnp.sum(m * m, axis=-1, keepdims=True)

        def c_post(rh, rb, quarters):
            blk = slice(rb * BLOCK, (rb + 1) * BLOCK)
            rows = slice(rh * HR + rb * BLOCK, rh * HR + (rb + 1) * BLOCK)
            mixed = jnp.concatenate([m[blk] for m, _ in quarters], axis=1)
            ssq = quarters[0][1][blk]
            for _, part_ssq in quarters[1:]:
                ssq = ssq + part_ssq[blk]
            y = (mixed * lax.rsqrt(ssq * (1.0 / D) + EPS)) * gpost_ref[...]
            out_ref[rows, :] = xb_ref[rows, :] + y

        PN = BLOCK // 2

        def prenorm(r8):
            rows = slice(r8 * PN, (r8 + 1) * PN)
            xv = xa_ref[rows, :]
            ms = jnp.mean(xv * xv, axis=-1, keepdims=True)
            xn_scr[rows, :] = (xv * lax.rsqrt(ms + EPS)).astype(bf16)

        mixed_q = {0: [], 1: []}
        part = functools.partial
        out_q = lambda rh, cq: ("c", lambda: mixed_q[rh].append(c_out(rh, cq)))
        ga = lambda c, rh: ("a", part(a_gate, GA0, sga_scr, c, rh))
        gp = lambda c, rh: ("a", part(a_gate, GP0, sgp_scr, c, rh))
        uu = lambda c, rh: ("a", part(a_u, c, rh))
        qq = lambda c, rh: ("a", part(a_q, c, rh))
        kv = lambda rh: ("a", part(a_kv, rh))
        pn = lambda r8: ("a", part(prenorm, r8))
        post = lambda rb: ("c", lambda: c_post(0, rb, mixed_q[0]))
        heavy = [[out_q(0, 0), None, None],
                 [out_q(0, 1), None, None],
                 [kv(0), None, None],
                 [uu(0, 0), None, None],
                 [qq(0, 0), None, None],
                 [ga(0, 0), None, None],
                 [gp(0, 0), None, None],
                 [None, None, None]]
        assert len(heavy) == len(units) and T == HR and PC == D_ATTN == D_POOL and D == 2 * PC
        slot_end = {0: [pn(4), pn(5), pn(6), pn(7)],
                    1: [post(0), post(1)],
                    2: [post(2), post(3)],
                    3: [("b", part(b_pool_pair, 0))],
                    4: [("b", part(b_pool_pair, 1))]}
        assert T // PN == 8

        def run(piece):
            if piece is not None and enabled[piece[0]]:
                piece[1]()

        if do_a:
            for r8 in range(4):
                prenorm(r8)
        elif do_b:
            u_scr[pb, POOL_HALO + T:, :] = jnp.zeros((POOL_HALO, D_POOL), f32)
        s = {0: b_scores(0)} if do_b else {}
        for n in range(len(units)):
            if do_b and n + 1 < len(units):
                s[n + 1] = b_scores(n + 1)
            pd, sink_term = {}, {}
            run(heavy[n][0])
            if do_b:
                b_soft(n, s[n], 0, pd, sink_term)
            run(heavy[n][1])
            if do_b:
                b_soft(n, s.pop(n), 1, pd, sink_term)
            run(heavy[n][2])
            if do_b:
                b_pv(n, pd, sink_term)
            for piece in slot_end.get(n, ()):
                run(piece)

        if do_a:
            k_scr[pb, :, 0:BLOCK, :] = k_scr[pa, :, T:T + BLOCK, :]
            v_scr[pb, :, 0:2 * BLOCK, :] = v_scr[pa, :, R * 2 * BLOCK:(R + 1) * 2 * BLOCK, :]
            u_scr[pb, 0:POOL_HALO, :] = u_scr[pa, T:T + POOL_HALO, :]

    steady = (i >= 2) & (i < n_tiles)
    for pa in range(2):
        @pl.when(steady & (i % 2 == pa))
        def _(pa=pa):
            run_step(pa, 1 - pa)

    @pl.when(i == 0)
    def _():
        run_step(0, 1, do_b=False, do_c=False)

    @pl.when(i == 1)
    def _():
        run_step(1, 0, do_c=False)

    @pl.when(i == n_tiles)
    def _():
        run_step(n_tiles % 2, 1 - n_tiles % 2, do_a=False)

    @pl.when(i == n_tiles + 1)
    def _():
        run_step((n_tiles + 1) % 2, n_tiles % 2, do_a=False, do_b=False)


def kernel(x, pre_norm_g, w_in, rel_bias, attn_sink, pool_w, pool_scale, w_out, post_norm_g):
    B, S, D = x.shape
    assert B == 1 and D == D_MODEL and S % TILE == 0 and S // TILE >= 3
    assert pre_norm_g.shape == (1, D) and attn_sink.shape == (1, N_Q_HEADS), "single layer"
    assert rel_bias.shape == (N_BUCKETS, N_Q_HEADS) and rel_bias.dtype == attn_sink.dtype == jnp.float32
    T = TILE
    R = T // BLOCK
    n_tiles = S // T
    bf16 = jnp.bfloat16

    x2 = x.reshape(S, D)
    smem = pl.BlockSpec(memory_space=pltpu.SMEM)
    full = lambda shape: pl.BlockSpec(shape, lambda i: (0,) * len(shape))

    out = pl.pallas_call(
        functools.partial(_layer_kernel, n_tiles=n_tiles),
        grid=(n_tiles + 2,),
        in_specs=[
            smem, smem,
            pl.BlockSpec((T, D), lambda i: (jnp.minimum(i, n_tiles - 1), 0)),
            pl.BlockSpec((T, D), lambda i: (jnp.maximum(i - 2, 0), 0)),
            full((1, D)),
            full((D, D_IN)),
            full((len(POOL_WINDOWS), POOL_GROUP, POOL_GROUP)),
            full((1, D_POOL)),
            full((D, D)),
            full((1, D)),
        ],
        out_specs=pl.BlockSpec((T, D), lambda i: (jnp.maximum(i - 2, 0), 0)),
        out_shape=jax.ShapeDtypeStruct((S, D), x.dtype),
        scratch_shapes=[
            pltpu.VMEM((3, N_Q_HEADS, BLOCK, 3 * BLOCK), jnp.float32),
            pltpu.VMEM((2, T, D_ATTN), bf16),
            pltpu.VMEM((2, 4, T + 2 * BLOCK, LANES), bf16),
            pltpu.VMEM((2, N_KV_HEADS, (R + 2) * 2 * BLOCK, LANES), bf16),
            pltpu.VMEM((2, T + 2 * POOL_HALO, D_POOL), jnp.float32),
            pltpu.VMEM((2, T, D_ATTN), jnp.float32),
            pltpu.VMEM((2, T, D_POOL), jnp.float32),
            pltpu.VMEM((2, T, D), bf16),
            pltpu.VMEM((T, D), bf16),
            pltpu.VMEM((len(POOL_WINDOWS) // 2, CHUNK, CHUNK), bf16),
            pltpu.VMEM((3 * 2 * BLOCK, LANES), bf16),
            pltpu.VMEM((D, D), bf16),
            pltpu.VMEM((D, D_IN), bf16),
        ],
        compiler_params=pltpu.CompilerParams(
            dimension_semantics=("arbitrary",),
            vmem_limit_bytes=VMEM_LIMIT_BYTES,
        ),
        name="hymba_layer_fused",
    )(
        rel_bias, attn_sink,
        x2, x2,
        pre_norm_g,
        w_in.reshape(D, D_IN),
        pool_w.reshape(len(POOL_WINDOWS), POOL_GROUP, POOL_GROUP),
        pool_scale,
        w_out.reshape(D, D),
        post_norm_g,
    )
    return out.reshape(B, S, D)
```

```python
import functools
import math

import jax
import jax.numpy as jnp
from jax import lax
from jax.experimental import pallas as pl
from jax.experimental.pallas import tpu as pltpu

D_MODEL = 1024
D_ATTN = 512
D_POOL = 512
HEAD_DIM = 64
N_Q_HEADS = 8
N_KV_HEADS = 2
Q_PER_KV = 4
D_KV = 128
WINDOW = 128
BLOCK = 128
N_BUCKETS = 32
POOL_WINDOWS = (2, 4, 8, 16)
POOL_GROUP = 128
D_IN = 2304
EPS = 1e-6
NEG = -1e30
LOG2E = math.log2(math.e)

LANES = 128
CHUNK = 2 * LANES
POOL_HALO = 16
TILE = 512
VMEM_LIMIT_BYTES = 62 * 1024 * 1024

Q0, K0, V0, GA0, U0, GP0 = 0, 512, 640, 768, 1280, 1792

_NT = (((1,), (1,)), ((), ()))


def _silu(x):
    h = 0.5 * x
    return h + h * jnp.tanh(h)


def _t5_bucket_exact(rel):
    nb = N_BUCKETS // 2
    max_exact = nb // 2
    ret = jnp.where(rel > 0, nb, 0)
    n = jnp.abs(rel)
    n2 = n * n
    large = jnp.full(rel.shape, max_exact, jnp.int32)
    for j in range(1, nb - max_exact):
        large = large + jnp.where(n2 >= (max_exact * max_exact) << j, 1, 0)
    return ret + jnp.where(n < max_exact, n, large)


def _window_sums(ug, w):
    n = ug.shape[0]
    if w == 2:
        return ug + pltpu.roll(ug, 1, axis=0)
    acc = ug + pltpu.roll(ug, n - 1, axis=0)
    span = 2
    while span * 2 < w:
        acc = acc + pltpu.roll(acc, n - span, axis=0)
        span *= 2
    return acc + pltpu.roll(acc, span, axis=0)


def _layer_kernel(relb_ref, sink_ref,
                  xa_ref, xb_ref, gpre_ref, win_ref, poolw_ref, pscale_ref, wout_ref, gpost_ref,
                  out_ref,
                  bias_scr, q_scr, k_scr, v_scr, u_scr, sga_scr, sgp_scr, mix_scr, xn_scr, pw2_scr, ones_scr,
                  wout_scr, win_scr,
                  *, n_tiles):
    i = pl.program_id(0)
    T = TILE
    R = T // BLOCK
    D = D_MODEL
    f32, bf16 = jnp.float32, jnp.bfloat16
    lane = lax.broadcasted_iota(jnp.int32, (1, LANES), 1)
    lo = lane < HEAD_DIM

    @pl.when(i == 0)
    def _():
        TOEP = 4 * BLOCK
        rel_row = lax.broadcasted_iota(jnp.int32, (8, TOEP), 1) - (TOEP // 2 - 1)
        bucket_row = _t5_bucket_exact(rel_row)
        band_row = jnp.abs(rel_row) <= WINDOW
        kj = lax.broadcasted_iota(jnp.int32, (BLOCK, 3 * BLOCK), 1)
        for h in range(N_Q_HEADS):
            def body(b, acc, h=h):
                return jnp.where(bucket_row == b, relb_ref[h, b], acc)
            row = lax.fori_loop(0, N_BUCKETS, body, jnp.zeros((8, TOEP), f32)) * LOG2E
            row = jnp.where(band_row, row, NEG)[0:1, :]
            tbl = pltpu.roll(jnp.broadcast_to(row, (BLOCK, TOEP)), BLOCK + TOEP // 2 + 1, axis=1,
                             stride=1, stride_axis=0)[:, :3 * BLOCK]
            bias_scr[0, h] = tbl
            bias_scr[1, h] = jnp.where(kj >= BLOCK, tbl, NEG)
            bias_scr[2, h] = jnp.where(kj < 2 * BLOCK, tbl, NEG)
        k_scr[0, :, 0:BLOCK, :] = jnp.zeros((4, BLOCK, LANES), bf16)
        v_scr[0, :, 0:2 * BLOCK, :] = jnp.zeros((N_KV_HEADS, 2 * BLOCK, LANES), bf16)
        u_scr[0, 0:POOL_HALO, :] = jnp.zeros((POOL_HALO, D_POOL), f32)
        pw2_scr[...] = jnp.zeros(pw2_scr.shape, bf16)
        for gi in range(len(POOL_WINDOWS)):
            r0 = (gi % 2) * POOL_GROUP
            pw2_scr[gi // 2, r0:r0 + POOL_GROUP, r0:r0 + POOL_GROUP] = poolw_ref[gi].astype(bf16)
        row_lo = lax.broadcasted_iota(jnp.int32, (2 * BLOCK, LANES), 0) < BLOCK
        ones_pat = jnp.where(row_lo, jnp.where(lo, 1.0, 0.0), jnp.where(lo, 0.0, 1.0)).astype(bf16)
        for kb in range(3):
            ones_scr[kb * 2 * BLOCK:(kb + 1) * 2 * BLOCK, :] = ones_pat
        g_rows = jnp.broadcast_to(gpre_ref[...], (LANES, D))
        for kb in range(D // LANES):
            rows = slice(kb * LANES, (kb + 1) * LANES)
            g_col = jnp.transpose(g_rows[:, rows])
            g_col = jnp.concatenate([g_col, g_col], axis=1)
            for c in range(D_IN // CHUNK):
                cs = slice(c * CHUNK, (c + 1) * CHUNK)
                win_scr[rows, cs] = (win_ref[rows, cs] * g_col).astype(bf16)
        for c in range(D // CHUNK):
            wout_scr[:, c * CHUNK:(c + 1) * CHUNK] = wout_ref[:, c * CHUNK:(c + 1) * CHUNK].astype(bf16)

    b_is_first = i == 1
    b_is_last = i == n_tiles
    tile_b = jnp.clip(i - 1, 0, n_tiles - 1)

    def run_step(pa, pb, do_a=True, do_b=True, do_c=True):
        enabled = {"a": do_a, "b": do_b, "c": do_c}
        zero = jnp.zeros((), bf16)

        def proj(c0, width=CHUNK):
            return jnp.dot(xn_scr[...], win_scr[:, c0:c0 + width], preferred_element_type=f32)

        def a_kv():
            kv = proj(K0, 2 * D_KV)
            k2 = kv[:, :D_KV].astype(bf16)
            kr = pltpu.roll(kv[:, :D_KV], HEAD_DIM, axis=1).astype(bf16)
            kvars = (jnp.where(lo, k2, zero), jnp.where(lo, zero, kr),
                     jnp.where(lo, kr, zero), jnp.where(lo, zero, k2))
            for var, kk in enumerate(kvars):
                k_scr[pa, var, BLOCK:BLOCK + T, :] = kk
                k_scr[pb, var, BLOCK + T:, :] = kk[:BLOCK]
            v2 = kv[:, D_KV:].astype(bf16)
            vr = pltpu.roll(kv[:, D_KV:], HEAD_DIM, axis=1).astype(bf16)
            vvars = ((jnp.where(lo, v2, zero), jnp.where(lo, zero, vr)),
                     (jnp.where(lo, vr, zero), jnp.where(lo, zero, v2)))
            for j in range(N_KV_HEADS):
                for par in range(2):
                    vv = vvars[j][par]
                    for kb in range(R):
                        r0 = (kb + 1) * 2 * BLOCK + par * BLOCK
                        v_scr[pa, j, r0:r0 + BLOCK, :] = vv[kb * BLOCK:(kb + 1) * BLOCK]
                    r0 = (R + 1) * 2 * BLOCK + par * BLOCK
                    v_scr[pb, j, r0:r0 + BLOCK, :] = vv[:BLOCK]

        def a_q(c):
            cs = slice(c * CHUNK, (c + 1) * CHUNK)
            q_scr[pa, :, cs] = (proj(Q0 + cs.start) * (HEAD_DIM ** -0.5 * LOG2E)).astype(bf16)

        def a_gate(c0, dst, c):
            cs = slice(c * CHUNK, (c + 1) * CHUNK)
            dst[pa, :, cs] = _silu(proj(c0 + cs.start))

        def a_u(c):
            cs = slice(c * CHUNK, (c + 1) * CHUNK)
            u = proj(U0 + cs.start)
            u_scr[pa, POOL_HALO:POOL_HALO + T, cs] = u
            u_scr[pb, POOL_HALO + T:, cs] = jnp.where(b_is_last, 0.0, u[:POOL_HALO])

        units = [(b, j) for b in range(R) for j in range(N_KV_HEADS)]
        assert len(units) == 8, "the program order below is written for eight attention units per tile"

        def b_scores(n):
            b, j = units[n]
            rows = slice(b * BLOCK, (b + 1) * BLOCK)
            win = slice(b * BLOCK, b * BLOCK + 3 * BLOCK)
            c0 = j * Q_PER_KV * HEAD_DIM
            q2 = jnp.concatenate([q_scr[pb, rows, c0:c0 + LANES],
                                  q_scr[pb, rows, c0 + LANES:c0 + 2 * LANES]], axis=0)
            return [lax.dot_general(q2, k_scr[pb, 2 * j + par, win, :], _NT, preferred_element_type=f32)
                    for par in range(2)]

        def b_soft(n, s2s, par, pd, sink_term):
            b, j = units[n]
            if b == 0:
                tbl = jnp.where(b_is_first, 1, 0)
            elif b == R - 1:
                tbl = jnp.where(b_is_last, 2, 0)
            else:
                tbl = 0
            for half in range(2):
                h = j * Q_PER_KV + 2 * half + par
                s = s2s[par][half * BLOCK:(half + 1) * BLOCK, :] + bias_scr[tbl, h]
                sink = sink_ref[0, h] * LOG2E
                m = jnp.maximum(jnp.max(s, axis=-1, keepdims=True), sink)
                pd[(half, par)] = jnp.exp2(s - m).astype(bf16)
                sink_term[(half, par)] = jnp.exp2(sink - m)

        def b_pv(n, pd, sink_term):
            b, j = units[n]
            rows = slice(b * BLOCK, (b + 1) * BLOCK)
            c0 = j * Q_PER_KV * HEAD_DIM
            pcat = jnp.concatenate(
                [jnp.concatenate([pd[(half, par)][:, kb * BLOCK:(kb + 1) * BLOCK]
                                  for kb in range(3) for par in range(2)], axis=1)
                 for half in range(2)], axis=0)
            vwin = jnp.concatenate([v_scr[pb, j, 2 * b * BLOCK:2 * b * BLOCK + 6 * BLOCK, :], ones_scr[...]],
                                   axis=1)
            o2 = jnp.dot(pcat, vwin, preferred_element_type=f32)
            for half in range(2):
                hr = slice(half * BLOCK, (half + 1) * BLOCK)
                denom = o2[hr, LANES:] + jnp.where(lo, sink_term[(half, 0)], sink_term[(half, 1)])
                cs = c0 + half * LANES
                a = (o2[hr, :LANES] / denom) * sga_scr[pb, rows, cs:cs + LANES]
                mix_scr[pb, rows, cs:cs + LANES] = a.astype(bf16)

        def b_pool_y(gi):
            w = POOL_WINDOWS[gi]
            seq = n_tiles * T
            E = POOL_HALO
            left = w // 2
            right = w - 1 - left
            cols = slice(gi * POOL_GROUP, (gi + 1) * POOL_GROUP)
            ug = u_scr[pb, :, cols]
            wsum = _window_sums(ug, w)[E:E + T]
            uc = ug[E:E + T]
            y_mid = wsum * (1.0 / w) - uc
            parts = []
            for r0 in (0, T - E):
                t_glob = tile_b * T + r0 + lax.broadcasted_iota(jnp.int32, (E, LANES), 0)
                cnt = (jnp.minimum(t_glob + right + 1, seq) - jnp.maximum(t_glob - left, 0)).astype(f32)
                parts.append(wsum[r0:r0 + E] / cnt - uc[r0:r0 + E])
            return jnp.concatenate([parts[0], y_mid[E:T - E], parts[1]], axis=0).astype(bf16)

        def b_pool_pair(pr):
            cols = slice(pr * CHUNK, (pr + 1) * CHUNK)
            y2 = jnp.concatenate([b_pool_y(2 * pr), b_pool_y(2 * pr + 1)], axis=1)
            yw = jnp.dot(y2, pw2_scr[pr], preferred_element_type=f32)
            pz = (yw * pscale_ref[:, cols]) * sgp_scr[pb, :, cols]
            mix_scr[pb, :, D_ATTN + pr * CHUNK:D_ATTN + (pr + 1) * CHUNK] = pz.astype(bf16)

        mixed_q = []

        def c_out(cq):
            m = jnp.dot(mix_scr[pa], wout_scr[:, cq * CHUNK:(cq + 1) * CHUNK], preferred_element_type=f32)
            mixed_q.append((m, jnp.sum(m * m, axis=-1, keepdims=True)))

        def c_post(rb):
            rows = slice(rb * BLOCK, (rb + 1) * BLOCK)
            mixed = jnp.concatenate([m[rows] for m, _ in mixed_q], axis=1)
            ssq = mixed_q[0][1][rows]
            for _, part_ssq in mixed_q[1:]:
                ssq = ssq + part_ssq[rows]
            y = (mixed * lax.rsqrt(ssq * (1.0 / D) + EPS)) * gpost_ref[...]
            out_ref[rows, :] = xb_ref[rows, :] + y

        PN = BLOCK // 2

        def prenorm(r8):
            rows = slice(r8 * PN, (r8 + 1) * PN)
            xv = xa_ref[rows, :]
            ms = jnp.mean(xv * xv, axis=-1, keepdims=True)
            xn_scr[rows, :] = (xv * lax.rsqrt(ms + EPS)).astype(bf16)

        part = functools.partial
        out_q = lambda cq: ("c", part(c_out, cq))
        post = lambda rb: ("c", part(c_post, rb))
        ga = lambda c: ("a", part(a_gate, GA0, sga_scr, c))
        gp = lambda c: ("a", part(a_gate, GP0, sgp_scr, c))
        uu = lambda c: ("a", part(a_u, c))
        qq = lambda c: ("a", part(a_q, c))
        kv = ("a", a_kv)
        pn = lambda r8: ("a", part(prenorm, r8))
        heavy = [[out_q(0), out_q(1), None],
                 [out_q(2), out_q(3), None],
                 [kv, uu(0), None],
                 [uu(1), qq(0), None],
                 [qq(1), ga(0), None],
                 [ga(1), None, None],
                 [gp(0), None, None],
                 [gp(1), None, None]]
        assert len(heavy) == len(units) and D // CHUNK == 4
        slot_end = {0: [pn(4), pn(5), pn(6), pn(7)],
                    1: [post(0), post(1)],
                    2: [post(2), post(3)],
                    3: [("b", part(b_pool_pair, 0))],
                    4: [("b", part(b_pool_pair, 1))]}

        def run(piece):
            if piece is not None and enabled[piece[0]]:
                piece[1]()

        if do_a:
            for r8 in range(T // PN // 2):
                prenorm(r8)
        elif do_b:
            u_scr[pb, POOL_HALO + T:, :] = jnp.zeros((POOL_HALO, D_POOL), f32)
        s = {0: b_scores(0)} if do_b else {}
        for n in range(len(units)):
            if do_b and n + 1 < len(units):
                s[n + 1] = b_scores(n + 1)
            pd, sink_term = {}, {}
            run(heavy[n][0])
            if do_b:
                b_soft(n, s[n], 0, pd, sink_term)
            run(heavy[n][1])
            if do_b:
                b_soft(n, s.pop(n), 1, pd, sink_term)
            run(heavy[n][2])
            if do_b:
                b_pv(n, pd, sink_term)
            for piece in slot_end.get(n, ()):
                run(piece)

        if do_a:
            k_scr[pb, :, 0:BLOCK, :] = k_scr[pa, :, T:T + BLOCK, :]
            v_scr[pb, :, 0:2 * BLOCK, :] = v_scr[pa, :, R * 2 * BLOCK:(R + 1) * 2 * BLOCK, :]
            u_scr[pb, 0:POOL_HALO, :] = u_scr[pa, T:T + POOL_HALO, :]

    steady = (i >= 2) & (i < n_tiles)
    for pa in range(2):
        @pl.when(steady & (i % 2 == pa))
        def _(pa=pa):
            run_step(pa, 1 - pa)

    def ramp(step, **stages):
        @pl.when(i == step)
        def _():
            run_step(step % 2, 1 - step % 2, **stages)

    ramp(0, do_b=False, do_c=False)
    ramp(1, do_c=False)
    ramp(n_tiles, do_a=False)
    ramp(n_tiles + 1, do_a=False, do_b=False)


def kernel(x, pre_norm_g, w_in, rel_bias, attn_sink, pool_w, pool_scale, w_out, post_norm_g):
    B, S, D = x.shape
    assert B == 1 and D == D_MODEL and S % TILE == 0 and S // TILE >= 3
    assert pre_norm_g.shape == (1, D) and attn_sink.shape == (1, N_Q_HEADS), "single layer"
    assert rel_bias.shape == (N_BUCKETS, N_Q_HEADS) and rel_bias.dtype == attn_sink.dtype == jnp.float32
    T = TILE
    R = T // BLOCK
    n_tiles = S // T
    bf16 = jnp.bfloat16

    x2 = x.reshape(S, D)
    smem = pl.BlockSpec(memory_space=pltpu.SMEM)
    full = lambda shape: pl.BlockSpec(shape, lambda i: (0,) * len(shape))

    out = pl.pallas_call(
        functools.partial(_layer_kernel, n_tiles=n_tiles),
        grid=(n_tiles + 2,),
        in_specs=[
            smem, smem,
            pl.BlockSpec((T, D), lambda i: (jnp.minimum(i, n_tiles - 1), 0)),
            pl.BlockSpec((T, D), lambda i: (jnp.maximum(i - 2, 0), 0)),
            full((1, D)),
            full((D, D_IN)),
            full((len(POOL_WINDOWS), POOL_GROUP, POOL_GROUP)),
            full((1, D_POOL)),
            full((D, D)),
            full((1, D)),
        ],
        out_specs=pl.BlockSpec((T, D), lambda i: (jnp.maximum(i - 2, 0), 0)),
        out_shape=jax.ShapeDtypeStruct((S, D), x.dtype),
        scratch_shapes=[
            pltpu.VMEM((3, N_Q_HEADS, BLOCK, 3 * BLOCK), jnp.float32),
            pltpu.VMEM((2, T, D_ATTN), bf16),
            pltpu.VMEM((2, 4, T + 2 * BLOCK, LANES), bf16),
            pltpu.VMEM((2, N_KV_HEADS, (R + 2) * 2 * BLOCK, LANES), bf16),
            pltpu.VMEM((2, T + 2 * POOL_HALO, D_POOL), jnp.float32),
            pltpu.VMEM((2, T, D_ATTN), jnp.float32),
            pltpu.VMEM((2, T, D_POOL), jnp.float32),
            pltpu.VMEM((2, T, D), bf16),
            pltpu.VMEM((T, D), bf16),
            pltpu.VMEM((len(POOL_WINDOWS) // 2, CHUNK, CHUNK), bf16),
            pltpu.VMEM((3 * 2 * BLOCK, LANES), bf16),
            pltpu.VMEM((D, D), bf16),
            pltpu.VMEM((D, D_IN), bf16),
        ],
        compiler_params=pltpu.CompilerParams(
            dimension_semantics=("arbitrary",),
            vmem_limit_bytes=VMEM_LIMIT_BYTES,
        ),
        name="hymba_layer_fused",
    )(
        rel_bias.T, attn_sink,
        x2, x2,
        pre_norm_g,
        w_in.reshape(D, D_IN),
        pool_w.reshape(len(POOL_WINDOWS), POOL_GROUP, POOL_GROUP),
        pool_scale,
        w_out.reshape(D, D),
        post_norm_g,
    )
    return out.reshape(B, S, D)
```

```python
import functools
import math

import jax
import jax.numpy as jnp
from jax import lax
from jax.experimental import pallas as pl
from jax.experimental.pallas import tpu as pltpu

D_MODEL = 1024
D_ATTN = 512
D_POOL = 512
HEAD_DIM = 64
N_Q_HEADS = 8
N_KV_HEADS = 2
Q_PER_KV = 4
D_KV = 128
WINDOW = 128
BLOCK = 128
N_BUCKETS = 32
POOL_WINDOWS = (2, 4, 8, 16)
POOL_GROUP = 128
D_IN = 2304
EPS = 1e-6
NEG = -1e30
LOG2E = math.log2(math.e)

LANES = 128
CHUNK = 2 * LANES
POOL_HALO = 16
TILE = 512
VMEM_LIMIT_BYTES = 62 * 1024 * 1024

Q0, K0, V0, GA0, U0, GP0 = 0, 512, 640, 768, 1280, 1792

_NT = (((1,), (1,)), ((), ()))


def _silu(x):
    h = 0.5 * x
    return h + h * jnp.tanh(h)


def _t5_bucket_exact(rel):
    nb = N_BUCKETS // 2
    max_exact = nb // 2
    ret = jnp.where(rel > 0, nb, 0)
    n = jnp.abs(rel)
    n2 = n * n
    large = jnp.full(rel.shape, max_exact, jnp.int32)
    for j in range(1, nb - max_exact):
        large = large + jnp.where(n2 >= (max_exact * max_exact) << j, 1, 0)
    return ret + jnp.where(n < max_exact, n, large)


def _window_sums(ug, w):
    n = ug.shape[0]
    if w == 2:
        return ug + pltpu.roll(ug, 1, axis=0)
    acc = ug + pltpu.roll(ug, n - 1, axis=0)
    span = 2
    while span * 2 < w:
        acc = acc + pltpu.roll(acc, n - span, axis=0)
        span *= 2
    return acc + pltpu.roll(acc, span, axis=0)


def _layer_kernel(relb_ref, sink_ref,
                  xa_ref, xb_ref, gpre_ref, win_ref, poolw_ref, pscale_ref, wout_ref, gpost_ref,
                  out_ref,
                  bias_scr, q_scr, k_scr, v_scr, u_scr, sga_scr, sgp_scr, mix_scr, xn_scr, pw2_scr, ones_scr,
                  wout_scr, win_scr,
                  *, n_tiles):
    i = pl.program_id(0)
    T = TILE
    R = T // BLOCK
    D = D_MODEL
    f32, bf16 = jnp.float32, jnp.bfloat16
    lane = lax.broadcasted_iota(jnp.int32, (1, LANES), 1)
    lo = lane < HEAD_DIM

    @pl.when(i == 0)
    def _():
        TOEP = 4 * BLOCK
        rel_row = lax.broadcasted_iota(jnp.int32, (8, TOEP), 1) - (TOEP // 2 - 1)
        bucket_row = _t5_bucket_exact(rel_row)
        band_row = jnp.abs(rel_row) <= WINDOW
        kj = lax.broadcasted_iota(jnp.int32, (BLOCK, 3 * BLOCK), 1)
        for h in range(N_Q_HEADS):
            def body(b, acc, h=h):
                return jnp.where(bucket_row == b, relb_ref[h, b], acc)
            row = lax.fori_loop(0, N_BUCKETS, body, jnp.zeros((8, TOEP), f32)) * LOG2E
            row = jnp.where(band_row, row, NEG)[0:1, :]
            tbl = pltpu.roll(jnp.broadcast_to(row, (BLOCK, TOEP)), BLOCK + TOEP // 2 + 1, axis=1,
                             stride=1, stride_axis=0)[:, :3 * BLOCK]
            bias_scr[0, h] = tbl
            bias_scr[1, h] = jnp.where(kj >= BLOCK, tbl, NEG)
            bias_scr[2, h] = jnp.where(kj < 2 * BLOCK, tbl, NEG)
        k_scr[0, :, 0:BLOCK, :] = jnp.zeros((4, BLOCK, LANES), bf16)
        v_scr[0, :, 0:2 * BLOCK, :] = jnp.zeros((N_KV_HEADS, 2 * BLOCK, LANES), bf16)
        u_scr[0, 0:POOL_HALO, :] = jnp.zeros((POOL_HALO, D_POOL), f32)
        pw2_scr[...] = jnp.zeros(pw2_scr.shape, bf16)
        for gi in range(len(POOL_WINDOWS)):
            r0 = (gi % 2) * POOL_GROUP
            pw2_scr[gi // 2, r0:r0 + POOL_GROUP, r0:r0 + POOL_GROUP] = poolw_ref[gi].astype(bf16)
        row_lo = lax.broadcasted_iota(jnp.int32, (2 * BLOCK, LANES), 0) < BLOCK
        ones_pat = jnp.where(row_lo, jnp.where(lo, 1.0, 0.0), jnp.where(lo, 0.0, 1.0)).astype(bf16)
        for kb in range(3):
            ones_scr[kb * 2 * BLOCK:(kb + 1) * 2 * BLOCK, :] = ones_pat
        g_rows = jnp.broadcast_to(gpre_ref[...], (LANES, D))
        for kb in range(D // LANES):
            rows = slice(kb * LANES, (kb + 1) * LANES)
            g_col = jnp.transpose(g_rows[:, rows])
            g_col = jnp.concatenate([g_col, g_col], axis=1)
            for c in range(D_IN // CHUNK):
                cs = slice(c * CHUNK, (c + 1) * CHUNK)
                win_scr[rows, cs] = (win_ref[rows, cs] * g_col).astype(bf16)
        for c in range(D // CHUNK):
            wout_scr[:, c * CHUNK:(c + 1) * CHUNK] = wout_ref[:, c * CHUNK:(c + 1) * CHUNK].astype(bf16)

    b_is_first = i == 1
    b_is_last = i == n_tiles
    tile_b = jnp.clip(i - 1, 0, n_tiles - 1)

    def run_step(pa, pb, do_a=True, do_b=True, do_c=True):
        enabled = {"a": do_a, "b": do_b, "c": do_c}
        zero = jnp.zeros((), bf16)

        def proj(c0, width=CHUNK):
            return jnp.dot(xn_scr[...], win_scr[:, c0:c0 + width], preferred_element_type=f32)

        def a_kv():
            kv = proj(K0, 2 * D_KV)
            k2 = kv[:, :D_KV].astype(bf16)
            kr = pltpu.roll(kv[:, :D_KV], HEAD_DIM, axis=1).astype(bf16)
            kvars = (jnp.where(lo, k2, zero), jnp.where(lo, zero, kr),
                     jnp.where(lo, kr, zero), jnp.where(lo, zero, k2))
            for var, kk in enumerate(kvars):
                k_scr[pa, var, BLOCK:BLOCK + T, :] = kk
                k_scr[pb, var, BLOCK + T:, :] = kk[:BLOCK]
            v2 = kv[:, D_KV:].astype(bf16)
            vr = pltpu.roll(kv[:, D_KV:], HEAD_DIM, axis=1).astype(bf16)
            vvars = ((jnp.where(lo, v2, zero), jnp.where(lo, zero, vr)),
                     (jnp.where(lo, vr, zero), jnp.where(lo, zero, v2)))
            for j in range(N_KV_HEADS):
                for par in range(2):
                    vv = vvars[j][par]
                    for kb in range(R):
                        r0 = (kb + 1) * 2 * BLOCK + par * BLOCK
                        v_scr[pa, j, r0:r0 + BLOCK, :] = vv[kb * BLOCK:(kb + 1) * BLOCK]
                    r0 = (R + 1) * 2 * BLOCK + par * BLOCK
                    v_scr[pb, j, r0:r0 + BLOCK, :] = vv[:BLOCK]

        def a_q(c):
            cs = slice(c * CHUNK, (c + 1) * CHUNK)
            q_scr[pa, :, cs] = (proj(Q0 + cs.start) * (HEAD_DIM ** -0.5 * LOG2E)).astype(bf16)

        def a_gate(c0, dst, c):
            cs = slice(c * CHUNK, (c + 1) * CHUNK)
            dst[pa, :, cs] = _silu(proj(c0 + cs.start))

        def a_u(c):
            cs = slice(c * CHUNK, (c + 1) * CHUNK)
            u = proj(U0 + cs.start)
            u_scr[pa, POOL_HALO:POOL_HALO + T, cs] = u
            u_scr[pb, POOL_HALO + T:, cs] = jnp.where(b_is_last, 0.0, u[:POOL_HALO])

        units = [(b, j) for b in range(R) for j in range(N_KV_HEADS)]
        assert len(units) == 8, "the program order below is written for eight attention units per tile"

        def b_scores(n):
            b, j = units[n]
            rows = slice(b * BLOCK, (b + 1) * BLOCK)
            win = slice(b * BLOCK, b * BLOCK + 3 * BLOCK)
            c0 = j * Q_PER_KV * HEAD_DIM
            q2 = jnp.concatenate([q_scr[pb, rows, c0:c0 + LANES],
                                  q_scr[pb, rows, c0 + LANES:c0 + 2 * LANES]], axis=0)
            return [lax.dot_general(q2, k_scr[pb, 2 * j + par, win, :], _NT, preferred_element_type=f32)
                    for par in range(2)]

        def b_soft(n, s2s, par, pd, sink_term):
            b, j = units[n]
            if b == 0:
                tbl = jnp.where(b_is_first, 1, 0)
            elif b == R - 1:
                tbl = jnp.where(b_is_last, 2, 0)
            else:
                tbl = 0
            for half in range(2):
                h = j * Q_PER_KV + 2 * half + par
                s = s2s[par][half * BLOCK:(half + 1) * BLOCK, :] + bias_scr[tbl, h]
                sink = sink_ref[0, h] * LOG2E
                m = jnp.maximum(jnp.max(s, axis=-1, keepdims=True), sink)
                pd[(half, par)] = jnp.exp2(s - m).astype(bf16)
                sink_term[(half, par)] = jnp.exp2(sink - m)

        def b_pv(n, pd, sink_term):
            b, j = units[n]
            rows = slice(b * BLOCK, (b + 1) * BLOCK)
            c0 = j * Q_PER_KV * HEAD_DIM
            pcat = jnp.concatenate(
                [jnp.concatenate([pd[(half, par)][:, kb * BLOCK:(kb + 1) * BLOCK]
                                  for kb in range(3) for par in range(2)], axis=1)
                 for half in range(2)], axis=0)
            vwin = jnp.concatenate([v_scr[pb, j, 2 * b * BLOCK:2 * b * BLOCK + 6 * BLOCK, :], ones_scr[...]],
                                   axis=1)
            o2 = jnp.dot(pcat, vwin, preferred_element_type=f32)
            for half in range(2):
                hr = slice(half * BLOCK, (half + 1) * BLOCK)
                denom = o2[hr, LANES:] + jnp.where(lo, sink_term[(half, 0)], sink_term[(half, 1)])
                cs = c0 + half * LANES
                a = (o2[hr, :LANES] / denom) * sga_scr[pb, rows, cs:cs + LANES]
                mix_scr[pb, rows, cs:cs + LANES] = a.astype(bf16)

        def b_pool_y(gi):
            w = POOL_WINDOWS[gi]
            seq = n_tiles * T
            E = POOL_HALO
            left = w // 2
            right = w - 1 - left
            cols = slice(gi * POOL_GROUP, (gi + 1) * POOL_GROUP)
            ug = u_scr[pb, :, cols]
            wsum = _window_sums(ug, w)[E:E + T]
            uc = ug[E:E + T]
            y_mid = wsum * (1.0 / w) - uc
            parts = []
            for r0 in (0, T - E):
                t_glob = tile_b * T + r0 + lax.broadcasted_iota(jnp.int32, (E, LANES), 0)
                cnt = (jnp.minimum(t_glob + right + 1, seq) - jnp.maximum(t_glob - left, 0)).astype(f32)
                parts.append(wsum[r0:r0 + E] / cnt - uc[r0:r0 + E])
            return jnp.concatenate([parts[0], y_mid[E:T - E], parts[1]], axis=0).astype(bf16)

        def b_pool_pair(pr):
            cols = slice(pr * CHUNK, (pr + 1) * CHUNK)
            y2 = jnp.concatenate([b_pool_y(2 * pr), b_pool_y(2 * pr + 1)], axis=1)
            yw = jnp.dot(y2, pw2_scr[pr], preferred_element_type=f32)
            pz = (yw * pscale_ref[:, cols]) * sgp_scr[pb, :, cols]
            mix_scr[pb, :, D_ATTN + pr * CHUNK:D_ATTN + (pr + 1) * CHUNK] = pz.astype(bf16)

        mixed_q = []

        def c_out(cq):
            m = jnp.dot(mix_scr[pa], wout_scr[:, cq * 2 * CHUNK:(cq + 1) * 2 * CHUNK], preferred_element_type=f32)
            mixed_q.append((m, jnp.sum(m * m, axis=-1, keepdims=True)))

        def c_post(rb):
            rows = slice(rb * BLOCK, (rb + 1) * BLOCK)
            mixed = jnp.concatenate([m[rows] for m, _ in mixed_q], axis=1)
            ssq = mixed_q[0][1][rows]
            for _, part_ssq in mixed_q[1:]:
                ssq = ssq + part_ssq[rows]
            y = (mixed * lax.rsqrt(ssq * (1.0 / D) + EPS)) * gpost_ref[...]
            out_ref[rows, :] = xb_ref[rows, :] + y

        PN = BLOCK // 2

        def prenorm(r8):
            rows = slice(r8 * PN, (r8 + 1) * PN)
            xv = xa_ref[rows, :]
            ms = jnp.mean(xv * xv, axis=-1, keepdims=True)
            xn_scr[rows, :] = (xv * lax.rsqrt(ms + EPS)).astype(bf16)

        part = functools.partial
        out_q = lambda cq: ("c", part(c_out, cq))
        post = lambda rb: ("c", part(c_post, rb))
        ga = lambda c: ("a", part(a_gate, GA0, sga_scr, c))
        gp = lambda c: ("a", part(a_gate, GP0, sgp_scr, c))
        uu = lambda c: ("a", part(a_u, c))
        qq = lambda c: ("a", part(a_q, c))
        kv = ("a", a_kv)
        pn = lambda r8: ("a", part(prenorm, r8))
        heavy = [[out_q(0), None, None],
                 [out_q(1), None, None],
                 [kv, uu(0), None],
                 [uu(1), qq(0), None],
                 [qq(1), ga(0), None],
                 [ga(1), None, None],
                 [gp(0), None, None],
                 [gp(1), None, None]]
        assert len(heavy) == len(units) and D // CHUNK == 4
        slot_end = {0: [pn(4), pn(5), pn(6), pn(7)],
                    1: [post(0), post(1)],
                    2: [post(2), post(3)],
                    3: [("b", part(b_pool_pair, 0))],
                    4: [("b", part(b_pool_pair, 1))]}

        def run(piece):
            if piece is not None and enabled[piece[0]]:
                piece[1]()

        if do_a:
            for r8 in range(T // PN // 2):
                prenorm(r8)
        elif do_b:
            u_scr[pb, POOL_HALO + T:, :] = jnp.zeros((POOL_HALO, D_POOL), f32)
        s = {0: b_scores(0)} if do_b else {}
        for n in range(len(units)):
            if do_b and n + 1 < len(units):
                s[n + 1] = b_scores(n + 1)
            pd, sink_term = {}, {}
            run(heavy[n][0])
            if do_b:
                b_soft(n, s[n], 0, pd, sink_term)
            run(heavy[n][1])
            if do_b:
                b_soft(n, s.pop(n), 1, pd, sink_term)
            run(heavy[n][2])
            if do_b:
                b_pv(n, pd, sink_term)
            for piece in slot_end.get(n, ()):
                run(piece)

        if do_a:
            k_scr[pb, :, 0:BLOCK, :] = k_scr[pa, :, T:T + BLOCK, :]
            v_scr[pb, :, 0:2 * BLOCK, :] = v_scr[pa, :, R * 2 * BLOCK:(R + 1) * 2 * BLOCK, :]
            u_scr[pb, 0:POOL_HALO, :] = u_scr[pa, T:T + POOL_HALO, :]

    steady = (i >= 2) & (i < n_tiles)
    for pa in range(2):
        @pl.when(steady & (i % 2 == pa))
        def _(pa=pa):
            run_step(pa, 1 - pa)

    def ramp(step, **stages):
        @pl.when(i == step)
        def _():
            run_step(step % 2, 1 - step % 2, **stages)

    ramp(0, do_b=False, do_c=False)
    ramp(1, do_c=False)
    ramp(n_tiles, do_a=False)
    ramp(n_tiles + 1, do_a=False, do_b=False)


def kernel(x, pre_norm_g, w_in, rel_bias, attn_sink, pool_w, pool_scale, w_out, post_norm_g):
    B, S, D = x.shape
    assert B == 1 and D == D_MODEL and S % TILE == 0 and S // TILE >= 3
    assert pre_norm_g.shape == (1, D) and attn_sink.shape == (1, N_Q_HEADS), "single layer"
    assert rel_bias.shape == (N_BUCKETS, N_Q_HEADS) and rel_bias.dtype == attn_sink.dtype == jnp.float32
    T = TILE
    R = T // BLOCK
    n_tiles = S // T
    bf16 = jnp.bfloat16

    x2 = x.reshape(S, D)
    smem = pl.BlockSpec(memory_space=pltpu.SMEM)
    full = lambda shape: pl.BlockSpec(shape, lambda i: (0,) * len(shape))

    out = pl.pallas_call(
        functools.partial(_layer_kernel, n_tiles=n_tiles),
        grid=(n_tiles + 2,),
        in_specs=[
            smem, smem,
            pl.BlockSpec((T, D), lambda i: (jnp.minimum(i, n_tiles - 1), 0)),
            pl.BlockSpec((T, D), lambda i: (jnp.maximum(i - 2, 0), 0)),
            full((1, D)),
            full((D, D_IN)),
            full((len(POOL_WINDOWS), POOL_GROUP, POOL_GROUP)),
            full((1, D_POOL)),
            full((D, D)),
            full((1, D)),
        ],
        out_specs=pl.BlockSpec((T, D), lambda i: (jnp.maximum(i - 2, 0), 0)),
        out_shape=jax.ShapeDtypeStruct((S, D), x.dtype),
        scratch_shapes=[
            pltpu.VMEM((3, N_Q_HEADS, BLOCK, 3 * BLOCK), jnp.float32),
            pltpu.VMEM((2, T, D_ATTN), bf16),
            pltpu.VMEM((2, 4, T + 2 * BLOCK, LANES), bf16),
            pltpu.VMEM((2, N_KV_HEADS, (R + 2) * 2 * BLOCK, LANES), bf16),
            pltpu.VMEM((2, T + 2 * POOL_HALO, D_POOL), jnp.float32),
            pltpu.VMEM((2, T, D_ATTN), jnp.float32),
            pltpu.VMEM((2, T, D_POOL), jnp.float32),
            pltpu.VMEM((2, T, D), bf16),
            pltpu.VMEM((T, D), bf16),
            pltpu.VMEM((len(POOL_WINDOWS) // 2, CHUNK, CHUNK), bf16),
            pltpu.VMEM((3 * 2 * BLOCK, LANES), bf16),
            pltpu.VMEM((D, D), bf16),
            pltpu.VMEM((D, D_IN), bf16),
        ],
        compiler_params=pltpu.CompilerParams(
            dimension_semantics=("arbitrary",),
            vmem_limit_bytes=VMEM_LIMIT_BYTES,
        ),
        name="hymba_layer_fused",
    )(
        rel_bias.T, attn_sink,
        x2, x2,
        pre_norm_g,
        w_in.reshape(D, D_IN),
        pool_w.reshape(len(POOL_WINDOWS), POOL_GROUP, POOL_GROUP),
        pool_scale,
        w_out.reshape(D, D),
        post_norm_g,
    )
    return out.reshape(B, S, D)
```

```python
import functools
import math

import jax
import jax.numpy as jnp
from jax import lax
from jax.experimental import pallas as pl
from jax.experimental.pallas import tpu as pltpu

D_MODEL = 1024
D_ATTN = 512
D_POOL = 512
HEAD_DIM = 64
N_Q_HEADS = 8
N_KV_HEADS = 2
Q_PER_KV = 4
D_KV = 128
WINDOW = 128
BLOCK = 128
N_BUCKETS = 32
POOL_WINDOWS = (2, 4, 8, 16)
POOL_GROUP = 128
D_IN = 2304
EPS = 1e-6
NEG = -1e30
LOG2E = math.log2(math.e)

LANES = 128
CHUNK = 2 * LANES
POOL_HALO = 16
TILE = 512
VMEM_LIMIT_BYTES = 62 * 1024 * 1024

Q0, K0, V0, GA0, U0, GP0 = 0, 512, 640, 768, 1280, 1792

_NT = (((1,), (1,)), ((), ()))


def _silu(x):
    h = 0.5 * x
    return h + h * jnp.tanh(h)


def _t5_bucket_exact(rel):
    nb = N_BUCKETS // 2
    max_exact = nb // 2
    ret = jnp.where(rel > 0, nb, 0)
    n = jnp.abs(rel)
    n2 = n * n
    large = jnp.full(rel.shape, max_exact, jnp.int32)
    for j in range(1, nb - max_exact):
        large = large + jnp.where(n2 >= (max_exact * max_exact) << j, 1, 0)
    return ret + jnp.where(n < max_exact, n, large)


def _window_sums(ug, w):
    n = ug.shape[0]
    if w == 2:
        return ug + pltpu.roll(ug, 1, axis=0)
    acc = ug + pltpu.roll(ug, n - 1, axis=0)
    span = 2
    while span * 2 < w:
        acc = acc + pltpu.roll(acc, n - span, axis=0)
        span *= 2
    return acc + pltpu.roll(acc, span, axis=0)


def _layer_kernel(relb_ref, sink_ref,
                  xa_ref, xb_ref, gpre_ref, win_ref, poolw_ref, pscale_ref, wout_ref, gpost_ref,
                  out_ref,
                  bias_scr, q_scr, k_scr, v_scr, u_scr, sga_scr, sgp_scr, mix_scr, xn_scr, pw2_scr, ones_scr,
                  wout_scr, win_scr,
                  *, n_tiles):
    i = pl.program_id(0)
    T = TILE
    R = T // BLOCK
    D = D_MODEL
    f32, bf16 = jnp.float32, jnp.bfloat16
    lane = lax.broadcasted_iota(jnp.int32, (1, LANES), 1)
    lo = lane < HEAD_DIM

    @pl.when(i == 0)
    def _():
        TOEP = 4 * BLOCK
        rel_row = lax.broadcasted_iota(jnp.int32, (8, TOEP), 1) - (TOEP // 2 - 1)
        bucket_row = _t5_bucket_exact(rel_row)
        band_row = jnp.abs(rel_row) <= WINDOW
        kj = lax.broadcasted_iota(jnp.int32, (BLOCK, 3 * BLOCK), 1)
        for h in range(N_Q_HEADS):
            def body(b, acc, h=h):
                return jnp.where(bucket_row == b, relb_ref[h, b], acc)
            row = lax.fori_loop(0, N_BUCKETS, body, jnp.zeros((8, TOEP), f32)) * LOG2E
            row = jnp.where(band_row, row, NEG)[0:1, :]
            tbl = pltpu.roll(jnp.broadcast_to(row, (BLOCK, TOEP)), BLOCK + TOEP // 2 + 1, axis=1,
                             stride=1, stride_axis=0)[:, :3 * BLOCK]
            bias_scr[0, h] = tbl
            bias_scr[1, h] = jnp.where(kj >= BLOCK, tbl, NEG)
            bias_scr[2, h] = jnp.where(kj < 2 * BLOCK, tbl, NEG)
        k_scr[0, :, 0:BLOCK, :] = jnp.zeros((4, BLOCK, LANES), bf16)
        v_scr[0, :, 0:2 * BLOCK, :] = jnp.zeros((N_KV_HEADS, 2 * BLOCK, LANES), bf16)
        u_scr[0, 0:POOL_HALO, :] = jnp.zeros((POOL_HALO, D_POOL), f32)
        pw2_scr[...] = jnp.zeros(pw2_scr.shape, bf16)
        for gi in range(len(POOL_WINDOWS)):
            r0 = (gi % 2) * POOL_GROUP
            pw2_scr[gi // 2, r0:r0 + POOL_GROUP, r0:r0 + POOL_GROUP] = poolw_ref[gi].astype(bf16)
        row_lo = lax.broadcasted_iota(jnp.int32, (2 * BLOCK, LANES), 0) < BLOCK
        ones_pat = jnp.where(row_lo, jnp.where(lo, 1.0, 0.0), jnp.where(lo, 0.0, 1.0)).astype(bf16)
        for kb in range(3):
            ones_scr[kb * 2 * BLOCK:(kb + 1) * 2 * BLOCK, :] = ones_pat
        g_rows = jnp.broadcast_to(gpre_ref[...], (LANES, D))
        for kb in range(D // LANES):
            rows = slice(kb * LANES, (kb + 1) * LANES)
            g_col = jnp.transpose(g_rows[:, rows])
            g_col = jnp.concatenate([g_col, g_col], axis=1)
            for c in range(D_IN // CHUNK):
                cs = slice(c * CHUNK, (c + 1) * CHUNK)
                win_scr[rows, cs] = (win_ref[rows, cs] * g_col).astype(bf16)
        for c in range(D // CHUNK):
            wout_scr[:, c * CHUNK:(c + 1) * CHUNK] = wout_ref[:, c * CHUNK:(c + 1) * CHUNK].astype(bf16)

    b_is_first = i == 1
    b_is_last = i == n_tiles
    tile_b = jnp.clip(i - 1, 0, n_tiles - 1)

    def run_step(pa, pb, do_a=True, do_b=True, do_c=True):
        enabled = {"a": do_a, "b": do_b, "c": do_c}
        zero = jnp.zeros((), bf16)

        def proj(c0, width=CHUNK):
            return jnp.dot(xn_scr[...], win_scr[:, c0:c0 + width], preferred_element_type=f32)

        def a_kv():
            kv = proj(K0, 2 * D_KV)
            k2 = kv[:, :D_KV].astype(bf16)
            kr = pltpu.roll(kv[:, :D_KV], HEAD_DIM, axis=1).astype(bf16)
            kvars = (jnp.where(lo, k2, zero), jnp.where(lo, zero, kr),
                     jnp.where(lo, kr, zero), jnp.where(lo, zero, k2))
            for var, kk in enumerate(kvars):
                k_scr[pa, var, BLOCK:BLOCK + T, :] = kk
                k_scr[pb, var, BLOCK + T:, :] = kk[:BLOCK]
            v2 = kv[:, D_KV:].astype(bf16)
            vr = pltpu.roll(kv[:, D_KV:], HEAD_DIM, axis=1).astype(bf16)
            vvars = ((jnp.where(lo, v2, zero), jnp.where(lo, zero, vr)),
                     (jnp.where(lo, vr, zero), jnp.where(lo, zero, v2)))
            for j in range(N_KV_HEADS):
                for par in range(2):
                    vv = vvars[j][par]
                    for kb in range(R):
                        r0 = (kb + 1) * 2 * BLOCK + par * BLOCK
                        v_scr[pa, j, r0:r0 + BLOCK, :] = vv[kb * BLOCK:(kb + 1) * BLOCK]
                    r0 = (R + 1) * 2 * BLOCK + par * BLOCK
                    v_scr[pb, j, r0:r0 + BLOCK, :] = vv[:BLOCK]

        def a_q(c):
            cs = slice(c * CHUNK, (c + 1) * CHUNK)
            q_scr[pa, :, cs] = (proj(Q0 + cs.start) * (HEAD_DIM ** -0.5 * LOG2E)).astype(bf16)

        def a_gate(c0, dst, c):
            cs = slice(c * CHUNK, (c + 1) * CHUNK)
            dst[pa, :, cs] = _silu(proj(c0 + cs.start))

        def a_u(c):
            cs = slice(c * CHUNK, (c + 1) * CHUNK)
            u = proj(U0 + cs.start)
            u_scr[pa, POOL_HALO:POOL_HALO + T, cs] = u
            u_scr[pb, POOL_HALO + T:, cs] = jnp.where(b_is_last, 0.0, u[:POOL_HALO])

        units = [(b, j) for b in range(R) for j in range(N_KV_HEADS)]
        assert len(units) == 8, "the program order below is written for eight attention units per tile"

        def b_scores(n):
            b, j = units[n]
            rows = slice(b * BLOCK, (b + 1) * BLOCK)
            win = slice(b * BLOCK, b * BLOCK + 3 * BLOCK)
            c0 = j * Q_PER_KV * HEAD_DIM
            q2 = jnp.concatenate([q_scr[pb, rows, c0:c0 + LANES],
                                  q_scr[pb, rows, c0 + LANES:c0 + 2 * LANES]], axis=0)
            return [lax.dot_general(q2, k_scr[pb, 2 * j + par, win, :], _NT, preferred_element_type=f32)
                    for par in range(2)]

        def b_soft(n, s2s, par, pd, sink_term):
            b, j = units[n]
            if b == 0:
                tbl = jnp.where(b_is_first, 1, 0)
            elif b == R - 1:
                tbl = jnp.where(b_is_last, 2, 0)
            else:
                tbl = 0
            for half in range(2):
                h = j * Q_PER_KV + 2 * half + par
                s = s2s[par][half * BLOCK:(half + 1) * BLOCK, :] + bias_scr[tbl, h]
                sink = sink_ref[0, h] * LOG2E
                m = jnp.maximum(jnp.max(s, axis=-1, keepdims=True), sink)
                pd[(half, par)] = jnp.exp2(s - m).astype(bf16)
                sink_term[(half, par)] = jnp.exp2(sink - m)

        def b_pv(n, pd, sink_term):
            b, j = units[n]
            rows = slice(b * BLOCK, (b + 1) * BLOCK)
            c0 = j * Q_PER_KV * HEAD_DIM
            pcat = jnp.concatenate(
                [jnp.concatenate([pd[(half, par)][:, kb * BLOCK:(kb + 1) * BLOCK]
                                  for kb in range(3) for par in range(2)], axis=1)
                 for half in range(2)], axis=0)
            vwin = jnp.concatenate([v_scr[pb, j, 2 * b * BLOCK:2 * b * BLOCK + 6 * BLOCK, :], ones_scr[...]],
                                   axis=1)
            o2 = jnp.dot(pcat, vwin, preferred_element_type=f32)
            for half in range(2):
                hr = slice(half * BLOCK, (half + 1) * BLOCK)
                denom = o2[hr, LANES:] + jnp.where(lo, sink_term[(half, 0)], sink_term[(half, 1)])
                cs = c0 + half * LANES
                a = (o2[hr, :LANES] / denom) * sga_scr[pb, rows, cs:cs + LANES]
                mix_scr[pb, rows, cs:cs + LANES] = a.astype(bf16)

        def b_pool_y(gi):
            w = POOL_WINDOWS[gi]
            seq = n_tiles * T
            E = POOL_HALO
            left = w // 2
            right = w - 1 - left
            cols = slice(gi * POOL_GROUP, (gi + 1) * POOL_GROUP)
            ug = u_scr[pb, :, cols]
            wsum = _window_sums(ug, w)[E:E + T]
            uc = ug[E:E + T]
            y_mid = wsum * (1.0 / w) - uc
            parts = []
            for r0 in (0, T - E):
                t_glob = tile_b * T + r0 + lax.broadcasted_iota(jnp.int32, (E, LANES), 0)
                cnt = (jnp.minimum(t_glob + right + 1, seq) - jnp.maximum(t_glob - left, 0)).astype(f32)
                parts.append(wsum[r0:r0 + E] / cnt - uc[r0:r0 + E])
            return jnp.concatenate([parts[0], y_mid[E:T - E], parts[1]], axis=0).astype(bf16)

        def b_pool_pair(pr):
            cols = slice(pr * CHUNK, (pr + 1) * CHUNK)
            y2 = jnp.concatenate([b_pool_y(2 * pr), b_pool_y(2 * pr + 1)], axis=1)
            yw = jnp.dot(y2, pw2_scr[pr], preferred_element_type=f32)
            pz = (yw * pscale_ref[:, cols]) * sgp_scr[pb, :, cols]
            mix_scr[pb, :, D_ATTN + pr * CHUNK:D_ATTN + (pr + 1) * CHUNK] = pz.astype(bf16)

        half_cols = D // 2
        mixed_halves = []

        def c_out(ch):
            m = jnp.dot(mix_scr[pa], wout_scr[:, ch * half_cols:(ch + 1) * half_cols], preferred_element_type=f32)
            mixed_halves.append((m, jnp.sum(m * m, axis=-1, keepdims=True)))

        def c_post(rb):
            rows = slice(rb * BLOCK, (rb + 1) * BLOCK)
            mixed = jnp.concatenate([m[rows] for m, _ in mixed_halves], axis=1)
            ssq = mixed_halves[0][1][rows]
            for _, part_ssq in mixed_halves[1:]:
                ssq = ssq + part_ssq[rows]
            y = (mixed * lax.rsqrt(ssq * (1.0 / D) + EPS)) * gpost_ref[...]
            out_ref[rows, :] = xb_ref[rows, :] + y

        PN = BLOCK // 2

        def prenorm(r8):
            rows = slice(r8 * PN, (r8 + 1) * PN)
            xv = xa_ref[rows, :]
            ms = jnp.mean(xv * xv, axis=-1, keepdims=True)
            xn_scr[rows, :] = (xv * lax.rsqrt(ms + EPS)).astype(bf16)

        part = functools.partial
        out_q = lambda ch: ("c", part(c_out, ch))
        post = lambda rb: ("c", part(c_post, rb))
        ga = lambda c: ("a", part(a_gate, GA0, sga_scr, c))
        gp = lambda c: ("a", part(a_gate, GP0, sgp_scr, c))
        uu = lambda c: ("a", part(a_u, c))
        qq = lambda c: ("a", part(a_q, c))
        kv = ("a", a_kv)
        pn = lambda r8: ("a", part(prenorm, r8))
        heavy = [[out_q(0), None, None],
                 [out_q(1), None, None],
                 [kv, uu(0), None],
                 [uu(1), qq(0), None],
                 [qq(1), ga(0), None],
                 [ga(1), None, None],
                 [gp(0), None, None],
                 [gp(1), None, None]]
        assert len(heavy) == len(units)
        slot_end = {0: [pn(4), pn(5), pn(6), pn(7)],
                    1: [post(0), post(1)],
                    2: [post(2), post(3)],
                    3: [("b", part(b_pool_pair, 0))],
                    4: [("b", part(b_pool_pair, 1))]}

        def run(piece):
            if piece is not None and enabled[piece[0]]:
                piece[1]()

        if do_a:
            for r8 in range(T // PN // 2):
                prenorm(r8)
        elif do_b:
            u_scr[pb, POOL_HALO + T:, :] = jnp.zeros((POOL_HALO, D_POOL), f32)
        s = {0: b_scores(0)} if do_b else {}
        for n in range(len(units)):
            if do_b and n + 1 < len(units):
                s[n + 1] = b_scores(n + 1)
            pd, sink_term = {}, {}
            run(heavy[n][0])
            if do_b:
                b_soft(n, s[n], 0, pd, sink_term)
            run(heavy[n][1])
            if do_b:
                b_soft(n, s.pop(n), 1, pd, sink_term)
            run(heavy[n][2])
            if do_b:
                b_pv(n, pd, sink_term)
            for piece in slot_end.get(n, ()):
                run(piece)

        if do_a:
            k_scr[pb, :, 0:BLOCK, :] = k_scr[pa, :, T:T + BLOCK, :]
            v_scr[pb, :, 0:2 * BLOCK, :] = v_scr[pa, :, R * 2 * BLOCK:(R + 1) * 2 * BLOCK, :]
            u_scr[pb, 0:POOL_HALO, :] = u_scr[pa, T:T + POOL_HALO, :]

    steady = (i >= 2) & (i < n_tiles)
    for pa in range(2):
        @pl.when(steady & (i % 2 == pa))
        def _(pa=pa):
            run_step(pa, 1 - pa)

    def ramp(step, **stages):
        @pl.when(i == step)
        def _():
            run_step(step % 2, 1 - step % 2, **stages)

    ramp(0, do_b=False, do_c=False)
    ramp(1, do_c=False)
    ramp(n_tiles, do_a=False)
    ramp(n_tiles + 1, do_a=False, do_b=False)


def kernel(x, pre_norm_g, w_in, rel_bias, attn_sink, pool_w, pool_scale, w_out, post_norm_g):
    B, S, D = x.shape
    assert B == 1 and D == D_MODEL and S % TILE == 0 and S // TILE >= 3
    assert pre_norm_g.shape == (1, D) and attn_sink.shape == (1, N_Q_HEADS), "single layer"
    assert rel_bias.shape == (N_BUCKETS, N_Q_HEADS) and rel_bias.dtype == attn_sink.dtype == jnp.float32
    T = TILE
    R = T // BLOCK
    n_tiles = S // T
    bf16 = jnp.bfloat16

    x2 = x.reshape(S, D)
    smem = pl.BlockSpec(memory_space=pltpu.SMEM)
    full = lambda shape: pl.BlockSpec(shape, lambda i: (0,) * len(shape))

    out = pl.pallas_call(
        functools.partial(_layer_kernel, n_tiles=n_tiles),
        grid=(n_tiles + 2,),
        in_specs=[
            smem, smem,
            pl.BlockSpec((T, D), lambda i: (jnp.minimum(i, n_tiles - 1), 0)),
            pl.BlockSpec((T, D), lambda i: (jnp.maximum(i - 2, 0), 0)),
            full((1, D)),
            full((D, D_IN)),
            full((len(POOL_WINDOWS), POOL_GROUP, POOL_GROUP)),
            full((1, D_POOL)),
            full((D, D)),
            full((1, D)),
        ],
        out_specs=pl.BlockSpec((T, D), lambda i: (jnp.maximum(i - 2, 0), 0)),
        out_shape=jax.ShapeDtypeStruct((S, D), x.dtype),
        scratch_shapes=[
            pltpu.VMEM((3, N_Q_HEADS, BLOCK, 3 * BLOCK), jnp.float32),
            pltpu.VMEM((2, T, D_ATTN), bf16),
            pltpu.VMEM((2, 4, T + 2 * BLOCK, LANES), bf16),
            pltpu.VMEM((2, N_KV_HEADS, (R + 2) * 2 * BLOCK, LANES), bf16),
            pltpu.VMEM((2, T + 2 * POOL_HALO, D_POOL), jnp.float32),
            pltpu.VMEM((2, T, D_ATTN), jnp.float32),
            pltpu.VMEM((2, T, D_POOL), jnp.float32),
            pltpu.VMEM((2, T, D), bf16),
            pltpu.VMEM((T, D), bf16),
            pltpu.VMEM((len(POOL_WINDOWS) // 2, CHUNK, CHUNK), bf16),
            pltpu.VMEM((3 * 2 * BLOCK, LANES), bf16),
            pltpu.VMEM((D, D), bf16),
            pltpu.VMEM((D, D_IN), bf16),
        ],
        compiler_params=pltpu.CompilerParams(
            dimension_semantics=("arbitrary",),
            vmem_limit_bytes=VMEM_LIMIT_BYTES,
        ),
        name="hymba_layer_fused",
    )(
        rel_bias.T, attn_sink,
        x2, x2,
        pre_norm_g,
        w_in.reshape(D, D_IN),
        pool_w.reshape(len(POOL_WINDOWS), POOL_GROUP, POOL_GROUP),
        pool_scale,
        w_out.reshape(D, D),
        post_norm_g,
    )
    return out.reshape(B, S, D)
```

```python
import functools
import math

import jax
import jax.numpy as jnp
from jax import lax
from jax.experimental import pallas as pl
from jax.experimental.pallas import tpu as pltpu

D_MODEL = 1024
D_ATTN = 512
D_POOL = 512
HEAD_DIM = 64
N_Q_HEADS = 8
N_KV_HEADS = 2
Q_PER_KV = 4
D_KV = 128
WINDOW = 128
BLOCK = 128
N_BUCKETS = 32
POOL_WINDOWS = (2, 4, 8, 16)
POOL_GROUP = 128
D_IN = 2304
EPS = 1e-6
NEG = -1e30
LOG2E = math.log2(math.e)

LANES = 128
CHUNK = 2 * LANES
POOL_HALO = 16
TILE = 512
VMEM_LIMIT_BYTES = 62 * 1024 * 1024

Q0, K0, V0, GA0, U0, GP0 = 0, 512, 640, 768, 1280, 1792

_NT = (((1,), (1,)), ((), ()))


def _silu_of_half(h):
    return h + h * jnp.tanh(h)


def _t5_bucket_exact(rel):
    nb = N_BUCKETS // 2
    max_exact = nb // 2
    ret = jnp.where(rel > 0, nb, 0)
    n = jnp.abs(rel)
    n2 = n * n
    large = jnp.full(rel.shape, max_exact, jnp.int32)
    for j in range(1, nb - max_exact):
        large = large + jnp.where(n2 >= (max_exact * max_exact) << j, 1, 0)
    return ret + jnp.where(n < max_exact, n, large)


def _window_sums(ug, w):
    n = ug.shape[0]
    if w == 2:
        return ug + pltpu.roll(ug, 1, axis=0)
    acc = ug + pltpu.roll(ug, n - 1, axis=0)
    span = 2
    while span * 2 < w:
        acc = acc + pltpu.roll(acc, n - span, axis=0)
        span *= 2
    return acc + pltpu.roll(acc, span, axis=0)


def _layer_kernel(relb_ref, sink_ref,
                  xa_ref, xb_ref, gpre_ref, win_hbm, poolw_ref, pscale_ref, wout_hbm, gpost_ref,
                  out_ref,
                  bias_scr, q_scr, k_scr, v_scr, u_scr, sga_scr, sgp_scr, mix_scr, xn_scr, pw2_scr, ones_scr,
                  wout_scr, win_scr, win_f32, wout_f32, w_sem,
                  *, n_tiles):
    i = pl.program_id(0)
    T = TILE
    R = T // BLOCK
    D = D_MODEL
    f32, bf16 = jnp.float32, jnp.bfloat16
    lane = lax.broadcasted_iota(jnp.int32, (1, LANES), 1)
    lo = lane < HEAD_DIM

    @pl.when(i == 0)
    def _():
        n_wchunks = D // LANES

        def w_copy(hbm, dst, kb, sem0):
            rows = pl.ds(kb * LANES, LANES)
            return pltpu.make_async_copy(hbm.at[rows, :], dst.at[rows, :], w_sem.at[sem0 + kb])

        win_copy = functools.partial(w_copy, win_hbm, win_f32, sem0=0)
        wout_copy = functools.partial(w_copy, wout_hbm, wout_f32, sem0=n_wchunks)
        for kb in range(n_wchunks):
            win_copy(kb=kb).start()
        for kb in range(n_wchunks):
            wout_copy(kb=kb).start()

        TOEP = 4 * BLOCK
        rel_row = lax.broadcasted_iota(jnp.int32, (8, TOEP), 1) - (TOEP // 2 - 1)
        bucket_row = _t5_bucket_exact(rel_row)
        band_row = jnp.abs(rel_row) <= WINDOW
        kj = lax.broadcasted_iota(jnp.int32, (BLOCK, 3 * BLOCK), 1)
        for h in range(N_Q_HEADS):
            def body(b, acc, h=h):
                return jnp.where(bucket_row == b, relb_ref[h, b], acc)
            row = lax.fori_loop(0, N_BUCKETS, body, jnp.zeros((8, TOEP), f32)) * LOG2E
            row = jnp.where(band_row, row, NEG)[0:1, :]
            tbl = pltpu.roll(jnp.broadcast_to(row, (BLOCK, TOEP)), BLOCK + TOEP // 2 + 1, axis=1,
                             stride=1, stride_axis=0)[:, :3 * BLOCK]
            bias_scr[0, h] = tbl
            bias_scr[1, h] = jnp.where(kj >= BLOCK, tbl, NEG)
            bias_scr[2, h] = jnp.where(kj < 2 * BLOCK, tbl, NEG)
        k_scr[0, :, 0:BLOCK, :] = jnp.zeros((4, BLOCK, LANES), bf16)
        v_scr[0, :, 0:2 * BLOCK, :] = jnp.zeros((N_KV_HEADS, 2 * BLOCK, LANES), bf16)
        u_scr[0, 0:POOL_HALO, :] = jnp.zeros((POOL_HALO, D_POOL), f32)
        pw2_scr[...] = jnp.zeros(pw2_scr.shape, bf16)
        for gi in range(len(POOL_WINDOWS)):
            r0 = (gi % 2) * POOL_GROUP
            pw2_scr[gi // 2, r0:r0 + POOL_GROUP, r0:r0 + POOL_GROUP] = poolw_ref[gi].astype(bf16)
        row_lo = lax.broadcasted_iota(jnp.int32, (2 * BLOCK, LANES), 0) < BLOCK
        ones_pat = jnp.where(row_lo, jnp.where(lo, 1.0, 0.0), jnp.where(lo, 0.0, 1.0)).astype(bf16)
        for kb in range(3):
            ones_scr[kb * 2 * BLOCK:(kb + 1) * 2 * BLOCK, :] = ones_pat
        g_rows = jnp.broadcast_to(gpre_ref[...], (LANES, D))
        for kb in range(D // LANES):
            rows = slice(kb * LANES, (kb + 1) * LANES)
            g_col = jnp.transpose(g_rows[:, rows])
            g_col = jnp.concatenate([g_col, g_col], axis=1)
            g_half = 0.5 * g_col
            win_copy(kb=kb).wait()
            for c in range(D_IN // CHUNK):
                cs = slice(c * CHUNK, (c + 1) * CHUNK)
                is_gate = GA0 <= cs.start < U0 or cs.start >= GP0
                win_scr[rows, cs] = (win_f32[rows, cs] * (g_half if is_gate else g_col)).astype(bf16)
        for kb in range(n_wchunks):
            rows = slice(kb * LANES, (kb + 1) * LANES)
            wout_copy(kb=kb).wait()
            wout_scr[rows, :] = wout_f32[rows, :].astype(bf16)

    b_is_first = i == 1
    b_is_last = i == n_tiles
    tile_b = jnp.clip(i - 1, 0, n_tiles - 1)

    def run_step(pa, pb, do_a=True, do_b=True, do_c=True):
        enabled = {"a": do_a, "b": do_b, "c": do_c}
        zero = jnp.zeros((), bf16)

        def proj(c0, width=CHUNK):
            return jnp.dot(xn_scr[...], win_scr[:, c0:c0 + width], preferred_element_type=f32)

        def a_kv():
            kv = proj(K0, 2 * D_KV)
            k2 = kv[:, :D_KV].astype(bf16)
            kr = pltpu.roll(kv[:, :D_KV], HEAD_DIM, axis=1).astype(bf16)
            kvars = (jnp.where(lo, k2, zero), jnp.where(lo, zero, kr),
                     jnp.where(lo, kr, zero), jnp.where(lo, zero, k2))
            for var, kk in enumerate(kvars):
                k_scr[pa, var, BLOCK:BLOCK + T, :] = kk
                k_scr[pb, var, BLOCK + T:, :] = kk[:BLOCK]
            v2 = kv[:, D_KV:].astype(bf16)
            vr = pltpu.roll(kv[:, D_KV:], HEAD_DIM, axis=1).astype(bf16)
            vvars = ((jnp.where(lo, v2, zero), jnp.where(lo, zero, vr)),
                     (jnp.where(lo, vr, zero), jnp.where(lo, zero, v2)))
            for j in range(N_KV_HEADS):
                for par in range(2):
                    vv = vvars[j][par]
                    for kb in range(R):
                        r0 = (kb + 1) * 2 * BLOCK + par * BLOCK
                        v_scr[pa, j, r0:r0 + BLOCK, :] = vv[kb * BLOCK:(kb + 1) * BLOCK]
                    r0 = (R + 1) * 2 * BLOCK + par * BLOCK
                    v_scr[pb, j, r0:r0 + BLOCK, :] = vv[:BLOCK]

        def a_q(c):
            cs = slice(c * CHUNK, (c + 1) * CHUNK)
            q_scr[pa, :, cs] = (proj(Q0 + cs.start) * (HEAD_DIM ** -0.5 * LOG2E)).astype(bf16)

        def a_gate(c0, dst, c):
            cs = slice(c * CHUNK, (c + 1) * CHUNK)
            dst[pa, :, cs] = _silu_of_half(proj(c0 + cs.start))

        def a_u(c):
            cs = slice(c * CHUNK, (c + 1) * CHUNK)
            u = proj(U0 + cs.start)
            u_scr[pa, POOL_HALO:POOL_HALO + T, cs] = u
            u_scr[pb, POOL_HALO + T:, cs] = jnp.where(b_is_last, 0.0, u[:POOL_HALO])

        units = [(b, j) for b in range(R) for j in range(N_KV_HEADS)]
        assert len(units) == 8, "the program order below is written for eight attention units per tile"

        def b_scores(n):
            b, j = units[n]
            rows = slice(b * BLOCK, (b + 1) * BLOCK)
            win = slice(b * BLOCK, b * BLOCK + 3 * BLOCK)
            c0 = j * Q_PER_KV * HEAD_DIM
            q2 = jnp.concatenate([q_scr[pb, rows, c0:c0 + LANES],
                                  q_scr[pb, rows, c0 + LANES:c0 + 2 * LANES]], axis=0)
            return [lax.dot_general(q2, k_scr[pb, 2 * j + par, win, :], _NT, preferred_element_type=f32)
                    for par in range(2)]

        def b_soft(n, s2s, par, pd, sink_term):
            b, j = units[n]
            if b == 0:
                tbl = jnp.where(b_is_first, 1, 0)
            elif b == R - 1:
                tbl = jnp.where(b_is_last, 2, 0)
            else:
                tbl = 0
            for half in range(2):
                h = j * Q_PER_KV + 2 * half + par
                s = s2s[par][half * BLOCK:(half + 1) * BLOCK, :] + bias_scr[tbl, h]
                sink = sink_ref[0, h] * LOG2E
                m = jnp.maximum(jnp.max(s, axis=-1, keepdims=True), sink)
                pd[(half, par)] = jnp.exp2(s - m).astype(bf16)
                sink_term[(half, par)] = jnp.exp2(sink - m)

        def b_pv(n, pd, sink_term):
            b, j = units[n]
            rows = slice(b * BLOCK, (b + 1) * BLOCK)
            c0 = j * Q_PER_KV * HEAD_DIM
            pcat = jnp.concatenate(
                [jnp.concatenate([pd[(half, par)][:, kb * BLOCK:(kb + 1) * BLOCK]
                                  for kb in range(3) for par in range(2)], axis=1)
                 for half in range(2)], axis=0)
            vwin = jnp.concatenate([v_scr[pb, j, 2 * b * BLOCK:2 * b * BLOCK + 6 * BLOCK, :], ones_scr[...]],
                                   axis=1)
            o2 = jnp.dot(pcat, vwin, preferred_element_type=f32)
            for half in range(2):
                hr = slice(half * BLOCK, (half + 1) * BLOCK)
                denom = o2[hr, LANES:] + jnp.where(lo, sink_term[(half, 0)], sink_term[(half, 1)])
                cs = c0 + half * LANES
                a = (o2[hr, :LANES] / denom) * sga_scr[pb, rows, cs:cs + LANES]
                mix_scr[pb, rows, cs:cs + LANES] = a.astype(bf16)

        def b_pool_y(gi):
            w = POOL_WINDOWS[gi]
            seq = n_tiles * T
            E = POOL_HALO
            left = w // 2
            right = w - 1 - left
            cols = slice(gi * POOL_GROUP, (gi + 1) * POOL_GROUP)
            ug = u_scr[pb, :, cols]
            wsum = _window_sums(ug, w)[E:E + T]
            uc = ug[E:E + T]
            y_mid = wsum * (1.0 / w) - uc
            parts = []
            for r0 in (0, T - E):
                t_glob = tile_b * T + r0 + lax.broadcasted_iota(jnp.int32, (E, LANES), 0)
                cnt = (jnp.minimum(t_glob + right + 1, seq) - jnp.maximum(t_glob - left, 0)).astype(f32)
                parts.append(wsum[r0:r0 + E] / cnt - uc[r0:r0 + E])
            return jnp.concatenate([parts[0], y_mid[E:T - E], parts[1]], axis=0).astype(bf16)

        def b_pool_pair(pr):
            cols = slice(pr * CHUNK, (pr + 1) * CHUNK)
            y2 = jnp.concatenate([b_pool_y(2 * pr), b_pool_y(2 * pr + 1)], axis=1)
            yw = jnp.dot(y2, pw2_scr[pr], preferred_element_type=f32)
            pz = (yw * pscale_ref[:, cols]) * sgp_scr[pb, :, cols]
            mix_scr[pb, :, D_ATTN + pr * CHUNK:D_ATTN + (pr + 1) * CHUNK] = pz.astype(bf16)

        half_cols = D // 2
        mixed_halves = []

        def c_out(ch):
            m = jnp.dot(mix_scr[pa], wout_scr[:, ch * half_cols:(ch + 1) * half_cols], preferred_element_type=f32)
            mixed_halves.append((m, jnp.sum(m * m, axis=-1, keepdims=True)))

        def c_post(rb):
            rows = slice(rb * BLOCK, (rb + 1) * BLOCK)
            mixed = jnp.concatenate([m[rows] for m, _ in mixed_halves], axis=1)
            ssq = mixed_halves[0][1][rows]
            for _, part_ssq in mixed_halves[1:]:
                ssq = ssq + part_ssq[rows]
            y = (mixed * lax.rsqrt(ssq * (1.0 / D) + EPS)) * gpost_ref[...]
            out_ref[rows, :] = xb_ref[rows, :] + y

        PN = BLOCK // 2

        def prenorm(r8):
            rows = slice(r8 * PN, (r8 + 1) * PN)
            xv = xa_ref[rows, :]
            ms = jnp.mean(xv * xv, axis=-1, keepdims=True)
            xn_scr[rows, :] = (xv * lax.rsqrt(ms + EPS)).astype(bf16)

        part = functools.partial
        out_q = lambda ch: ("c", part(c_out, ch))
        post = lambda rb: ("c", part(c_post, rb))
        ga = lambda c: ("a", part(a_gate, GA0, sga_scr, c))
        gp = lambda c: ("a", part(a_gate, GP0, sgp_scr, c))
        uu = lambda c: ("a", part(a_u, c))
        qq = lambda c: ("a", part(a_q, c))
        kv = ("a", a_kv)
        pn = lambda r8: ("a", part(prenorm, r8))
        heavy = [[out_q(0), None, None],
                 [out_q(1), None, None],
                 [kv, uu(0), None],
                 [uu(1), qq(0), None],
                 [qq(1), ga(0), None],
                 [ga(1), None, None],
                 [gp(0), None, None],
                 [gp(1), None, None]]
        assert len(heavy) == len(units)
        slot_end = {0: [pn(4), pn(5), pn(6), pn(7)],
                    1: [post(0), post(1)],
                    2: [post(2), post(3)],
                    3: [("b", part(b_pool_pair, 0))],
                    4: [("b", part(b_pool_pair, 1))]}

        def run(piece):
            if piece is not None and enabled[piece[0]]:
                piece[1]()

        if do_a:
            for r8 in range(T // PN // 2):
                prenorm(r8)
        elif do_b:
            u_scr[pb, POOL_HALO + T:, :] = jnp.zeros((POOL_HALO, D_POOL), f32)
        s = {0: b_scores(0)} if do_b else {}
        for n in range(len(units)):
            if do_b and n + 1 < len(units):
                s[n + 1] = b_scores(n + 1)
            pd, sink_term = {}, {}
            run(heavy[n][0])
            if do_b:
                b_soft(n, s[n], 0, pd, sink_term)
            run(heavy[n][1])
            if do_b:
                b_soft(n, s.pop(n), 1, pd, sink_term)
            run(heavy[n][2])
            if do_b:
                b_pv(n, pd, sink_term)
            for piece in slot_end.get(n, ()):
                run(piece)

        if do_a:
            k_scr[pb, :, 0:BLOCK, :] = k_scr[pa, :, T:T + BLOCK, :]
            v_scr[pb, :, 0:2 * BLOCK, :] = v_scr[pa, :, R * 2 * BLOCK:(R + 1) * 2 * BLOCK, :]
            u_scr[pb, 0:POOL_HALO, :] = u_scr[pa, T:T + POOL_HALO, :]

    steady = (i >= 2) & (i < n_tiles)
    for pa in range(2):
        @pl.when(steady & (i % 2 == pa))
        def _(pa=pa):
            run_step(pa, 1 - pa)

    def ramp(step, **stages):
        @pl.when(i == step)
        def _():
            run_step(step % 2, 1 - step % 2, **stages)

    ramp(0, do_b=False, do_c=False)
    ramp(1, do_c=False)
    ramp(n_tiles, do_a=False)
    ramp(n_tiles + 1, do_a=False, do_b=False)


def kernel(x, pre_norm_g, w_in, rel_bias, attn_sink, pool_w, pool_scale, w_out, post_norm_g):
    B, S, D = x.shape
    assert B == 1 and D == D_MODEL and S % TILE == 0 and S // TILE >= 3
    assert pre_norm_g.shape == (1, D) and attn_sink.shape == (1, N_Q_HEADS), "single layer"
    assert rel_bias.shape == (N_BUCKETS, N_Q_HEADS) and rel_bias.dtype == attn_sink.dtype == jnp.float32
    T = TILE
    R = T // BLOCK
    n_tiles = S // T
    bf16 = jnp.bfloat16

    x2 = x.reshape(S, D)
    smem = pl.BlockSpec(memory_space=pltpu.SMEM)
    hbm = pl.BlockSpec(memory_space=pltpu.HBM)
    full = lambda shape: pl.BlockSpec(shape, lambda i: (0,) * len(shape))

    out = pl.pallas_call(
        functools.partial(_layer_kernel, n_tiles=n_tiles),
        grid=(n_tiles + 2,),
        in_specs=[
            smem, smem,
            pl.BlockSpec((T, D), lambda i: (jnp.minimum(i, n_tiles - 1), 0)),
            pl.BlockSpec((T, D), lambda i: (jnp.maximum(i - 2, 0), 0)),
            full((1, D)),
            hbm,
            full((len(POOL_WINDOWS), POOL_GROUP, POOL_GROUP)),
            full((1, D_POOL)),
            hbm,
            full((1, D)),
        ],
        out_specs=pl.BlockSpec((T, D), lambda i: (jnp.maximum(i - 2, 0), 0)),
        out_shape=jax.ShapeDtypeStruct((S, D), x.dtype),
        scratch_shapes=[
            pltpu.VMEM((3, N_Q_HEADS, BLOCK, 3 * BLOCK), jnp.float32),
            pltpu.VMEM((2, T, D_ATTN), bf16),
            pltpu.VMEM((2, 4, T + 2 * BLOCK, LANES), bf16),
            pltpu.VMEM((2, N_KV_HEADS, (R + 2) * 2 * BLOCK, LANES), bf16),
            pltpu.VMEM((2, T + 2 * POOL_HALO, D_POOL), jnp.float32),
            pltpu.VMEM((2, T, D_ATTN), jnp.float32),
            pltpu.VMEM((2, T, D_POOL), jnp.float32),
            pltpu.VMEM((2, T, D), bf16),
            pltpu.VMEM((T, D), bf16),
            pltpu.VMEM((len(POOL_WINDOWS) // 2, CHUNK, CHUNK), bf16),
            pltpu.VMEM((3 * 2 * BLOCK, LANES), bf16),
            pltpu.VMEM((D, D), bf16),
            pltpu.VMEM((D, D_IN), bf16),
            pltpu.VMEM((D, D_IN), jnp.float32),
            pltpu.VMEM((D, D), jnp.float32),
            pltpu.SemaphoreType.DMA((2 * (D // LANES),)),
        ],
        compiler_params=pltpu.CompilerParams(
            dimension_semantics=("arbitrary",),
            vmem_limit_bytes=VMEM_LIMIT_BYTES,
        ),
        name="hymba_layer_fused",
    )(
        rel_bias.T, attn_sink,
        x2, x2,
        pre_norm_g,
        w_in.reshape(D, D_IN),
        pool_w.reshape(len(POOL_WINDOWS), POOL_GROUP, POOL_GROUP),
        pool_scale,
        w_out.reshape(D, D),
        post_norm_g,
    )
    return out.reshape(B, S, D)
```

```python
import functools
import math

import jax
import jax.numpy as jnp
from jax import lax
from jax.experimental import pallas as pl
from jax.experimental.pallas import tpu as pltpu

D_MODEL = 1024
D_ATTN = 512
D_POOL = 512
HEAD_DIM = 64
N_Q_HEADS = 8
N_KV_HEADS = 2
Q_PER_KV = 4
D_KV = 128
WINDOW = 128
BLOCK = 128
N_BUCKETS = 32
POOL_WINDOWS = (2, 4, 8, 16)
POOL_GROUP = 128
D_IN = 2304
EPS = 1e-6
NEG = -1e30
LOG2E = math.log2(math.e)

LANES = 128
CHUNK = 2 * LANES
POOL_HALO = 16
TILE = 512
VMEM_LIMIT_BYTES = 62 * 1024 * 1024

Q0, K0, V0, GA0, U0, GP0 = 0, 512, 640, 768, 1280, 1792

_NT = (((1,), (1,)), ((), ()))


def _silu_of_half(h):
    return h + h * jnp.tanh(h)


def _t5_bucket_exact(rel):
    nb = N_BUCKETS // 2
    max_exact = nb // 2
    ret = jnp.where(rel > 0, nb, 0)
    n = jnp.abs(rel)
    n2 = n * n
    large = jnp.full(rel.shape, max_exact, jnp.int32)
    for j in range(1, nb - max_exact):
        large = large + jnp.where(n2 >= (max_exact * max_exact) << j, 1, 0)
    return ret + jnp.where(n < max_exact, n, large)


def _window_sums(ug, w):
    n = ug.shape[0]
    if w == 2:
        return ug + pltpu.roll(ug, 1, axis=0)
    acc = ug + pltpu.roll(ug, n - 1, axis=0)
    span = 2
    while span * 2 < w:
        acc = acc + pltpu.roll(acc, n - span, axis=0)
        span *= 2
    return acc + pltpu.roll(acc, span, axis=0)


def _layer_kernel(relb_ref, sink_ref,
                  xa_ref, xb_ref, gpre_ref, win_hbm, poolw_ref, pscale_ref, wout_hbm, gpost_ref,
                  out_ref,
                  bias_scr, q_scr, k_scr, v_scr, u_scr, sga_scr, sgp_scr, mix_scr, xn_scr, pw2_scr, ones_scr,
                  wout_scr, win_scr, win_f32, wout_f32, w_sem,
                  *, n_tiles):
    i = pl.program_id(0)
    T = TILE
    R = T // BLOCK
    D = D_MODEL
    f32, bf16 = jnp.float32, jnp.bfloat16
    lane = lax.broadcasted_iota(jnp.int32, (1, LANES), 1)
    lo = lane < HEAD_DIM

    n_wchunks = D // LANES

    def w_copy(hbm, dst, kb, sem0):
        rows = pl.ds(kb * LANES, LANES)
        return pltpu.make_async_copy(hbm.at[rows, :], dst.at[rows, :], w_sem.at[sem0 + kb])

    win_copy = functools.partial(w_copy, win_hbm, win_f32, sem0=0)
    wout_copy = functools.partial(w_copy, wout_hbm, wout_f32, sem0=n_wchunks)

    def convert_wout():
        for kb in range(n_wchunks):
            rows = slice(kb * LANES, (kb + 1) * LANES)
            wout_copy(kb=kb).wait()
            wout_scr[rows, :] = wout_f32[rows, :].astype(bf16)

    @pl.when(i == 0)
    def _():
        for kb in range(n_wchunks):
            win_copy(kb=kb).start()

        TOEP = 4 * BLOCK
        rel_row = lax.broadcasted_iota(jnp.int32, (8, TOEP), 1) - (TOEP // 2 - 1)
        bucket_row = _t5_bucket_exact(rel_row)
        band_row = jnp.abs(rel_row) <= WINDOW
        kj = lax.broadcasted_iota(jnp.int32, (BLOCK, 3 * BLOCK), 1)
        for h in range(N_Q_HEADS):
            def body(b, acc, h=h):
                return jnp.where(bucket_row == b, relb_ref[h, b], acc)
            row = lax.fori_loop(0, N_BUCKETS, body, jnp.zeros((8, TOEP), f32)) * LOG2E
            row = jnp.where(band_row, row, NEG)[0:1, :]
            tbl = pltpu.roll(jnp.broadcast_to(row, (BLOCK, TOEP)), BLOCK + TOEP // 2 + 1, axis=1,
                             stride=1, stride_axis=0)[:, :3 * BLOCK]
            bias_scr[0, h] = tbl
            bias_scr[1, h] = jnp.where(kj >= BLOCK, tbl, NEG)
            bias_scr[2, h] = jnp.where(kj < 2 * BLOCK, tbl, NEG)
        k_scr[0, :, 0:BLOCK, :] = jnp.zeros((4, BLOCK, LANES), bf16)
        v_scr[0, :, 0:2 * BLOCK, :] = jnp.zeros((N_KV_HEADS, 2 * BLOCK, LANES), bf16)
        u_scr[0, 0:POOL_HALO, :] = jnp.zeros((POOL_HALO, D_POOL), f32)
        pw2_scr[...] = jnp.zeros(pw2_scr.shape, bf16)
        for gi in range(len(POOL_WINDOWS)):
            r0 = (gi % 2) * POOL_GROUP
            pw2_scr[gi // 2, r0:r0 + POOL_GROUP, r0:r0 + POOL_GROUP] = poolw_ref[gi].astype(bf16)
        row_lo = lax.broadcasted_iota(jnp.int32, (2 * BLOCK, LANES), 0) < BLOCK
        ones_pat = jnp.where(row_lo, jnp.where(lo, 1.0, 0.0), jnp.where(lo, 0.0, 1.0)).astype(bf16)
        for kb in range(3):
            ones_scr[kb * 2 * BLOCK:(kb + 1) * 2 * BLOCK, :] = ones_pat
        g_rows = jnp.broadcast_to(gpre_ref[...], (LANES, D))
        for kb in range(D // LANES):
            rows = slice(kb * LANES, (kb + 1) * LANES)
            g_col = jnp.transpose(g_rows[:, rows])
            g_col = jnp.concatenate([g_col, g_col], axis=1)
            g_half = 0.5 * g_col
            win_copy(kb=kb).wait()
            for c in range(D_IN // CHUNK):
                cs = slice(c * CHUNK, (c + 1) * CHUNK)
                is_gate = GA0 <= cs.start < U0 or cs.start >= GP0
                win_scr[rows, cs] = (win_f32[rows, cs] * (g_half if is_gate else g_col)).astype(bf16)
        for kb in range(n_wchunks):
            wout_copy(kb=kb).start()

    b_is_first = i == 1
    b_is_last = i == n_tiles
    tile_b = jnp.clip(i - 1, 0, n_tiles - 1)

    def run_step(pa, pb, do_a=True, do_b=True, do_c=True):
        enabled = {"a": do_a, "b": do_b, "c": do_c}
        zero = jnp.zeros((), bf16)

        def proj(c0, width=CHUNK):
            return jnp.dot(xn_scr[...], win_scr[:, c0:c0 + width], preferred_element_type=f32)

        def a_kv():
            kv = proj(K0, 2 * D_KV)
            k2 = kv[:, :D_KV].astype(bf16)
            kr = pltpu.roll(kv[:, :D_KV], HEAD_DIM, axis=1).astype(bf16)
            kvars = (jnp.where(lo, k2, zero), jnp.where(lo, zero, kr),
                     jnp.where(lo, kr, zero), jnp.where(lo, zero, k2))
            for var, kk in enumerate(kvars):
                k_scr[pa, var, BLOCK:BLOCK + T, :] = kk
                k_scr[pb, var, BLOCK + T:, :] = kk[:BLOCK]
            v2 = kv[:, D_KV:].astype(bf16)
            vr = pltpu.roll(kv[:, D_KV:], HEAD_DIM, axis=1).astype(bf16)
            vvars = ((jnp.where(lo, v2, zero), jnp.where(lo, zero, vr)),
                     (jnp.where(lo, vr, zero), jnp.where(lo, zero, v2)))
            for j in range(N_KV_HEADS):
                for par in range(2):
                    vv = vvars[j][par]
                    for kb in range(R):
                        r0 = (kb + 1) * 2 * BLOCK + par * BLOCK
                        v_scr[pa, j, r0:r0 + BLOCK, :] = vv[kb * BLOCK:(kb + 1) * BLOCK]
                    r0 = (R + 1) * 2 * BLOCK + par * BLOCK
                    v_scr[pb, j, r0:r0 + BLOCK, :] = vv[:BLOCK]

        def a_q(c):
            cs = slice(c * CHUNK, (c + 1) * CHUNK)
            q_scr[pa, :, cs] = (proj(Q0 + cs.start) * (HEAD_DIM ** -0.5 * LOG2E)).astype(bf16)

        def a_gate(c0, dst, c):
            cs = slice(c * CHUNK, (c + 1) * CHUNK)
            dst[pa, :, cs] = _silu_of_half(proj(c0 + cs.start))

        def a_u(c):
            cs = slice(c * CHUNK, (c + 1) * CHUNK)
            u = proj(U0 + cs.start)
            u_scr[pa, POOL_HALO:POOL_HALO + T, cs] = u
            u_scr[pb, POOL_HALO + T:, cs] = jnp.where(b_is_last, 0.0, u[:POOL_HALO])

        units = [(b, j) for b in range(R) for j in range(N_KV_HEADS)]
        assert len(units) == 8, "the program order below is written for eight attention units per tile"

        def b_scores(n):
            b, j = units[n]
            rows = slice(b * BLOCK, (b + 1) * BLOCK)
            win = slice(b * BLOCK, b * BLOCK + 3 * BLOCK)
            c0 = j * Q_PER_KV * HEAD_DIM
            q2 = jnp.concatenate([q_scr[pb, rows, c0:c0 + LANES],
                                  q_scr[pb, rows, c0 + LANES:c0 + 2 * LANES]], axis=0)
            return [lax.dot_general(q2, k_scr[pb, 2 * j + par, win, :], _NT, preferred_element_type=f32)
                    for par in range(2)]

        def b_soft(n, s2s, par, pd, sink_term):
            b, j = units[n]
            if b == 0:
                tbl = jnp.where(b_is_first, 1, 0)
            elif b == R - 1:
                tbl = jnp.where(b_is_last, 2, 0)
            else:
                tbl = 0
            for half in range(2):
                h = j * Q_PER_KV + 2 * half + par
                s = s2s[par][half * BLOCK:(half + 1) * BLOCK, :] + bias_scr[tbl, h]
                sink = sink_ref[0, h] * LOG2E
                m = jnp.maximum(jnp.max(s, axis=-1, keepdims=True), sink)
                pd[(half, par)] = jnp.exp2(s - m).astype(bf16)
                sink_term[(half, par)] = jnp.exp2(sink - m)

        def b_pv(n, pd, sink_term):
            b, j = units[n]
            rows = slice(b * BLOCK, (b + 1) * BLOCK)
            c0 = j * Q_PER_KV * HEAD_DIM
            pcat = jnp.concatenate(
                [jnp.concatenate([pd[(half, par)][:, kb * BLOCK:(kb + 1) * BLOCK]
                                  for kb in range(3) for par in range(2)], axis=1)
                 for half in range(2)], axis=0)
            vwin = jnp.concatenate([v_scr[pb, j, 2 * b * BLOCK:2 * b * BLOCK + 6 * BLOCK, :], ones_scr[...]],
                                   axis=1)
            o2 = jnp.dot(pcat, vwin, preferred_element_type=f32)
            for half in range(2):
                hr = slice(half * BLOCK, (half + 1) * BLOCK)
                denom = o2[hr, LANES:] + jnp.where(lo, sink_term[(half, 0)], sink_term[(half, 1)])
                cs = c0 + half * LANES
                a = (o2[hr, :LANES] / denom) * sga_scr[pb, rows, cs:cs + LANES]
                mix_scr[pb, rows, cs:cs + LANES] = a.astype(bf16)

        def b_pool_y(gi):
            w = POOL_WINDOWS[gi]
            seq = n_tiles * T
            E = POOL_HALO
            left = w // 2
            right = w - 1 - left
            cols = slice(gi * POOL_GROUP, (gi + 1) * POOL_GROUP)
            ug = u_scr[pb, :, cols]
            wsum = _window_sums(ug, w)[E:E + T]
            uc = ug[E:E + T]
            y_mid = wsum * (1.0 / w) - uc
            parts = []
            for r0 in (0, T - E):
                t_glob = tile_b * T + r0 + lax.broadcasted_iota(jnp.int32, (E, LANES), 0)
                cnt = (jnp.minimum(t_glob + right + 1, seq) - jnp.maximum(t_glob - left, 0)).astype(f32)
                parts.append(wsum[r0:r0 + E] / cnt - uc[r0:r0 + E])
            return jnp.concatenate([parts[0], y_mid[E:T - E], parts[1]], axis=0).astype(bf16)

        def b_pool_pair(pr):
            cols = slice(pr * CHUNK, (pr + 1) * CHUNK)
            y2 = jnp.concatenate([b_pool_y(2 * pr), b_pool_y(2 * pr + 1)], axis=1)
            yw = jnp.dot(y2, pw2_scr[pr], preferred_element_type=f32)
            pz = (yw * pscale_ref[:, cols]) * sgp_scr[pb, :, cols]
            mix_scr[pb, :, D_ATTN + pr * CHUNK:D_ATTN + (pr + 1) * CHUNK] = pz.astype(bf16)

        half_cols = D // 2
        mixed_halves = []

        def c_out(ch):
            m = jnp.dot(mix_scr[pa], wout_scr[:, ch * half_cols:(ch + 1) * half_cols], preferred_element_type=f32)
            mixed_halves.append((m, jnp.sum(m * m, axis=-1, keepdims=True)))

        def c_post(rb):
            rows = slice(rb * BLOCK, (rb + 1) * BLOCK)
            mixed = jnp.concatenate([m[rows] for m, _ in mixed_halves], axis=1)
            ssq = mixed_halves[0][1][rows]
            for _, part_ssq in mixed_halves[1:]:
                ssq = ssq + part_ssq[rows]
            y = (mixed * lax.rsqrt(ssq * (1.0 / D) + EPS)) * gpost_ref[...]
            out_ref[rows, :] = xb_ref[rows, :] + y

        PN = BLOCK // 2

        def prenorm(r8):
            rows = slice(r8 * PN, (r8 + 1) * PN)
            xv = xa_ref[rows, :]
            ms = jnp.mean(xv * xv, axis=-1, keepdims=True)
            xn_scr[rows, :] = (xv * lax.rsqrt(ms + EPS)).astype(bf16)

        part = functools.partial
        out_q = lambda ch: ("c", part(c_out, ch))
        post = lambda rb: ("c", part(c_post, rb))
        ga = lambda c: ("a", part(a_gate, GA0, sga_scr, c))
        gp = lambda c: ("a", part(a_gate, GP0, sgp_scr, c))
        uu = lambda c: ("a", part(a_u, c))
        qq = lambda c: ("a", part(a_q, c))
        kv = ("a", a_kv)
        pn = lambda r8: ("a", part(prenorm, r8))
        heavy = [[out_q(0), None, None],
                 [out_q(1), None, None],
                 [kv, uu(0), None],
                 [uu(1), qq(0), None],
                 [qq(1), ga(0), None],
                 [ga(1), None, None],
                 [gp(0), None, None],
                 [gp(1), None, None]]
        assert len(heavy) == len(units)
        slot_end = {0: [pn(4), pn(5), pn(6), pn(7)],
                    1: [post(0), post(1)],
                    2: [post(2), post(3)],
                    3: [("b", part(b_pool_pair, 0))],
                    4: [("b", part(b_pool_pair, 1))]}

        def run(piece):
            if piece is not None and enabled[piece[0]]:
                piece[1]()

        if do_a:
            for r8 in range(T // PN // 2):
                prenorm(r8)
        elif do_b:
            u_scr[pb, POOL_HALO + T:, :] = jnp.zeros((POOL_HALO, D_POOL), f32)
        s = {0: b_scores(0)} if do_b else {}
        for n in range(len(units)):
            if do_b and n + 1 < len(units):
                s[n + 1] = b_scores(n + 1)
            pd, sink_term = {}, {}
            run(heavy[n][0])
            if do_b:
                b_soft(n, s[n], 0, pd, sink_term)
            run(heavy[n][1])
            if do_b:
                b_soft(n, s.pop(n), 1, pd, sink_term)
            run(heavy[n][2])
            if do_b:
                b_pv(n, pd, sink_term)
            for piece in slot_end.get(n, ()):
                run(piece)

        if do_a:
            k_scr[pb, :, 0:BLOCK, :] = k_scr[pa, :, T:T + BLOCK, :]
            v_scr[pb, :, 0:2 * BLOCK, :] = v_scr[pa, :, R * 2 * BLOCK:(R + 1) * 2 * BLOCK, :]
            u_scr[pb, 0:POOL_HALO, :] = u_scr[pa, T:T + POOL_HALO, :]

    steady = (i >= 2) & (i < n_tiles)
    for pa in range(2):
        @pl.when(steady & (i % 2 == pa))
        def _(pa=pa):
            run_step(pa, 1 - pa)

    def ramp(step, before=None, **stages):
        @pl.when(i == step)
        def _():
            if before is not None:
                before()
            run_step(step % 2, 1 - step % 2, **stages)

    ramp(0, do_b=False, do_c=False)
    ramp(1, before=convert_wout, do_c=False)
    ramp(n_tiles, do_a=False)
    ramp(n_tiles + 1, do_a=False, do_b=False)


def kernel(x, pre_norm_g, w_in, rel_bias, attn_sink, pool_w, pool_scale, w_out, post_norm_g):
    B, S, D = x.shape
    assert B == 1 and D == D_MODEL and S % TILE == 0 and S // TILE >= 3
    assert pre_norm_g.shape == (1, D) and attn_sink.shape == (1, N_Q_HEADS), "single layer"
    assert rel_bias.shape == (N_BUCKETS, N_Q_HEADS) and rel_bias.dtype == attn_sink.dtype == jnp.float32
    T = TILE
    R = T // BLOCK
    n_tiles = S // T
    bf16 = jnp.bfloat16

    x2 = x.reshape(S, D)
    smem = pl.BlockSpec(memory_space=pltpu.SMEM)
    hbm = pl.BlockSpec(memory_space=pltpu.HBM)
    full = lambda shape: pl.BlockSpec(shape, lambda i: (0,) * len(shape))

    out = pl.pallas_call(
        functools.partial(_layer_kernel, n_tiles=n_tiles),
        grid=(n_tiles + 2,),
        in_specs=[
            smem, smem,
            pl.BlockSpec((T, D), lambda i: (jnp.minimum(i, n_tiles - 1), 0)),
            pl.BlockSpec((T, D), lambda i: (jnp.maximum(i - 2, 0), 0)),
            full((1, D)),
            hbm,
            full((len(POOL_WINDOWS), POOL_GROUP, POOL_GROUP)),
            full((1, D_POOL)),
            hbm,
            full((1, D)),
        ],
        out_specs=pl.BlockSpec((T, D), lambda i: (jnp.maximum(i - 2, 0), 0)),
        out_shape=jax.ShapeDtypeStruct((S, D), x.dtype),
        scratch_shapes=[
            pltpu.VMEM((3, N_Q_HEADS, BLOCK, 3 * BLOCK), jnp.float32),
            pltpu.VMEM((2, T, D_ATTN), bf16),
            pltpu.VMEM((2, 4, T + 2 * BLOCK, LANES), bf16),
            pltpu.VMEM((2, N_KV_HEADS, (R + 2) * 2 * BLOCK, LANES), bf16),
            pltpu.VMEM((2, T + 2 * POOL_HALO, D_POOL), jnp.float32),
            pltpu.VMEM((2, T, D_ATTN), jnp.float32),
            pltpu.VMEM((2, T, D_POOL), jnp.float32),
            pltpu.VMEM((2, T, D), bf16),
            pltpu.VMEM((T, D), bf16),
            pltpu.VMEM((len(POOL_WINDOWS) // 2, CHUNK, CHUNK), bf16),
            pltpu.VMEM((3 * 2 * BLOCK, LANES), bf16),
            pltpu.VMEM((D, D), bf16),
            pltpu.VMEM((D, D_IN), bf16),
            pltpu.VMEM((D, D_IN), jnp.float32),
            pltpu.VMEM((D, D), jnp.float32),
            pltpu.SemaphoreType.DMA((2 * (D // LANES),)),
        ],
        compiler_params=pltpu.CompilerParams(
            dimension_semantics=("arbitrary",),
            vmem_limit_bytes=VMEM_LIMIT_BYTES,
        ),
        name="hymba_layer_fused",
    )(
        rel_bias.T, attn_sink,
        x2, x2,
        pre_norm_g,
        w_in.reshape(D, D_IN),
        pool_w.reshape(len(POOL_WINDOWS), POOL_GROUP, POOL_GROUP),
        pool_scale,
        w_out.reshape(D, D),
        post_norm_g,
    )
    return out.reshape(B, S, D)
```

```python
import functools
import math

import jax
import jax.numpy as jnp
from jax import lax
from jax.experimental import pallas as pl
from jax.experimental.pallas import tpu as pltpu

D_MODEL = 1024
D_ATTN = 512
D_POOL = 512
HEAD_DIM = 64
N_Q_HEADS = 8
N_KV_HEADS = 2
Q_PER_KV = 4
D_KV = 128
WINDOW = 128
BLOCK = 128
N_BUCKETS = 32
POOL_WINDOWS = (2, 4, 8, 16)
POOL_GROUP = 128
D_IN = 2304
EPS = 1e-6
NEG = -1e30
LOG2E = math.log2(math.e)

LANES = 128
CHUNK = 2 * LANES
POOL_HALO = 16
TILE = 512
VMEM_LIMIT_BYTES = 62 * 1024 * 1024

Q0, K0, V0, GA0, U0, GP0 = 0, 512, 640, 768, 1280, 1792

_NT = (((1,), (1,)), ((), ()))


def _silu_of_half(h):
    return h + h * jnp.tanh(h)


def _t5_bucket_exact(rel):
    nb = N_BUCKETS // 2
    max_exact = nb // 2
    ret = jnp.where(rel > 0, nb, 0)
    n = jnp.abs(rel)
    n2 = n * n
    large = jnp.full(rel.shape, max_exact, jnp.int32)
    for j in range(1, nb - max_exact):
        large = large + jnp.where(n2 >= (max_exact * max_exact) << j, 1, 0)
    return ret + jnp.where(n < max_exact, n, large)


def _window_sums(ug, w):
    n = ug.shape[0]
    if w == 2:
        return ug + pltpu.roll(ug, 1, axis=0)
    acc = ug + pltpu.roll(ug, n - 1, axis=0)
    span = 2
    while span * 2 < w:
        acc = acc + pltpu.roll(acc, n - span, axis=0)
        span *= 2
    return acc + pltpu.roll(acc, span, axis=0)


def _layer_kernel(relb_ref, sink_ref,
                  xa_ref, xb_ref, gpre_ref, win_hbm, poolw_ref, pscale_ref, wout_hbm, gpost_ref,
                  out_ref,
                  bias_scr, q_scr, k_scr, v_scr, u_scr, sga_scr, sgp_scr, mix_scr, xn_scr, pw2_scr, ones_scr,
                  wout_scr, win_scr, win_f32, wout_f32, gcol_scr, w_sem,
                  *, n_tiles):
    i = pl.program_id(0)
    T = TILE
    R = T // BLOCK
    D = D_MODEL
    f32, bf16 = jnp.float32, jnp.bfloat16
    lane = lax.broadcasted_iota(jnp.int32, (1, LANES), 1)
    lo = lane < HEAD_DIM

    W_IN_FIRST = (K0, U0, U0 + CHUNK, Q0, Q0 + CHUNK)
    W_IN_GATES = (GA0, GA0 + CHUNK, GP0, GP0 + CHUNK)
    assert sorted(W_IN_FIRST + W_IN_GATES) == list(range(0, D_IN, CHUNK))
    n_in_chunks = D_IN // CHUNK
    n_out_chunks = D // LANES

    def win_copy(c0):
        cols = pl.ds(c0, CHUNK)
        return pltpu.make_async_copy(win_hbm.at[:, cols], win_f32.at[:, cols], w_sem.at[c0 // CHUNK])

    def wout_copy(kb):
        rows = pl.ds(kb * LANES, LANES)
        return pltpu.make_async_copy(wout_hbm.at[rows, :], wout_f32.at[rows, :], w_sem.at[n_in_chunks + kb])

    def convert_win(c0):
        win_copy(c0).wait()
        cs = slice(c0, c0 + CHUNK)
        for kb in range(D // LANES):
            rows = slice(kb * LANES, (kb + 1) * LANES)
            g = gcol_scr[rows, :]
            if c0 in W_IN_GATES:
                g = 0.5 * g
            win_scr[rows, cs] = (win_f32[rows, cs] * g).astype(bf16)

    def convert_wout():
        for kb in range(n_out_chunks):
            rows = slice(kb * LANES, (kb + 1) * LANES)
            wout_copy(kb).wait()
            wout_scr[rows, :] = wout_f32[rows, :].astype(bf16)

    @pl.when(i == 0)
    def _():
        for c0 in W_IN_FIRST + W_IN_GATES:
            win_copy(c0).start()

        TOEP = 4 * BLOCK
        rel_row = lax.broadcasted_iota(jnp.int32, (8, TOEP), 1) - (TOEP // 2 - 1)
        bucket_row = _t5_bucket_exact(rel_row)
        band_row = jnp.abs(rel_row) <= WINDOW
        kj = lax.broadcasted_iota(jnp.int32, (BLOCK, 3 * BLOCK), 1)
        for h in range(N_Q_HEADS):
            def body(b, acc, h=h):
                return jnp.where(bucket_row == b, relb_ref[h, b], acc)
            row = lax.fori_loop(0, N_BUCKETS, body, jnp.zeros((8, TOEP), f32)) * LOG2E
            row = jnp.where(band_row, row, NEG)[0:1, :]
            tbl = pltpu.roll(jnp.broadcast_to(row, (BLOCK, TOEP)), BLOCK + TOEP // 2 + 1, axis=1,
                             stride=1, stride_axis=0)[:, :3 * BLOCK]
            bias_scr[0, h] = tbl
            bias_scr[1, h] = jnp.where(kj >= BLOCK, tbl, NEG)
            bias_scr[2, h] = jnp.where(kj < 2 * BLOCK, tbl, NEG)
        k_scr[0, :, 0:BLOCK, :] = jnp.zeros((4, BLOCK, LANES), bf16)
        v_scr[0, :, 0:2 * BLOCK, :] = jnp.zeros((N_KV_HEADS, 2 * BLOCK, LANES), bf16)
        u_scr[0, 0:POOL_HALO, :] = jnp.zeros((POOL_HALO, D_POOL), f32)
        pw2_scr[...] = jnp.zeros(pw2_scr.shape, bf16)
        for gi in range(len(POOL_WINDOWS)):
            r0 = (gi % 2) * POOL_GROUP
            pw2_scr[gi // 2, r0:r0 + POOL_GROUP, r0:r0 + POOL_GROUP] = poolw_ref[gi].astype(bf16)
        row_lo = lax.broadcasted_iota(jnp.int32, (2 * BLOCK, LANES), 0) < BLOCK
        ones_pat = jnp.where(row_lo, jnp.where(lo, 1.0, 0.0), jnp.where(lo, 0.0, 1.0)).astype(bf16)
        for kb in range(3):
            ones_scr[kb * 2 * BLOCK:(kb + 1) * 2 * BLOCK, :] = ones_pat
        g_rows = jnp.broadcast_to(gpre_ref[...], (LANES, D))
        for kb in range(D // LANES):
            rows = slice(kb * LANES, (kb + 1) * LANES)
            g_col = jnp.transpose(g_rows[:, rows])
            gcol_scr[rows, :] = jnp.concatenate([g_col, g_col], axis=1)
        for c0 in W_IN_FIRST:
            convert_win(c0)

    b_is_first = i == 1
    b_is_last = i == n_tiles
    tile_b = jnp.clip(i - 1, 0, n_tiles - 1)

    def run_step(pa, pb, do_a=True, do_b=True, do_c=True, a_parts=(True, True)):
        do_a_first, do_a_gates = (do_a and a_parts[0]), (do_a and a_parts[1])
        enabled = {"a": do_a_first, "gates": do_a_gates, "b": do_b, "c": do_c}
        zero = jnp.zeros((), bf16)

        def proj(c0, width=CHUNK):
            return jnp.dot(xn_scr[...], win_scr[:, c0:c0 + width], preferred_element_type=f32)

        def a_kv():
            kv = proj(K0, 2 * D_KV)
            k2 = kv[:, :D_KV].astype(bf16)
            kr = pltpu.roll(kv[:, :D_KV], HEAD_DIM, axis=1).astype(bf16)
            kvars = (jnp.where(lo, k2, zero), jnp.where(lo, zero, kr),
                     jnp.where(lo, kr, zero), jnp.where(lo, zero, k2))
            for var, kk in enumerate(kvars):
                k_scr[pa, var, BLOCK:BLOCK + T, :] = kk
                k_scr[pb, var, BLOCK + T:, :] = kk[:BLOCK]
            v2 = kv[:, D_KV:].astype(bf16)
            vr = pltpu.roll(kv[:, D_KV:], HEAD_DIM, axis=1).astype(bf16)
            vvars = ((jnp.where(lo, v2, zero), jnp.where(lo, zero, vr)),
                     (jnp.where(lo, vr, zero), jnp.where(lo, zero, v2)))
            for j in range(N_KV_HEADS):
                for par in range(2):
                    vv = vvars[j][par]
                    for kb in range(R):
                        r0 = (kb + 1) * 2 * BLOCK + par * BLOCK
                        v_scr[pa, j, r0:r0 + BLOCK, :] = vv[kb * BLOCK:(kb + 1) * BLOCK]
                    r0 = (R + 1) * 2 * BLOCK + par * BLOCK
                    v_scr[pb, j, r0:r0 + BLOCK, :] = vv[:BLOCK]

        def a_q(c):
            cs = slice(c * CHUNK, (c + 1) * CHUNK)
            q_scr[pa, :, cs] = (proj(Q0 + cs.start) * (HEAD_DIM ** -0.5 * LOG2E)).astype(bf16)

        def a_gate(c0, dst, c):
            cs = slice(c * CHUNK, (c + 1) * CHUNK)
            dst[pa, :, cs] = _silu_of_half(proj(c0 + cs.start))

        def a_u(c):
            cs = slice(c * CHUNK, (c + 1) * CHUNK)
            u = proj(U0 + cs.start)
            u_scr[pa, POOL_HALO:POOL_HALO + T, cs] = u
            u_scr[pb, POOL_HALO + T:, cs] = jnp.where(b_is_last, 0.0, u[:POOL_HALO])

        units = [(b, j) for b in range(R) for j in range(N_KV_HEADS)]
        assert len(units) == 8, "the program order below is written for eight attention units per tile"

        def b_scores(n):
            b, j = units[n]
            rows = slice(b * BLOCK, (b + 1) * BLOCK)
            win = slice(b * BLOCK, b * BLOCK + 3 * BLOCK)
            c0 = j * Q_PER_KV * HEAD_DIM
            q2 = jnp.concatenate([q_scr[pb, rows, c0:c0 + LANES],
                                  q_scr[pb, rows, c0 + LANES:c0 + 2 * LANES]], axis=0)
            return [lax.dot_general(q2, k_scr[pb, 2 * j + par, win, :], _NT, preferred_element_type=f32)
                    for par in range(2)]

        def b_soft(n, s2s, par, pd, sink_term):
            b, j = units[n]
            if b == 0:
                tbl = jnp.where(b_is_first, 1, 0)
            elif b == R - 1:
                tbl = jnp.where(b_is_last, 2, 0)
            else:
                tbl = 0
            for half in range(2):
                h = j * Q_PER_KV + 2 * half + par
                s = s2s[par][half * BLOCK:(half + 1) * BLOCK, :] + bias_scr[tbl, h]
                sink = sink_ref[0, h] * LOG2E
                m = jnp.maximum(jnp.max(s, axis=-1, keepdims=True), sink)
                pd[(half, par)] = jnp.exp2(s - m).astype(bf16)
                sink_term[(half, par)] = jnp.exp2(sink - m)

        def b_pv(n, pd, sink_term):
            b, j = units[n]
            rows = slice(b * BLOCK, (b + 1) * BLOCK)
            c0 = j * Q_PER_KV * HEAD_DIM
            pcat = jnp.concatenate(
                [jnp.concatenate([pd[(half, par)][:, kb * BLOCK:(kb + 1) * BLOCK]
                                  for kb in range(3) for par in range(2)], axis=1)
                 for half in range(2)], axis=0)
            vwin = jnp.concatenate([v_scr[pb, j, 2 * b * BLOCK:2 * b * BLOCK + 6 * BLOCK, :], ones_scr[...]],
                                   axis=1)
            o2 = jnp.dot(pcat, vwin, preferred_element_type=f32)
            for half in range(2):
                hr = slice(half * BLOCK, (half + 1) * BLOCK)
                denom = o2[hr, LANES:] + jnp.where(lo, sink_term[(half, 0)], sink_term[(half, 1)])
                cs = c0 + half * LANES
                a = (o2[hr, :LANES] / denom) * sga_scr[pb, rows, cs:cs + LANES]
                mix_scr[pb, rows, cs:cs + LANES] = a.astype(bf16)

        def b_pool_y(gi):
            w = POOL_WINDOWS[gi]
            seq = n_tiles * T
            E = POOL_HALO
            left = w // 2
            right = w - 1 - left
            cols = slice(gi * POOL_GROUP, (gi + 1) * POOL_GROUP)
            ug = u_scr[pb, :, cols]
            wsum = _window_sums(ug, w)[E:E + T]
            uc = ug[E:E + T]
            y_mid = wsum * (1.0 / w) - uc
            parts = []
            for r0 in (0, T - E):
                t_glob = tile_b * T + r0 + lax.broadcasted_iota(jnp.int32, (E, LANES), 0)
                cnt = (jnp.minimum(t_glob + right + 1, seq) - jnp.maximum(t_glob - left, 0)).astype(f32)
                parts.append(wsum[r0:r0 + E] / cnt - uc[r0:r0 + E])
            return jnp.concatenate([parts[0], y_mid[E:T - E], parts[1]], axis=0).astype(bf16)

        def b_pool_pair(pr):
            cols = slice(pr * CHUNK, (pr + 1) * CHUNK)
            y2 = jnp.concatenate([b_pool_y(2 * pr), b_pool_y(2 * pr + 1)], axis=1)
            yw = jnp.dot(y2, pw2_scr[pr], preferred_element_type=f32)
            pz = (yw * pscale_ref[:, cols]) * sgp_scr[pb, :, cols]
            mix_scr[pb, :, D_ATTN + pr * CHUNK:D_ATTN + (pr + 1) * CHUNK] = pz.astype(bf16)

        half_cols = D // 2
        mixed_halves = []

        def c_out(ch):
            m = jnp.dot(mix_scr[pa], wout_scr[:, ch * half_cols:(ch + 1) * half_cols], preferred_element_type=f32)
            mixed_halves.append((m, jnp.sum(m * m, axis=-1, keepdims=True)))

        def c_post(rb):
            rows = slice(rb * BLOCK, (rb + 1) * BLOCK)
            mixed = jnp.concatenate([m[rows] for m, _ in mixed_halves], axis=1)
            ssq = mixed_halves[0][1][rows]
            for _, part_ssq in mixed_halves[1:]:
                ssq = ssq + part_ssq[rows]
            y = (mixed * lax.rsqrt(ssq * (1.0 / D) + EPS)) * gpost_ref[...]
            out_ref[rows, :] = xb_ref[rows, :] + y

        PN = BLOCK // 2

        def prenorm(r8):
            rows = slice(r8 * PN, (r8 + 1) * PN)
            xv = xa_ref[rows, :]
            ms = jnp.mean(xv * xv, axis=-1, keepdims=True)
            xn_scr[rows, :] = (xv * lax.rsqrt(ms + EPS)).astype(bf16)

        part = functools.partial
        out_q = lambda ch: ("c", part(c_out, ch))
        post = lambda rb: ("c", part(c_post, rb))
        ga = lambda c: ("gates", part(a_gate, GA0, sga_scr, c))
        gp = lambda c: ("gates", part(a_gate, GP0, sgp_scr, c))
        uu = lambda c: ("a", part(a_u, c))
        qq = lambda c: ("a", part(a_q, c))
        kv = ("a", a_kv)
        pn = lambda r8: ("a", part(prenorm, r8))
        heavy = [[out_q(0), None, None],
                 [out_q(1), None, None],
                 [kv, uu(0), None],
                 [uu(1), qq(0), None],
                 [qq(1), ga(0), None],
                 [ga(1), None, None],
                 [gp(0), None, None],
                 [gp(1), None, None]]
        assert len(heavy) == len(units)
        slot_end = {0: [pn(4), pn(5), pn(6), pn(7)],
                    1: [post(0), post(1)],
                    2: [post(2), post(3)],
                    3: [("b", part(b_pool_pair, 0))],
                    4: [("b", part(b_pool_pair, 1))]}

        def run(piece):
            if piece is not None and enabled[piece[0]]:
                piece[1]()

        if do_a_first:
            for r8 in range(T // PN // 2):
                prenorm(r8)
        elif do_b and not do_a:
            u_scr[pb, POOL_HALO + T:, :] = jnp.zeros((POOL_HALO, D_POOL), f32)
        s = {0: b_scores(0)} if do_b else {}
        for n in range(len(units)):
            if do_b and n + 1 < len(units):
                s[n + 1] = b_scores(n + 1)
            pd, sink_term = {}, {}
            run(heavy[n][0])
            if do_b:
                b_soft(n, s[n], 0, pd, sink_term)
            run(heavy[n][1])
            if do_b:
                b_soft(n, s.pop(n), 1, pd, sink_term)
            run(heavy[n][2])
            if do_b:
                b_pv(n, pd, sink_term)
            for piece in slot_end.get(n, ()):
                run(piece)

        if do_a_first:
            k_scr[pb, :, 0:BLOCK, :] = k_scr[pa, :, T:T + BLOCK, :]
            v_scr[pb, :, 0:2 * BLOCK, :] = v_scr[pa, :, R * 2 * BLOCK:(R + 1) * 2 * BLOCK, :]
            u_scr[pb, 0:POOL_HALO, :] = u_scr[pa, T:T + POOL_HALO, :]

    steady = (i >= 2) & (i < n_tiles)
    for pa in range(2):
        @pl.when(steady & (i % 2 == pa))
        def _(pa=pa):
            run_step(pa, 1 - pa)

    def ramp(step, before=None, **stages):
        @pl.when(i == step)
        def _():
            if before is not None:
                before()
            run_step(step % 2, 1 - step % 2, **stages)

    def convert_gates_start_wout():
        for c0 in W_IN_GATES:
            convert_win(c0)
        for kb in range(n_out_chunks):
            wout_copy(kb).start()

    ramp(0, do_b=False, do_c=False, a_parts=(True, False))
    ramp(0, before=convert_gates_start_wout, do_b=False, do_c=False, a_parts=(False, True))
    ramp(1, before=convert_wout, do_c=False)
    ramp(n_tiles, do_a=False)
    ramp(n_tiles + 1, do_a=False, do_b=False)


def kernel(x, pre_norm_g, w_in, rel_bias, attn_sink, pool_w, pool_scale, w_out, post_norm_g):
    B, S, D = x.shape
    assert B == 1 and D == D_MODEL and S % TILE == 0 and S // TILE >= 3
    assert pre_norm_g.shape == (1, D) and attn_sink.shape == (1, N_Q_HEADS), "single layer"
    assert rel_bias.shape == (N_BUCKETS, N_Q_HEADS) and rel_bias.dtype == attn_sink.dtype == jnp.float32
    T = TILE
    R = T // BLOCK
    n_tiles = S // T
    bf16 = jnp.bfloat16

    x2 = x.reshape(S, D)
    smem = pl.BlockSpec(memory_space=pltpu.SMEM)
    hbm = pl.BlockSpec(memory_space=pltpu.HBM)
    full = lambda shape: pl.BlockSpec(shape, lambda i: (0,) * len(shape))

    out = pl.pallas_call(
        functools.partial(_layer_kernel, n_tiles=n_tiles),
        grid=(n_tiles + 2,),
        in_specs=[
            smem, smem,
            pl.BlockSpec((T, D), lambda i: (jnp.minimum(i, n_tiles - 1), 0)),
            pl.BlockSpec((T, D), lambda i: (jnp.maximum(i - 2, 0), 0)),
            full((1, D)),
            hbm,
            full((len(POOL_WINDOWS), POOL_GROUP, POOL_GROUP)),
            full((1, D_POOL)),
            hbm,
            full((1, D)),
        ],
        out_specs=pl.BlockSpec((T, D), lambda i: (jnp.maximum(i - 2, 0), 0)),
        out_shape=jax.ShapeDtypeStruct((S, D), x.dtype),
        scratch_shapes=[
            pltpu.VMEM((3, N_Q_HEADS, BLOCK, 3 * BLOCK), jnp.float32),
            pltpu.VMEM((2, T, D_ATTN), bf16),
            pltpu.VMEM((2, 4, T + 2 * BLOCK, LANES), bf16),
            pltpu.VMEM((2, N_KV_HEADS, (R + 2) * 2 * BLOCK, LANES), bf16),
            pltpu.VMEM((2, T + 2 * POOL_HALO, D_POOL), jnp.float32),
            pltpu.VMEM((2, T, D_ATTN), jnp.float32),
            pltpu.VMEM((2, T, D_POOL), jnp.float32),
            pltpu.VMEM((2, T, D), bf16),
            pltpu.VMEM((T, D), bf16),
            pltpu.VMEM((len(POOL_WINDOWS) // 2, CHUNK, CHUNK), bf16),
            pltpu.VMEM((3 * 2 * BLOCK, LANES), bf16),
            pltpu.VMEM((D, D), bf16),
            pltpu.VMEM((D, D_IN), bf16),
            pltpu.VMEM((D, D_IN), jnp.float32),
            pltpu.VMEM((D, D), jnp.float32),
            pltpu.VMEM((D, CHUNK), jnp.float32),
            pltpu.SemaphoreType.DMA((D_IN // CHUNK + D // LANES,)),
        ],
        compiler_params=pltpu.CompilerParams(
            dimension_semantics=("arbitrary",),
            vmem_limit_bytes=VMEM_LIMIT_BYTES,
        ),
        name="hymba_layer_fused",
    )(
        rel_bias.T, attn_sink,
        x2, x2,
        pre_norm_g,
        w_in.reshape(D, D_IN),
        pool_w.reshape(len(POOL_WINDOWS), POOL_GROUP, POOL_GROUP),
        pool_scale,
        w_out.reshape(D, D),
        post_norm_g,
    )
    return out.reshape(B, S, D)
```

```python
import functools
import math

import jax
import jax.numpy as jnp
from jax import lax
from jax.experimental import pallas as pl
from jax.experimental.pallas import tpu as pltpu

D_MODEL = 1024
D_ATTN = 512
D_POOL = 512
HEAD_DIM = 64
N_Q_HEADS = 8
N_KV_HEADS = 2
Q_PER_KV = 4
D_KV = 128
WINDOW = 128
BLOCK = 128
N_BUCKETS = 32
POOL_WINDOWS = (2, 4, 8, 16)
POOL_GROUP = 128
D_IN = 2304
EPS = 1e-6
NEG = -1e30
LOG2E = math.log2(math.e)

LANES = 128
CHUNK = 2 * LANES
POOL_HALO = 16
TILE = 512
VMEM_LIMIT_BYTES = 62 * 1024 * 1024

Q0, K0, V0, GA0, U0, GP0 = 0, 512, 640, 768, 1280, 1792

_NT = (((1,), (1,)), ((), ()))


def _silu_of_half(h):
    return h + h * jnp.tanh(h)


def _t5_bucket_exact(rel):
    nb = N_BUCKETS // 2
    max_exact = nb // 2
    ret = jnp.where(rel > 0, nb, 0)
    n = jnp.abs(rel)
    n2 = n * n
    large = jnp.full(rel.shape, max_exact, jnp.int32)
    for j in range(1, nb - max_exact):
        large = large + jnp.where(n2 >= (max_exact * max_exact) << j, 1, 0)
    return ret + jnp.where(n < max_exact, n, large)


def _window_sums(ug, w):
    n = ug.shape[0]
    if w == 2:
        return ug + pltpu.roll(ug, 1, axis=0)
    acc = ug + pltpu.roll(ug, n - 1, axis=0)
    span = 2
    while span * 2 < w:
        acc = acc + pltpu.roll(acc, n - span, axis=0)
        span *= 2
    return acc + pltpu.roll(acc, span, axis=0)


def _layer_kernel(relb_ref, sink_ref,
                  xa_ref, xb_ref, gpre_ref, win_hbm, poolw_ref, pscale_ref, wout_hbm, gpost_ref,
                  out_ref,
                  bias_scr, q_scr, k_scr, v_scr, u_scr, sga_scr, sgp_scr, mix_scr, xn_scr, pw2_scr, ones_scr,
                  wout_scr, win_scr, win_f32, wout_f32, w_sem,
                  *, n_tiles):
    i = pl.program_id(0)
    T = TILE
    R = T // BLOCK
    D = D_MODEL
    f32, bf16 = jnp.float32, jnp.bfloat16
    lane = lax.broadcasted_iota(jnp.int32, (1, LANES), 1)
    lo = lane < HEAD_DIM

    n_wchunks = D // LANES

    def w_copy(hbm, dst, kb, sem0):
        rows = pl.ds(kb * LANES, LANES)
        return pltpu.make_async_copy(hbm.at[rows, :], dst.at[rows, :], w_sem.at[sem0 + kb])

    win_copy = functools.partial(w_copy, win_hbm, win_f32, sem0=0)
    wout_copy = functools.partial(w_copy, wout_hbm, wout_f32, sem0=n_wchunks)

    def convert_wout():
        for kb in range(n_wchunks):
            rows = slice(kb * LANES, (kb + 1) * LANES)
            wout_copy(kb=kb).wait()
            wout_scr[rows, :] = wout_f32[rows, :].astype(bf16)

    W_IN_FLIGHT = 4

    @pl.when(i == 0)
    def _():
        for kb in range(W_IN_FLIGHT):
            win_copy(kb=kb).start()

        TOEP = 4 * BLOCK
        rel_row = lax.broadcasted_iota(jnp.int32, (8, TOEP), 1) - (TOEP // 2 - 1)
        bucket_row = _t5_bucket_exact(rel_row)
        band_row = jnp.abs(rel_row) <= WINDOW
        kj = lax.broadcasted_iota(jnp.int32, (BLOCK, 3 * BLOCK), 1)
        for h in range(N_Q_HEADS):
            def body(b, acc, h=h):
                return jnp.where(bucket_row == b, relb_ref[h, b], acc)
            row = lax.fori_loop(0, N_BUCKETS, body, jnp.zeros((8, TOEP), f32)) * LOG2E
            row = jnp.where(band_row, row, NEG)[0:1, :]
            tbl = pltpu.roll(jnp.broadcast_to(row, (BLOCK, TOEP)), BLOCK + TOEP // 2 + 1, axis=1,
                             stride=1, stride_axis=0)[:, :3 * BLOCK]
            bias_scr[0, h] = tbl
            bias_scr[1, h] = jnp.where(kj >= BLOCK, tbl, NEG)
            bias_scr[2, h] = jnp.where(kj < 2 * BLOCK, tbl, NEG)
        k_scr[0, :, 0:BLOCK, :] = jnp.zeros((4, BLOCK, LANES), bf16)
        v_scr[0, :, 0:2 * BLOCK, :] = jnp.zeros((N_KV_HEADS, 2 * BLOCK, LANES), bf16)
        u_scr[0, 0:POOL_HALO, :] = jnp.zeros((POOL_HALO, D_POOL), f32)
        pw2_scr[...] = jnp.zeros(pw2_scr.shape, bf16)
        for gi in range(len(POOL_WINDOWS)):
            r0 = (gi % 2) * POOL_GROUP
            pw2_scr[gi // 2, r0:r0 + POOL_GROUP, r0:r0 + POOL_GROUP] = poolw_ref[gi].astype(bf16)
        row_lo = lax.broadcasted_iota(jnp.int32, (2 * BLOCK, LANES), 0) < BLOCK
        ones_pat = jnp.where(row_lo, jnp.where(lo, 1.0, 0.0), jnp.where(lo, 0.0, 1.0)).astype(bf16)
        for kb in range(3):
            ones_scr[kb * 2 * BLOCK:(kb + 1) * 2 * BLOCK, :] = ones_pat
        g_rows = jnp.broadcast_to(gpre_ref[...], (LANES, D))
        for kb in range(D // LANES):
            rows = slice(kb * LANES, (kb + 1) * LANES)
            g_col = jnp.transpose(g_rows[:, rows])
            g_col = jnp.concatenate([g_col, g_col], axis=1)
            g_half = 0.5 * g_col
            win_copy(kb=kb).wait()
            if kb + W_IN_FLIGHT < n_wchunks:
                win_copy(kb=kb + W_IN_FLIGHT).start()
            for c in range(D_IN // CHUNK):
                cs = slice(c * CHUNK, (c + 1) * CHUNK)
                is_gate = GA0 <= cs.start < U0 or cs.start >= GP0
                win_scr[rows, cs] = (win_f32[rows, cs] * (g_half if is_gate else g_col)).astype(bf16)
        for kb in range(n_wchunks):
            wout_copy(kb=kb).start()

    b_is_first = i == 1
    b_is_last = i == n_tiles
    tile_b = jnp.clip(i - 1, 0, n_tiles - 1)

    def run_step(pa, pb, do_a=True, do_b=True, do_c=True):
        enabled = {"a": do_a, "b": do_b, "c": do_c}
        zero = jnp.zeros((), bf16)

        def proj(c0, width=CHUNK):
            return jnp.dot(xn_scr[...], win_scr[:, c0:c0 + width], preferred_element_type=f32)

        def a_kv():
            kv = proj(K0, 2 * D_KV)
            k2 = kv[:, :D_KV].astype(bf16)
            kr = pltpu.roll(kv[:, :D_KV], HEAD_DIM, axis=1).astype(bf16)
            kvars = (jnp.where(lo, k2, zero), jnp.where(lo, zero, kr),
                     jnp.where(lo, kr, zero), jnp.where(lo, zero, k2))
            for var, kk in enumerate(kvars):
                k_scr[pa, var, BLOCK:BLOCK + T, :] = kk
                k_scr[pb, var, BLOCK + T:, :] = kk[:BLOCK]
            v2 = kv[:, D_KV:].astype(bf16)
            vr = pltpu.roll(kv[:, D_KV:], HEAD_DIM, axis=1).astype(bf16)
            vvars = ((jnp.where(lo, v2, zero), jnp.where(lo, zero, vr)),
                     (jnp.where(lo, vr, zero), jnp.where(lo, zero, v2)))
            for j in range(N_KV_HEADS):
                for par in range(2):
                    vv = vvars[j][par]
                    for kb in range(R):
                        r0 = (kb + 1) * 2 * BLOCK + par * BLOCK
                        v_scr[pa, j, r0:r0 + BLOCK, :] = vv[kb * BLOCK:(kb + 1) * BLOCK]
                    r0 = (R + 1) * 2 * BLOCK + par * BLOCK
                    v_scr[pb, j, r0:r0 + BLOCK, :] = vv[:BLOCK]

        def a_q(c):
            cs = slice(c * CHUNK, (c + 1) * CHUNK)
            q_scr[pa, :, cs] = (proj(Q0 + cs.start) * (HEAD_DIM ** -0.5 * LOG2E)).astype(bf16)

        def a_gate(c0, dst, c):
            cs = slice(c * CHUNK, (c + 1) * CHUNK)
            dst[pa, :, cs] = _silu_of_half(proj(c0 + cs.start))

        def a_u(c):
            cs = slice(c * CHUNK, (c + 1) * CHUNK)
            u = proj(U0 + cs.start)
            u_scr[pa, POOL_HALO:POOL_HALO + T, cs] = u
            u_scr[pb, POOL_HALO + T:, cs] = jnp.where(b_is_last, 0.0, u[:POOL_HALO])

        units = [(b, j) for b in range(R) for j in range(N_KV_HEADS)]
        assert len(units) == 8, "the program order below is written for eight attention units per tile"

        def b_scores(n):
            b, j = units[n]
            rows = slice(b * BLOCK, (b + 1) * BLOCK)
            win = slice(b * BLOCK, b * BLOCK + 3 * BLOCK)
            c0 = j * Q_PER_KV * HEAD_DIM
            q2 = jnp.concatenate([q_scr[pb, rows, c0:c0 + LANES],
                                  q_scr[pb, rows, c0 + LANES:c0 + 2 * LANES]], axis=0)
            return [lax.dot_general(q2, k_scr[pb, 2 * j + par, win, :], _NT, preferred_element_type=f32)
                    for par in range(2)]

        def b_soft(n, s2s, par, pd, sink_term):
            b, j = units[n]
            if b == 0:
                tbl = jnp.where(b_is_first, 1, 0)
            elif b == R - 1:
                tbl = jnp.where(b_is_last, 2, 0)
            else:
                tbl = 0
            for half in range(2):
                h = j * Q_PER_KV + 2 * half + par
                s = s2s[par][half * BLOCK:(half + 1) * BLOCK, :] + bias_scr[tbl, h]
                sink = sink_ref[0, h] * LOG2E
                m = jnp.maximum(jnp.max(s, axis=-1, keepdims=True), sink)
                pd[(half, par)] = jnp.exp2(s - m).astype(bf16)
                sink_term[(half, par)] = jnp.exp2(sink - m)

        def b_pv(n, pd, sink_term):
            b, j = units[n]
            rows = slice(b * BLOCK, (b + 1) * BLOCK)
            c0 = j * Q_PER_KV * HEAD_DIM
            pcat = jnp.concatenate(
                [jnp.concatenate([pd[(half, par)][:, kb * BLOCK:(kb + 1) * BLOCK]
                                  for kb in range(3) for par in range(2)], axis=1)
                 for half in range(2)], axis=0)
            vwin = jnp.concatenate([v_scr[pb, j, 2 * b * BLOCK:2 * b * BLOCK + 6 * BLOCK, :], ones_scr[...]],
                                   axis=1)
            o2 = jnp.dot(pcat, vwin, preferred_element_type=f32)
            for half in range(2):
                hr = slice(half * BLOCK, (half + 1) * BLOCK)
                denom = o2[hr, LANES:] + jnp.where(lo, sink_term[(half, 0)], sink_term[(half, 1)])
                cs = c0 + half * LANES
                a = (o2[hr, :LANES] / denom) * sga_scr[pb, rows, cs:cs + LANES]
                mix_scr[pb, rows, cs:cs + LANES] = a.astype(bf16)

        def b_pool_y(gi):
            w = POOL_WINDOWS[gi]
            seq = n_tiles * T
            E = POOL_HALO
            left = w // 2
            right = w - 1 - left
            cols = slice(gi * POOL_GROUP, (gi + 1) * POOL_GROUP)
            ug = u_scr[pb, :, cols]
            wsum = _window_sums(ug, w)[E:E + T]
            uc = ug[E:E + T]
            y_mid = wsum * (1.0 / w) - uc
            parts = []
            for r0 in (0, T - E):
                t_glob = tile_b * T + r0 + lax.broadcasted_iota(jnp.int32, (E, LANES), 0)
                cnt = (jnp.minimum(t_glob + right + 1, seq) - jnp.maximum(t_glob - left, 0)).astype(f32)
                parts.append(wsum[r0:r0 + E] / cnt - uc[r0:r0 + E])
            return jnp.concatenate([parts[0], y_mid[E:T - E], parts[1]], axis=0).astype(bf16)

        def b_pool_pair(pr):
            cols = slice(pr * CHUNK, (pr + 1) * CHUNK)
            y2 = jnp.concatenate([b_pool_y(2 * pr), b_pool_y(2 * pr + 1)], axis=1)
            yw = jnp.dot(y2, pw2_scr[pr], preferred_element_type=f32)
            pz = (yw * pscale_ref[:, cols]) * sgp_scr[pb, :, cols]
            mix_scr[pb, :, D_ATTN + pr * CHUNK:D_ATTN + (pr + 1) * CHUNK] = pz.astype(bf16)

        half_cols = D // 2
        mixed_halves = []

        def c_out(ch):
            m = jnp.dot(mix_scr[pa], wout_scr[:, ch * half_cols:(ch + 1) * half_cols], preferred_element_type=f32)
            mixed_halves.append((m, jnp.sum(m * m, axis=-1, keepdims=True)))

        def c_post(rb):
            rows = slice(rb * BLOCK, (rb + 1) * BLOCK)
            mixed = jnp.concatenate([m[rows] for m, _ in mixed_halves], axis=1)
            ssq = mixed_halves[0][1][rows]
            for _, part_ssq in mixed_halves[1:]:
                ssq = ssq + part_ssq[rows]
            y = (mixed * lax.rsqrt(ssq * (1.0 / D) + EPS)) * gpost_ref[...]
            out_ref[rows, :] = xb_ref[rows, :] + y

        PN = BLOCK // 2

        def prenorm(r8):
            rows = slice(r8 * PN, (r8 + 1) * PN)
            xv = xa_ref[rows, :]
            ms = jnp.mean(xv * xv, axis=-1, keepdims=True)
            xn_scr[rows, :] = (xv * lax.rsqrt(ms + EPS)).astype(bf16)

        part = functools.partial
        out_q = lambda ch: ("c", part(c_out, ch))
        post = lambda rb: ("c", part(c_post, rb))
        ga = lambda c: ("a", part(a_gate, GA0, sga_scr, c))
        gp = lambda c: ("a", part(a_gate, GP0, sgp_scr, c))
        uu = lambda c: ("a", part(a_u, c))
        qq = lambda c: ("a", part(a_q, c))
        kv = ("a", a_kv)
        pn = lambda r8: ("a", part(prenorm, r8))
        heavy = [[out_q(0), None, None],
                 [out_q(1), None, None],
                 [kv, uu(0), None],
                 [uu(1), qq(0), None],
                 [qq(1), ga(0), None],
                 [ga(1), None, None],
                 [gp(0), None, None],
                 [gp(1), None, None]]
        assert len(heavy) == len(units)
        slot_end = {0: [pn(4), pn(5), pn(6), pn(7)],
                    1: [post(0), post(1)],
                    2: [post(2), post(3)],
                    3: [("b", part(b_pool_pair, 0))],
                    4: [("b", part(b_pool_pair, 1))]}

        def run(piece):
            if piece is not None and enabled[piece[0]]:
                piece[1]()

        if do_a:
            for r8 in range(T // PN // 2):
                prenorm(r8)
        elif do_b:
            u_scr[pb, POOL_HALO + T:, :] = jnp.zeros((POOL_HALO, D_POOL), f32)
        s = {0: b_scores(0)} if do_b else {}
        for n in range(len(units)):
            if do_b and n + 1 < len(units):
                s[n + 1] = b_scores(n + 1)
            pd, sink_term = {}, {}
            run(heavy[n][0])
            if do_b:
                b_soft(n, s[n], 0, pd, sink_term)
            run(heavy[n][1])
            if do_b:
                b_soft(n, s.pop(n), 1, pd, sink_term)
            run(heavy[n][2])
            if do_b:
                b_pv(n, pd, sink_term)
            for piece in slot_end.get(n, ()):
                run(piece)

        if do_a:
            k_scr[pb, :, 0:BLOCK, :] = k_scr[pa, :, T:T + BLOCK, :]
            v_scr[pb, :, 0:2 * BLOCK, :] = v_scr[pa, :, R * 2 * BLOCK:(R + 1) * 2 * BLOCK, :]
            u_scr[pb, 0:POOL_HALO, :] = u_scr[pa, T:T + POOL_HALO, :]

    steady = (i >= 2) & (i < n_tiles)
    for pa in range(2):
        @pl.when(steady & (i % 2 == pa))
        def _(pa=pa):
            run_step(pa, 1 - pa)

    def ramp(step, before=None, **stages):
        @pl.when(i == step)
        def _():
            if before is not None:
                before()
            run_step(step % 2, 1 - step % 2, **stages)

    ramp(0, do_b=False, do_c=False)
    ramp(1, before=convert_wout, do_c=False)
    ramp(n_tiles, do_a=False)
    ramp(n_tiles + 1, do_a=False, do_b=False)


def kernel(x, pre_norm_g, w_in, rel_bias, attn_sink, pool_w, pool_scale, w_out, post_norm_g):
    B, S, D = x.shape
    assert B == 1 and D == D_MODEL and S % TILE == 0 and S // TILE >= 3
    assert pre_norm_g.shape == (1, D) and attn_sink.shape == (1, N_Q_HEADS), "single layer"
    assert rel_bias.shape == (N_BUCKETS, N_Q_HEADS) and rel_bias.dtype == attn_sink.dtype == jnp.float32
    T = TILE
    R = T // BLOCK
    n_tiles = S // T
    bf16 = jnp.bfloat16

    x2 = x.reshape(S, D)
    smem = pl.BlockSpec(memory_space=pltpu.SMEM)
    hbm = pl.BlockSpec(memory_space=pltpu.HBM)
    full = lambda shape: pl.BlockSpec(shape, lambda i: (0,) * len(shape))

    out = pl.pallas_call(
        functools.partial(_layer_kernel, n_tiles=n_tiles),
        grid=(n_tiles + 2,),
        in_specs=[
            smem, smem,
            pl.BlockSpec((T, D), lambda i: (jnp.minimum(i, n_tiles - 1), 0)),
            pl.BlockSpec((T, D), lambda i: (jnp.maximum(i - 2, 0), 0)),
            full((1, D)),
            hbm,
            full((len(POOL_WINDOWS), POOL_GROUP, POOL_GROUP)),
            full((1, D_POOL)),
            hbm,
            full((1, D)),
        ],
        out_specs=pl.BlockSpec((T, D), lambda i: (jnp.maximum(i - 2, 0), 0)),
        out_shape=jax.ShapeDtypeStruct((S, D), x.dtype),
        scratch_shapes=[
            pltpu.VMEM((3, N_Q_HEADS, BLOCK, 3 * BLOCK), jnp.float32),
            pltpu.VMEM((2, T, D_ATTN), bf16),
            pltpu.VMEM((2, 4, T + 2 * BLOCK, LANES), bf16),
            pltpu.VMEM((2, N_KV_HEADS, (R + 2) * 2 * BLOCK, LANES), bf16),
            pltpu.VMEM((2, T + 2 * POOL_HALO, D_POOL), jnp.float32),
            pltpu.VMEM((2, T, D_ATTN), jnp.float32),
            pltpu.VMEM((2, T, D_POOL), jnp.float32),
            pltpu.VMEM((2, T, D), bf16),
            pltpu.VMEM((T, D), bf16),
            pltpu.VMEM((len(POOL_WINDOWS) // 2, CHUNK, CHUNK), bf16),
            pltpu.VMEM((3 * 2 * BLOCK, LANES), bf16),
            pltpu.VMEM((D, D), bf16),
            pltpu.VMEM((D, D_IN), bf16),
            pltpu.VMEM((D, D_IN), jnp.float32),
            pltpu.VMEM((D, D), jnp.float32),
            pltpu.SemaphoreType.DMA((2 * (D // LANES),)),
        ],
        compiler_params=pltpu.CompilerParams(
            dimension_semantics=("arbitrary",),
            vmem_limit_bytes=VMEM_LIMIT_BYTES,
        ),
        name="hymba_layer_fused",
    )(
        rel_bias.T, attn_sink,
        x2, x2,
        pre_norm_g,
        w_in.reshape(D, D_IN),
        pool_w.reshape(len(POOL_WINDOWS), POOL_GROUP, POOL_GROUP),
        pool_scale,
        w_out.reshape(D, D),
        post_norm_g,
    )
    return out.reshape(B, S, D)
```

```python
import functools
import math

import jax
import jax.numpy as jnp
from jax import lax
from jax.experimental import pallas as pl
from jax.experimental.pallas import tpu as pltpu

D_MODEL = 1024
D_ATTN = 512
D_POOL = 512
HEAD_DIM = 64
N_Q_HEADS = 8
N_KV_HEADS = 2
Q_PER_KV = 4
D_KV = 128
WINDOW = 128
BLOCK = 128
N_BUCKETS = 32
POOL_WINDOWS = (2, 4, 8, 16)
POOL_GROUP = 128
D_IN = 2304
EPS = 1e-6
NEG = -1e30
LOG2E = math.log2(math.e)

LANES = 128
CHUNK = 2 * LANES
POOL_HALO = 16
TILE = 512
W_COPIES_PER_CHUNK = 4
VMEM_LIMIT_BYTES = 62 * 1024 * 1024

Q0, K0, V0, GA0, U0, GP0 = 0, 512, 640, 768, 1280, 1792

_NT = (((1,), (1,)), ((), ()))


def _silu_of_half(h):
    return h + h * jnp.tanh(h)


def _t5_bucket_exact(rel):
    nb = N_BUCKETS // 2
    max_exact = nb // 2
    ret = jnp.where(rel > 0, nb, 0)
    n = jnp.abs(rel)
    n2 = n * n
    large = jnp.full(rel.shape, max_exact, jnp.int32)
    for j in range(1, nb - max_exact):
        large = large + jnp.where(n2 >= (max_exact * max_exact) << j, 1, 0)
    return ret + jnp.where(n < max_exact, n, large)


def _window_sums(ug, w):
    n = ug.shape[0]
    if w == 2:
        return ug + pltpu.roll(ug, 1, axis=0)
    acc = ug + pltpu.roll(ug, n - 1, axis=0)
    span = 2
    while span * 2 < w:
        acc = acc + pltpu.roll(acc, n - span, axis=0)
        span *= 2
    return acc + pltpu.roll(acc, span, axis=0)


def _layer_kernel(relb_ref, sink_ref,
                  xa_ref, xb_ref, gpre_ref, win_hbm, poolw_ref, pscale_ref, wout_hbm, gpost_ref,
                  out_ref,
                  bias_scr, q_scr, k_scr, v_scr, u_scr, sga_scr, sgp_scr, mix_scr, xn_scr, pw2_scr, ones_scr,
                  wout_scr, win_scr, win_f32, wout_f32, w_sem,
                  *, n_tiles):
    i = pl.program_id(0)
    T = TILE
    R = T // BLOCK
    D = D_MODEL
    f32, bf16 = jnp.float32, jnp.bfloat16
    lane = lax.broadcasted_iota(jnp.int32, (1, LANES), 1)
    lo = lane < HEAD_DIM

    n_wchunks = D // LANES
    W_SPLIT = W_COPIES_PER_CHUNK
    sub_rows = LANES // W_SPLIT

    def w_copies(hbm, dst, kb, sem0):
        copies = []
        for j in range(W_SPLIT):
            rows = pl.ds(kb * LANES + j * sub_rows, sub_rows)
            copies.append(pltpu.make_async_copy(hbm.at[rows, :], dst.at[rows, :],
                                                w_sem.at[sem0 + kb * W_SPLIT + j]))
        return copies

    win_copy = functools.partial(w_copies, win_hbm, win_f32, sem0=0)
    wout_copy = functools.partial(w_copies, wout_hbm, wout_f32, sem0=n_wchunks * W_SPLIT)

    def convert_wout():
        for kb in range(n_wchunks):
            rows = slice(kb * LANES, (kb + 1) * LANES)
            for cp in wout_copy(kb=kb):
                cp.wait()
            wout_scr[rows, :] = wout_f32[rows, :].astype(bf16)

    @pl.when(i == 0)
    def _():
        for kb in range(n_wchunks):
            for cp in win_copy(kb=kb):
                cp.start()

        TOEP = 4 * BLOCK
        rel_row = lax.broadcasted_iota(jnp.int32, (8, TOEP), 1) - (TOEP // 2 - 1)
        bucket_row = _t5_bucket_exact(rel_row)
        band_row = jnp.abs(rel_row) <= WINDOW
        kj = lax.broadcasted_iota(jnp.int32, (BLOCK, 3 * BLOCK), 1)
        for h in range(N_Q_HEADS):
            def body(b, acc, h=h):
                return jnp.where(bucket_row == b, relb_ref[h, b], acc)
            row = lax.fori_loop(0, N_BUCKETS, body, jnp.zeros((8, TOEP), f32)) * LOG2E
            row = jnp.where(band_row, row, NEG)[0:1, :]
            tbl = pltpu.roll(jnp.broadcast_to(row, (BLOCK, TOEP)), BLOCK + TOEP // 2 + 1, axis=1,
                             stride=1, stride_axis=0)[:, :3 * BLOCK]
            bias_scr[0, h] = tbl
            bias_scr[1, h] = jnp.where(kj >= BLOCK, tbl, NEG)
            bias_scr[2, h] = jnp.where(kj < 2 * BLOCK, tbl, NEG)
        k_scr[0, :, 0:BLOCK, :] = jnp.zeros((4, BLOCK, LANES), bf16)
        v_scr[0, :, 0:2 * BLOCK, :] = jnp.zeros((N_KV_HEADS, 2 * BLOCK, LANES), bf16)
        u_scr[0, 0:POOL_HALO, :] = jnp.zeros((POOL_HALO, D_POOL), f32)
        pw2_scr[...] = jnp.zeros(pw2_scr.shape, bf16)
        for gi in range(len(POOL_WINDOWS)):
            r0 = (gi % 2) * POOL_GROUP
            pw2_scr[gi // 2, r0:r0 + POOL_GROUP, r0:r0 + POOL_GROUP] = poolw_ref[gi].astype(bf16)
        row_lo = lax.broadcasted_iota(jnp.int32, (2 * BLOCK, LANES), 0) < BLOCK
        ones_pat = jnp.where(row_lo, jnp.where(lo, 1.0, 0.0), jnp.where(lo, 0.0, 1.0)).astype(bf16)
        for kb in range(3):
            ones_scr[kb * 2 * BLOCK:(kb + 1) * 2 * BLOCK, :] = ones_pat
        g_rows = jnp.broadcast_to(gpre_ref[...], (LANES, D))
        for kb in range(D // LANES):
            rows = slice(kb * LANES, (kb + 1) * LANES)
            g_col = jnp.transpose(g_rows[:, rows])
            g_col = jnp.concatenate([g_col, g_col], axis=1)
            g_half = 0.5 * g_col
            for cp in win_copy(kb=kb):
                cp.wait()
            for c in range(D_IN // CHUNK):
                cs = slice(c * CHUNK, (c + 1) * CHUNK)
                is_gate = GA0 <= cs.start < U0 or cs.start >= GP0
                win_scr[rows, cs] = (win_f32[rows, cs] * (g_half if is_gate else g_col)).astype(bf16)
        for kb in range(n_wchunks):
            for cp in wout_copy(kb=kb):
                cp.start()

    b_is_first = i == 1
    b_is_last = i == n_tiles
    tile_b = jnp.clip(i - 1, 0, n_tiles - 1)

    def run_step(pa, pb, do_a=True, do_b=True, do_c=True):
        enabled = {"a": do_a, "b": do_b, "c": do_c}
        zero = jnp.zeros((), bf16)

        def proj(c0, width=CHUNK):
            return jnp.dot(xn_scr[...], win_scr[:, c0:c0 + width], preferred_element_type=f32)

        def a_kv():
            kv = proj(K0, 2 * D_KV)
            k2 = kv[:, :D_KV].astype(bf16)
            kr = pltpu.roll(kv[:, :D_KV], HEAD_DIM, axis=1).astype(bf16)
            kvars = (jnp.where(lo, k2, zero), jnp.where(lo, zero, kr),
                     jnp.where(lo, kr, zero), jnp.where(lo, zero, k2))
            for var, kk in enumerate(kvars):
                k_scr[pa, var, BLOCK:BLOCK + T, :] = kk
                k_scr[pb, var, BLOCK + T:, :] = kk[:BLOCK]
            v2 = kv[:, D_KV:].astype(bf16)
            vr = pltpu.roll(kv[:, D_KV:], HEAD_DIM, axis=1).astype(bf16)
            vvars = ((jnp.where(lo, v2, zero), jnp.where(lo, zero, vr)),
                     (jnp.where(lo, vr, zero), jnp.where(lo, zero, v2)))
            for j in range(N_KV_HEADS):
                for par in range(2):
                    vv = vvars[j][par]
                    for kb in range(R):
                        r0 = (kb + 1) * 2 * BLOCK + par * BLOCK
                        v_scr[pa, j, r0:r0 + BLOCK, :] = vv[kb * BLOCK:(kb + 1) * BLOCK]
                    r0 = (R + 1) * 2 * BLOCK + par * BLOCK
                    v_scr[pb, j, r0:r0 + BLOCK, :] = vv[:BLOCK]

        def a_q(c):
            cs = slice(c * CHUNK, (c + 1) * CHUNK)
            q_scr[pa, :, cs] = (proj(Q0 + cs.start) * (HEAD_DIM ** -0.5 * LOG2E)).astype(bf16)

        def a_gate(c0, dst, c):
            cs = slice(c * CHUNK, (c + 1) * CHUNK)
            dst[pa, :, cs] = _silu_of_half(proj(c0 + cs.start))

        def a_u(c):
            cs = slice(c * CHUNK, (c + 1) * CHUNK)
            u = proj(U0 + cs.start)
            u_scr[pa, POOL_HALO:POOL_HALO + T, cs] = u
            u_scr[pb, POOL_HALO + T:, cs] = jnp.where(b_is_last, 0.0, u[:POOL_HALO])

        units = [(b, j) for b in range(R) for j in range(N_KV_HEADS)]
        assert len(units) == 8, "the program order below is written for eight attention units per tile"

        def b_scores(n):
            b, j = units[n]
            rows = slice(b * BLOCK, (b + 1) * BLOCK)
            win = slice(b * BLOCK, b * BLOCK + 3 * BLOCK)
            c0 = j * Q_PER_KV * HEAD_DIM
            q2 = jnp.concatenate([q_scr[pb, rows, c0:c0 + LANES],
                                  q_scr[pb, rows, c0 + LANES:c0 + 2 * LANES]], axis=0)
            return [lax.dot_general(q2, k_scr[pb, 2 * j + par, win, :], _NT, preferred_element_type=f32)
                    for par in range(2)]

        def b_soft(n, s2s, par, pd, sink_term):
            b, j = units[n]
            if b == 0:
                tbl = jnp.where(b_is_first, 1, 0)
            elif b == R - 1:
                tbl = jnp.where(b_is_last, 2, 0)
            else:
                tbl = 0
            for half in range(2):
                h = j * Q_PER_KV + 2 * half + par
                s = s2s[par][half * BLOCK:(half + 1) * BLOCK, :] + bias_scr[tbl, h]
                sink = sink_ref[0, h] * LOG2E
                m = jnp.maximum(jnp.max(s, axis=-1, keepdims=True), sink)
                pd[(half, par)] = jnp.exp2(s - m).astype(bf16)
                sink_term[(half, par)] = jnp.exp2(sink - m)

        def b_pv(n, pd, sink_term):
            b, j = units[n]
            rows = slice(b * BLOCK, (b + 1) * BLOCK)
            c0 = j * Q_PER_KV * HEAD_DIM
            pcat = jnp.concatenate(
                [jnp.concatenate([pd[(half, par)][:, kb * BLOCK:(kb + 1) * BLOCK]
                                  for kb in range(3) for par in range(2)], axis=1)
                 for half in range(2)], axis=0)
            vwin = jnp.concatenate([v_scr[pb, j, 2 * b * BLOCK:2 * b * BLOCK + 6 * BLOCK, :], ones_scr[...]],
                                   axis=1)
            o2 = jnp.dot(pcat, vwin, preferred_element_type=f32)
            for half in range(2):
                hr = slice(half * BLOCK, (half + 1) * BLOCK)
                denom = o2[hr, LANES:] + jnp.where(lo, sink_term[(half, 0)], sink_term[(half, 1)])
                cs = c0 + half * LANES
                a = (o2[hr, :LANES] / denom) * sga_scr[pb, rows, cs:cs + LANES]
                mix_scr[pb, rows, cs:cs + LANES] = a.astype(bf16)

        def b_pool_y(gi):
            w = POOL_WINDOWS[gi]
            seq = n_tiles * T
            E = POOL_HALO
            left = w // 2
            right = w - 1 - left
            cols = slice(gi * POOL_GROUP, (gi + 1) * POOL_GROUP)
            ug = u_scr[pb, :, cols]
            wsum = _window_sums(ug, w)[E:E + T]
            uc = ug[E:E + T]
            y_mid = wsum * (1.0 / w) - uc
            parts = []
            for r0 in (0, T - E):
                t_glob = tile_b * T + r0 + lax.broadcasted_iota(jnp.int32, (E, LANES), 0)
                cnt = (jnp.minimum(t_glob + right + 1, seq) - jnp.maximum(t_glob - left, 0)).astype(f32)
                parts.append(wsum[r0:r0 + E] / cnt - uc[r0:r0 + E])
            return jnp.concatenate([parts[0], y_mid[E:T - E], parts[1]], axis=0).astype(bf16)

        def b_pool_pair(pr):
            cols = slice(pr * CHUNK, (pr + 1) * CHUNK)
            y2 = jnp.concatenate([b_pool_y(2 * pr), b_pool_y(2 * pr + 1)], axis=1)
            yw = jnp.dot(y2, pw2_scr[pr], preferred_element_type=f32)
            pz = (yw * pscale_ref[:, cols]) * sgp_scr[pb, :, cols]
            mix_scr[pb, :, D_ATTN + pr * CHUNK:D_ATTN + (pr + 1) * CHUNK] = pz.astype(bf16)

        half_cols = D // 2
        mixed_halves = []

        def c_out(ch):
            m = jnp.dot(mix_scr[pa], wout_scr[:, ch * half_cols:(ch + 1) * half_cols], preferred_element_type=f32)
            mixed_halves.append((m, jnp.sum(m * m, axis=-1, keepdims=True)))

        def c_post(rb):
            rows = slice(rb * BLOCK, (rb + 1) * BLOCK)
            mixed = jnp.concatenate([m[rows] for m, _ in mixed_halves], axis=1)
            ssq = mixed_halves[0][1][rows]
            for _, part_ssq in mixed_halves[1:]:
                ssq = ssq + part_ssq[rows]
            y = (mixed * lax.rsqrt(ssq * (1.0 / D) + EPS)) * gpost_ref[...]
            out_ref[rows, :] = xb_ref[rows, :] + y

        PN = BLOCK // 2

        def prenorm(r8):
            rows = slice(r8 * PN, (r8 + 1) * PN)
            xv = xa_ref[rows, :]
            ms = jnp.mean(xv * xv, axis=-1, keepdims=True)
            xn_scr[rows, :] = (xv * lax.rsqrt(ms + EPS)).astype(bf16)

        part = functools.partial
        out_q = lambda ch: ("c", part(c_out, ch))
        post = lambda rb: ("c", part(c_post, rb))
        ga = lambda c: ("a", part(a_gate, GA0, sga_scr, c))
        gp = lambda c: ("a", part(a_gate, GP0, sgp_scr, c))
        uu = lambda c: ("a", part(a_u, c))
        qq = lambda c: ("a", part(a_q, c))
        kv = ("a", a_kv)
        pn = lambda r8: ("a", part(prenorm, r8))
        heavy = [[out_q(0), None, None],
                 [out_q(1), None, None],
                 [kv, uu(0), None],
                 [uu(1), qq(0), None],
                 [qq(1), ga(0), None],
                 [ga(1), None, None],
                 [gp(0), None, None],
                 [gp(1), None, None]]
        assert len(heavy) == len(units)
        slot_end = {0: [pn(4), pn(5), pn(6), pn(7)],
                    1: [post(0), post(1)],
                    2: [post(2), post(3)],
                    3: [("b", part(b_pool_pair, 0))],
                    4: [("b", part(b_pool_pair, 1))]}

        def run(piece):
            if piece is not None and enabled[piece[0]]:
                piece[1]()

        if do_a:
            for r8 in range(T // PN // 2):
                prenorm(r8)
        elif do_b:
            u_scr[pb, POOL_HALO + T:, :] = jnp.zeros((POOL_HALO, D_POOL), f32)
        s = {0: b_scores(0)} if do_b else {}
        for n in range(len(units)):
            if do_b and n + 1 < len(units):
                s[n + 1] = b_scores(n + 1)
            pd, sink_term = {}, {}
            run(heavy[n][0])
            if do_b:
                b_soft(n, s[n], 0, pd, sink_term)
            run(heavy[n][1])
            if do_b:
                b_soft(n, s.pop(n), 1, pd, sink_term)
            run(heavy[n][2])
            if do_b:
                b_pv(n, pd, sink_term)
            for piece in slot_end.get(n, ()):
                run(piece)

        if do_a:
            k_scr[pb, :, 0:BLOCK, :] = k_scr[pa, :, T:T + BLOCK, :]
            v_scr[pb, :, 0:2 * BLOCK, :] = v_scr[pa, :, R * 2 * BLOCK:(R + 1) * 2 * BLOCK, :]
            u_scr[pb, 0:POOL_HALO, :] = u_scr[pa, T:T + POOL_HALO, :]

    steady = (i >= 2) & (i < n_tiles)
    for pa in range(2):
        @pl.when(steady & (i % 2 == pa))
        def _(pa=pa):
            run_step(pa, 1 - pa)

    def ramp(step, before=None, **stages):
        @pl.when(i == step)
        def _():
            if before is not None:
                before()
            run_step(step % 2, 1 - step % 2, **stages)

    ramp(0, do_b=False, do_c=False)
    ramp(1, before=convert_wout, do_c=False)
    ramp(n_tiles, do_a=False)
    ramp(n_tiles + 1, do_a=False, do_b=False)


def kernel(x, pre_norm_g, w_in, rel_bias, attn_sink, pool_w, pool_scale, w_out, post_norm_g):
    B, S, D = x.shape
    assert B == 1 and D == D_MODEL and S % TILE == 0 and S // TILE >= 3
    assert pre_norm_g.shape == (1, D) and attn_sink.shape == (1, N_Q_HEADS), "single layer"
    assert rel_bias.shape == (N_BUCKETS, N_Q_HEADS) and rel_bias.dtype == attn_sink.dtype == jnp.float32
    T = TILE
    R = T // BLOCK
    n_tiles = S // T
    bf16 = jnp.bfloat16

    x2 = x.reshape(S, D)
    smem = pl.BlockSpec(memory_space=pltpu.SMEM)
    hbm = pl.BlockSpec(memory_space=pltpu.HBM)
    full = lambda shape: pl.BlockSpec(shape, lambda i: (0,) * len(shape))

    out = pl.pallas_call(
        functools.partial(_layer_kernel, n_tiles=n_tiles),
        grid=(n_tiles + 2,),
        in_specs=[
            smem, smem,
            pl.BlockSpec((T, D), lambda i: (jnp.minimum(i, n_tiles - 1), 0)),
            pl.BlockSpec((T, D), lambda i: (jnp.maximum(i - 2, 0), 0)),
            full((1, D)),
            hbm,
            full((len(POOL_WINDOWS), POOL_GROUP, POOL_GROUP)),
            full((1, D_POOL)),
            hbm,
            full((1, D)),
        ],
        out_specs=pl.BlockSpec((T, D), lambda i: (jnp.maximum(i - 2, 0), 0)),
        out_shape=jax.ShapeDtypeStruct((S, D), x.dtype),
        scratch_shapes=[
            pltpu.VMEM((3, N_Q_HEADS, BLOCK, 3 * BLOCK), jnp.float32),
            pltpu.VMEM((2, T, D_ATTN), bf16),
            pltpu.VMEM((2, 4, T + 2 * BLOCK, LANES), bf16),
            pltpu.VMEM((2, N_KV_HEADS, (R + 2) * 2 * BLOCK, LANES), bf16),
            pltpu.VMEM((2, T + 2 * POOL_HALO, D_POOL), jnp.float32),
            pltpu.VMEM((2, T, D_ATTN), jnp.float32),
            pltpu.VMEM((2, T, D_POOL), jnp.float32),
            pltpu.VMEM((2, T, D), bf16),
            pltpu.VMEM((T, D), bf16),
            pltpu.VMEM((len(POOL_WINDOWS) // 2, CHUNK, CHUNK), bf16),
            pltpu.VMEM((3 * 2 * BLOCK, LANES), bf16),
            pltpu.VMEM((D, D), bf16),
            pltpu.VMEM((D, D_IN), bf16),
            pltpu.VMEM((D, D_IN), jnp.float32),
            pltpu.VMEM((D, D), jnp.float32),
            pltpu.SemaphoreType.DMA((2 * (D // LANES) * W_COPIES_PER_CHUNK,)),
        ],
        compiler_params=pltpu.CompilerParams(
            dimension_semantics=("arbitrary",),
            vmem_limit_bytes=VMEM_LIMIT_BYTES,
        ),
        name="hymba_layer_fused",
    )(
        rel_bias.T, attn_sink,
        x2, x2,
        pre_norm_g,
        w_in.reshape(D, D_IN),
        pool_w.reshape(len(POOL_WINDOWS), POOL_GROUP, POOL_GROUP),
        pool_scale,
        w_out.reshape(D, D),
        post_norm_g,
    )
    return out.reshape(B, S, D)
```

```python
import functools
import math

import jax
import jax.numpy as jnp
from jax import lax
from jax.experimental import pallas as pl
from jax.experimental.pallas import tpu as pltpu

D_MODEL = 1024
D_ATTN = 512
D_POOL = 512
HEAD_DIM = 64
N_Q_HEADS = 8
N_KV_HEADS = 2
Q_PER_KV = 4
D_KV = 128
WINDOW = 128
BLOCK = 128
N_BUCKETS = 32
POOL_WINDOWS = (2, 4, 8, 16)
POOL_GROUP = 128
D_IN = 2304
EPS = 1e-6
NEG = -1e30
LOG2E = math.log2(math.e)

LANES = 128
CHUNK = 2 * LANES
POOL_HALO = 16
TILE = 512
W_COPIES_PER_CHUNK = 8
VMEM_LIMIT_BYTES = 62 * 1024 * 1024

Q0, K0, V0, GA0, U0, GP0 = 0, 512, 640, 768, 1280, 1792

_NT = (((1,), (1,)), ((), ()))


def _silu_of_half(h):
    return h + h * jnp.tanh(h)


def _t5_bucket_exact(rel):
    nb = N_BUCKETS // 2
    max_exact = nb // 2
    ret = jnp.where(rel > 0, nb, 0)
    n = jnp.abs(rel)
    n2 = n * n
    large = jnp.full(rel.shape, max_exact, jnp.int32)
    for j in range(1, nb - max_exact):
        large = large + jnp.where(n2 >= (max_exact * max_exact) << j, 1, 0)
    return ret + jnp.where(n < max_exact, n, large)


def _window_sums(ug, w):
    n = ug.shape[0]
    if w == 2:
        return ug + pltpu.roll(ug, 1, axis=0)
    acc = ug + pltpu.roll(ug, n - 1, axis=0)
    span = 2
    while span * 2 < w:
        acc = acc + pltpu.roll(acc, n - span, axis=0)
        span *= 2
    return acc + pltpu.roll(acc, span, axis=0)


def _layer_kernel(relb_ref, sink_ref,
                  xa_ref, xb_ref, gpre_ref, win_hbm, poolw_ref, pscale_ref, wout_hbm, gpost_ref,
                  out_ref,
                  bias_scr, q_scr, k_scr, v_scr, u_scr, sga_scr, sgp_scr, mix_scr, xn_scr, pw2_scr, ones_scr,
                  wout_scr, win_scr, win_f32, wout_f32, w_sem,
                  *, n_tiles):
    i = pl.program_id(0)
    T = TILE
    R = T // BLOCK
    D = D_MODEL
    f32, bf16 = jnp.float32, jnp.bfloat16
    lane = lax.broadcasted_iota(jnp.int32, (1, LANES), 1)
    lo = lane < HEAD_DIM

    n_wchunks = D // LANES
    W_SPLIT = W_COPIES_PER_CHUNK
    sub_rows = LANES // W_SPLIT

    def w_copies(hbm, dst, kb, sem0):
        copies = []
        for j in range(W_SPLIT):
            rows = pl.ds(kb * LANES + j * sub_rows, sub_rows)
            copies.append(pltpu.make_async_copy(hbm.at[rows, :], dst.at[rows, :],
                                                w_sem.at[sem0 + kb * W_SPLIT + j]))
        return copies

    win_copy = functools.partial(w_copies, win_hbm, win_f32, sem0=0)
    wout_copy = functools.partial(w_copies, wout_hbm, wout_f32, sem0=n_wchunks * W_SPLIT)

    def convert_wout():
        for kb in range(n_wchunks):
            rows = slice(kb * LANES, (kb + 1) * LANES)
            for cp in wout_copy(kb=kb):
                cp.wait()
            wout_scr[rows, :] = wout_f32[rows, :].astype(bf16)

    @pl.when(i == 0)
    def _():
        for kb in range(n_wchunks):
            for cp in win_copy(kb=kb):
                cp.start()

        TOEP = 4 * BLOCK
        rel_row = lax.broadcasted_iota(jnp.int32, (8, TOEP), 1) - (TOEP // 2 - 1)
        bucket_row = _t5_bucket_exact(rel_row)
        band_row = jnp.abs(rel_row) <= WINDOW
        kj = lax.broadcasted_iota(jnp.int32, (BLOCK, 3 * BLOCK), 1)
        for h in range(N_Q_HEADS):
            def body(b, acc, h=h):
                return jnp.where(bucket_row == b, relb_ref[h, b], acc)
            row = lax.fori_loop(0, N_BUCKETS, body, jnp.zeros((8, TOEP), f32)) * LOG2E
            row = jnp.where(band_row, row, NEG)[0:1, :]
            tbl = pltpu.roll(jnp.broadcast_to(row, (BLOCK, TOEP)), BLOCK + TOEP // 2 + 1, axis=1,
                             stride=1, stride_axis=0)[:, :3 * BLOCK]
            bias_scr[0, h] = tbl
            bias_scr[1, h] = jnp.where(kj >= BLOCK, tbl, NEG)
            bias_scr[2, h] = jnp.where(kj < 2 * BLOCK, tbl, NEG)
        k_scr[0, :, 0:BLOCK, :] = jnp.zeros((4, BLOCK, LANES), bf16)
        v_scr[0, :, 0:2 * BLOCK, :] = jnp.zeros((N_KV_HEADS, 2 * BLOCK, LANES), bf16)
        u_scr[0, 0:POOL_HALO, :] = jnp.zeros((POOL_HALO, D_POOL), f32)
        pw2_scr[...] = jnp.zeros(pw2_scr.shape, bf16)
        for gi in range(len(POOL_WINDOWS)):
            r0 = (gi % 2) * POOL_GROUP
            pw2_scr[gi // 2, r0:r0 + POOL_GROUP, r0:r0 + POOL_GROUP] = poolw_ref[gi].astype(bf16)
        row_lo = lax.broadcasted_iota(jnp.int32, (2 * BLOCK, LANES), 0) < BLOCK
        ones_pat = jnp.where(row_lo, jnp.where(lo, 1.0, 0.0), jnp.where(lo, 0.0, 1.0)).astype(bf16)
        for kb in range(3):
            ones_scr[kb * 2 * BLOCK:(kb + 1) * 2 * BLOCK, :] = ones_pat
        g_rows = jnp.broadcast_to(gpre_ref[...], (LANES, D))
        for kb in range(D // LANES):
            rows = slice(kb * LANES, (kb + 1) * LANES)
            g_col = jnp.transpose(g_rows[:, rows])
            g_col = jnp.concatenate([g_col, g_col], axis=1)
            g_half = 0.5 * g_col
            for cp in win_copy(kb=kb):
                cp.wait()
            for c in range(D_IN // CHUNK):
                cs = slice(c * CHUNK, (c + 1) * CHUNK)
                is_gate = GA0 <= cs.start < U0 or cs.start >= GP0
                win_scr[rows, cs] = (win_f32[rows, cs] * (g_half if is_gate else g_col)).astype(bf16)
        for kb in range(n_wchunks):
            for cp in wout_copy(kb=kb):
                cp.start()

    b_is_first = i == 1
    b_is_last = i == n_tiles
    tile_b = jnp.clip(i - 1, 0, n_tiles - 1)

    def run_step(pa, pb, do_a=True, do_b=True, do_c=True):
        enabled = {"a": do_a, "b": do_b, "c": do_c}
        zero = jnp.zeros((), bf16)

        def proj(c0, width=CHUNK):
            return jnp.dot(xn_scr[...], win_scr[:, c0:c0 + width], preferred_element_type=f32)

        def a_kv():
            kv = proj(K0, 2 * D_KV)
            k2 = kv[:, :D_KV].astype(bf16)
            kr = pltpu.roll(kv[:, :D_KV], HEAD_DIM, axis=1).astype(bf16)
            kvars = (jnp.where(lo, k2, zero), jnp.where(lo, zero, kr),
                     jnp.where(lo, kr, zero), jnp.where(lo, zero, k2))
            for var, kk in enumerate(kvars):
                k_scr[pa, var, BLOCK:BLOCK + T, :] = kk
                k_scr[pb, var, BLOCK + T:, :] = kk[:BLOCK]
            v2 = kv[:, D_KV:].astype(bf16)
            vr = pltpu.roll(kv[:, D_KV:], HEAD_DIM, axis=1).astype(bf16)
            vvars = ((jnp.where(lo, v2, zero), jnp.where(lo, zero, vr)),
                     (jnp.where(lo, vr, zero), jnp.where(lo, zero, v2)))
            for j in range(N_KV_HEADS):
                for par in range(2):
                    vv = vvars[j][par]
                    for kb in range(R):
                        r0 = (kb + 1) * 2 * BLOCK + par * BLOCK
                        v_scr[pa, j, r0:r0 + BLOCK, :] = vv[kb * BLOCK:(kb + 1) * BLOCK]
                    r0 = (R + 1) * 2 * BLOCK + par * BLOCK
                    v_scr[pb, j, r0:r0 + BLOCK, :] = vv[:BLOCK]

        def a_q(c):
            cs = slice(c * CHUNK, (c + 1) * CHUNK)
            q_scr[pa, :, cs] = (proj(Q0 + cs.start) * (HEAD_DIM ** -0.5 * LOG2E)).astype(bf16)

        def a_gate(c0, dst, c):
            cs = slice(c * CHUNK, (c + 1) * CHUNK)
            dst[pa, :, cs] = _silu_of_half(proj(c0 + cs.start))

        def a_u(c):
            cs = slice(c * CHUNK, (c + 1) * CHUNK)
            u = proj(U0 + cs.start)
            u_scr[pa, POOL_HALO:POOL_HALO + T, cs] = u
            u_scr[pb, POOL_HALO + T:, cs] = jnp.where(b_is_last, 0.0, u[:POOL_HALO])

        units = [(b, j) for b in range(R) for j in range(N_KV_HEADS)]
        assert len(units) == 8, "the program order below is written for eight attention units per tile"

        def b_scores(n):
            b, j = units[n]
            rows = slice(b * BLOCK, (b + 1) * BLOCK)
            win = slice(b * BLOCK, b * BLOCK + 3 * BLOCK)
            c0 = j * Q_PER_KV * HEAD_DIM
            q2 = jnp.concatenate([q_scr[pb, rows, c0:c0 + LANES],
                                  q_scr[pb, rows, c0 + LANES:c0 + 2 * LANES]], axis=0)
            return [lax.dot_general(q2, k_scr[pb, 2 * j + par, win, :], _NT, preferred_element_type=f32)
                    for par in range(2)]

        def b_soft(n, s2s, par, pd, sink_term):
            b, j = units[n]
            if b == 0:
                tbl = jnp.where(b_is_first, 1, 0)
            elif b == R - 1:
                tbl = jnp.where(b_is_last, 2, 0)
            else:
                tbl = 0
            for half in range(2):
                h = j * Q_PER_KV + 2 * half + par
                s = s2s[par][half * BLOCK:(half + 1) * BLOCK, :] + bias_scr[tbl, h]
                sink = sink_ref[0, h] * LOG2E
                m = jnp.maximum(jnp.max(s, axis=-1, keepdims=True), sink)
                pd[(half, par)] = jnp.exp2(s - m).astype(bf16)
                sink_term[(half, par)] = jnp.exp2(sink - m)

        def b_pv(n, pd, sink_term):
            b, j = units[n]
            rows = slice(b * BLOCK, (b + 1) * BLOCK)
            c0 = j * Q_PER_KV * HEAD_DIM
            pcat = jnp.concatenate(
                [jnp.concatenate([pd[(half, par)][:, kb * BLOCK:(kb + 1) * BLOCK]
                                  for kb in range(3) for par in range(2)], axis=1)
                 for half in range(2)], axis=0)
            vwin = jnp.concatenate([v_scr[pb, j, 2 * b * BLOCK:2 * b * BLOCK + 6 * BLOCK, :], ones_scr[...]],
                                   axis=1)
            o2 = jnp.dot(pcat, vwin, preferred_element_type=f32)
            for half in range(2):
                hr = slice(half * BLOCK, (half + 1) * BLOCK)
                denom = o2[hr, LANES:] + jnp.where(lo, sink_term[(half, 0)], sink_term[(half, 1)])
                cs = c0 + half * LANES
                a = (o2[hr, :LANES] / denom) * sga_scr[pb, rows, cs:cs + LANES]
                mix_scr[pb, rows, cs:cs + LANES] = a.astype(bf16)

        def b_pool_y(gi):
            w = POOL_WINDOWS[gi]
            seq = n_tiles * T
            E = POOL_HALO
            left = w // 2
            right = w - 1 - left
            cols = slice(gi * POOL_GROUP, (gi + 1) * POOL_GROUP)
            ug = u_scr[pb, :, cols]
            wsum = _window_sums(ug, w)[E:E + T]
            uc = ug[E:E + T]
            y_mid = wsum * (1.0 / w) - uc
            parts = []
            for r0 in (0, T - E):
                t_glob = tile_b * T + r0 + lax.broadcasted_iota(jnp.int32, (E, LANES), 0)
                cnt = (jnp.minimum(t_glob + right + 1, seq) - jnp.maximum(t_glob - left, 0)).astype(f32)
                parts.append(wsum[r0:r0 + E] / cnt - uc[r0:r0 + E])
            return jnp.concatenate([parts[0], y_mid[E:T - E], parts[1]], axis=0).astype(bf16)

        def b_pool_pair(pr):
            cols = slice(pr * CHUNK, (pr + 1) * CHUNK)
            y2 = jnp.concatenate([b_pool_y(2 * pr), b_pool_y(2 * pr + 1)], axis=1)
            yw = jnp.dot(y2, pw2_scr[pr], preferred_element_type=f32)
            pz = (yw * pscale_ref[:, cols]) * sgp_scr[pb, :, cols]
            mix_scr[pb, :, D_ATTN + pr * CHUNK:D_ATTN + (pr + 1) * CHUNK] = pz.astype(bf16)

        half_cols = D // 2
        mixed_halves = []

        def c_out(ch):
            m = jnp.dot(mix_scr[pa], wout_scr[:, ch * half_cols:(ch + 1) * half_cols], preferred_element_type=f32)
            mixed_halves.append((m, jnp.sum(m * m, axis=-1, keepdims=True)))

        def c_post(rb):
            rows = slice(rb * BLOCK, (rb + 1) * BLOCK)
            mixed = jnp.concatenate([m[rows] for m, _ in mixed_halves], axis=1)
            ssq = mixed_halves[0][1][rows]
            for _, part_ssq in mixed_halves[1:]:
                ssq = ssq + part_ssq[rows]
            y = (mixed * lax.rsqrt(ssq * (1.0 / D) + EPS)) * gpost_ref[...]
            out_ref[rows, :] = xb_ref[rows, :] + y

        PN = BLOCK // 2

        def prenorm(r8):
            rows = slice(r8 * PN, (r8 + 1) * PN)
            xv = xa_ref[rows, :]
            ms = jnp.mean(xv * xv, axis=-1, keepdims=True)
            xn_scr[rows, :] = (xv * lax.rsqrt(ms + EPS)).astype(bf16)

        part = functools.partial
        out_q = lambda ch: ("c", part(c_out, ch))
        post = lambda rb: ("c", part(c_post, rb))
        ga = lambda c: ("a", part(a_gate, GA0, sga_scr, c))
        gp = lambda c: ("a", part(a_gate, GP0, sgp_scr, c))
        uu = lambda c: ("a", part(a_u, c))
        qq = lambda c: ("a", part(a_q, c))
        kv = ("a", a_kv)
        pn = lambda r8: ("a", part(prenorm, r8))
        heavy = [[out_q(0), None, None],
                 [out_q(1), None, None],
                 [kv, uu(0), None],
                 [uu(1), qq(0), None],
                 [qq(1), ga(0), None],
                 [ga(1), None, None],
                 [gp(0), None, None],
                 [gp(1), None, None]]
        assert len(heavy) == len(units)
        slot_end = {0: [pn(4), pn(5), pn(6), pn(7)],
                    1: [post(0), post(1)],
                    2: [post(2), post(3)],
                    3: [("b", part(b_pool_pair, 0))],
                    4: [("b", part(b_pool_pair, 1))]}

        def run(piece):
            if piece is not None and enabled[piece[0]]:
                piece[1]()

        if do_a:
            for r8 in range(T // PN // 2):
                prenorm(r8)
        elif do_b:
            u_scr[pb, POOL_HALO + T:, :] = jnp.zeros((POOL_HALO, D_POOL), f32)
        s = {0: b_scores(0)} if do_b else {}
        for n in range(len(units)):
            if do_b and n + 1 < len(units):
                s[n + 1] = b_scores(n + 1)
            pd, sink_term = {}, {}
            run(heavy[n][0])
            if do_b:
                b_soft(n, s[n], 0, pd, sink_term)
            run(heavy[n][1])
            if do_b:
                b_soft(n, s.pop(n), 1, pd, sink_term)
            run(heavy[n][2])
            if do_b:
                b_pv(n, pd, sink_term)
            for piece in slot_end.get(n, ()):
                run(piece)

        if do_a:
            k_scr[pb, :, 0:BLOCK, :] = k_scr[pa, :, T:T + BLOCK, :]
            v_scr[pb, :, 0:2 * BLOCK, :] = v_scr[pa, :, R * 2 * BLOCK:(R + 1) * 2 * BLOCK, :]
            u_scr[pb, 0:POOL_HALO, :] = u_scr[pa, T:T + POOL_HALO, :]

    steady = (i >= 2) & (i < n_tiles)
    for pa in range(2):
        @pl.when(steady & (i % 2 == pa))
        def _(pa=pa):
            run_step(pa, 1 - pa)

    def ramp(step, before=None, **stages):
        @pl.when(i == step)
        def _():
            if before is not None:
                before()
            run_step(step % 2, 1 - step % 2, **stages)

    ramp(0, do_b=False, do_c=False)
    ramp(1, before=convert_wout, do_c=False)
    ramp(n_tiles, do_a=False)
    ramp(n_tiles + 1, do_a=False, do_b=False)


def kernel(x, pre_norm_g, w_in, rel_bias, attn_sink, pool_w, pool_scale, w_out, post_norm_g):
    B, S, D = x.shape
    assert B == 1 and D == D_MODEL and S % TILE == 0 and S // TILE >= 3
    assert pre_norm_g.shape == (1, D) and attn_sink.shape == (1, N_Q_HEADS), "single layer"
    assert rel_bias.shape == (N_BUCKETS, N_Q_HEADS) and rel_bias.dtype == attn_sink.dtype == jnp.float32
    T = TILE
    R = T // BLOCK
    n_tiles = S // T
    bf16 = jnp.bfloat16

    x2 = x.reshape(S, D)
    smem = pl.BlockSpec(memory_space=pltpu.SMEM)
    hbm = pl.BlockSpec(memory_space=pltpu.HBM)
    full = lambda shape: pl.BlockSpec(shape, lambda i: (0,) * len(shape))

    out = pl.pallas_call(
        functools.partial(_layer_kernel, n_tiles=n_tiles),
        grid=(n_tiles + 2,),
        in_specs=[
            smem, smem,
            pl.BlockSpec((T, D), lambda i: (jnp.minimum(i, n_tiles - 1), 0)),
            pl.BlockSpec((T, D), lambda i: (jnp.maximum(i - 2, 0), 0)),
            full((1, D)),
            hbm,
            full((len(POOL_WINDOWS), POOL_GROUP, POOL_GROUP)),
            full((1, D_POOL)),
            hbm,
            full((1, D)),
        ],
        out_specs=pl.BlockSpec((T, D), lambda i: (jnp.maximum(i - 2, 0), 0)),
        out_shape=jax.ShapeDtypeStruct((S, D), x.dtype),
        scratch_shapes=[
            pltpu.VMEM((3, N_Q_HEADS, BLOCK, 3 * BLOCK), jnp.float32),
            pltpu.VMEM((2, T, D_ATTN), bf16),
            pltpu.VMEM((2, 4, T + 2 * BLOCK, LANES), bf16),
            pltpu.VMEM((2, N_KV_HEADS, (R + 2) * 2 * BLOCK, LANES), bf16),
            pltpu.VMEM((2, T + 2 * POOL_HALO, D_POOL), jnp.float32),
            pltpu.VMEM((2, T, D_ATTN), jnp.float32),
            pltpu.VMEM((2, T, D_POOL), jnp.float32),
            pltpu.VMEM((2, T, D), bf16),
            pltpu.VMEM((T, D), bf16),
            pltpu.VMEM((len(POOL_WINDOWS) // 2, CHUNK, CHUNK), bf16),
            pltpu.VMEM((3 * 2 * BLOCK, LANES), bf16),
            pltpu.VMEM((D, D), bf16),
            pltpu.VMEM((D, D_IN), bf16),
            pltpu.VMEM((D, D_IN), jnp.float32),
            pltpu.VMEM((D, D), jnp.float32),
            pltpu.SemaphoreType.DMA((2 * (D // LANES) * W_COPIES_PER_CHUNK,)),
        ],
        compiler_params=pltpu.CompilerParams(
            dimension_semantics=("arbitrary",),
            vmem_limit_bytes=VMEM_LIMIT_BYTES,
        ),
        name="hymba_layer_fused",
    )(
        rel_bias.T, attn_sink,
        x2, x2,
        pre_norm_g,
        w_in.reshape(D, D_IN),
        pool_w.reshape(len(POOL_WINDOWS), POOL_GROUP, POOL_GROUP),
        pool_scale,
        w_out.reshape(D, D),
        post_norm_g,
    )
    return out.reshape(B, S, D)
```

```python
import functools
import math

import jax
import jax.numpy as jnp
from jax import lax
from jax.experimental import pallas as pl
from jax.experimental.pallas import tpu as pltpu

D_MODEL = 1024
D_ATTN = 512
D_POOL = 512
HEAD_DIM = 64
N_Q_HEADS = 8
N_KV_HEADS = 2
Q_PER_KV = 4
D_KV = 128
WINDOW = 128
BLOCK = 128
N_BUCKETS = 32
POOL_WINDOWS = (2, 4, 8, 16)
POOL_GROUP = 128
D_IN = 2304
EPS = 1e-6
NEG = -1e30
LOG2E = math.log2(math.e)

LANES = 128
CHUNK = 2 * LANES
POOL_HALO = 16
TILE = 512
W_COPIES_PER_CHUNK = 4
VMEM_LIMIT_BYTES = 62 * 1024 * 1024

Q0, K0, V0, GA0, U0, GP0 = 0, 512, 640, 768, 1280, 1792

_NT = (((1,), (1,)), ((), ()))


def _silu_of_half(h):
    return h + h * jnp.tanh(h)


def _t5_bucket_exact(rel):
    nb = N_BUCKETS // 2
    max_exact = nb // 2
    ret = jnp.where(rel > 0, nb, 0)
    n = jnp.abs(rel)
    n2 = n * n
    large = jnp.full(rel.shape, max_exact, jnp.int32)
    for j in range(1, nb - max_exact):
        large = large + jnp.where(n2 >= (max_exact * max_exact) << j, 1, 0)
    return ret + jnp.where(n < max_exact, n, large)


def _window_sums(ug, w):
    n = ug.shape[0]
    if w == 2:
        return ug + pltpu.roll(ug, 1, axis=0)
    acc = ug + pltpu.roll(ug, n - 1, axis=0)
    span = 2
    while span * 2 < w:
        acc = acc + pltpu.roll(acc, n - span, axis=0)
        span *= 2
    return acc + pltpu.roll(acc, span, axis=0)


def _layer_kernel(relb_ref, sink_ref,
                  xa_ref, x_hbm, gpre_ref, win_hbm, poolw_ref, pscale_ref, wout_hbm, gpost_ref,
                  out_ref,
                  bias_scr, q_scr, k_scr, v_scr, u_scr, sga_scr, sgp_scr, mix_scr, xn_scr, pw2_scr, ones_scr,
                  wout_scr, win_scr, win_f32, wout_f32, w_sem, xres_scr, x_sem,
                  *, n_tiles):
    i = pl.program_id(0)
    T = TILE
    R = T // BLOCK
    D = D_MODEL
    f32, bf16 = jnp.float32, jnp.bfloat16
    lane = lax.broadcasted_iota(jnp.int32, (1, LANES), 1)
    lo = lane < HEAD_DIM

    n_wchunks = D // LANES
    W_SPLIT = W_COPIES_PER_CHUNK
    sub_rows = LANES // W_SPLIT

    def w_copies(hbm, dst, kb, sem0):
        copies = []
        for j in range(W_SPLIT):
            rows = pl.ds(kb * LANES + j * sub_rows, sub_rows)
            copies.append(pltpu.make_async_copy(hbm.at[rows, :], dst.at[rows, :],
                                                w_sem.at[sem0 + kb * W_SPLIT + j]))
        return copies

    win_copy = functools.partial(w_copies, win_hbm, win_f32, sem0=0)
    wout_copy = functools.partial(w_copies, wout_hbm, wout_f32, sem0=n_wchunks * W_SPLIT)

    def convert_wout():
        for kb in range(n_wchunks):
            rows = slice(kb * LANES, (kb + 1) * LANES)
            for cp in wout_copy(kb=kb):
                cp.wait()
            wout_scr[rows, :] = wout_f32[rows, :].astype(bf16)

    @pl.when(i == 0)
    def _():
        for kb in range(n_wchunks):
            for cp in win_copy(kb=kb):
                cp.start()

        TOEP = 4 * BLOCK
        rel_row = lax.broadcasted_iota(jnp.int32, (8, TOEP), 1) - (TOEP // 2 - 1)
        bucket_row = _t5_bucket_exact(rel_row)
        band_row = jnp.abs(rel_row) <= WINDOW
        kj = lax.broadcasted_iota(jnp.int32, (BLOCK, 3 * BLOCK), 1)
        for h in range(N_Q_HEADS):
            def body(b, acc, h=h):
                return jnp.where(bucket_row == b, relb_ref[h, b], acc)
            row = lax.fori_loop(0, N_BUCKETS, body, jnp.zeros((8, TOEP), f32)) * LOG2E
            row = jnp.where(band_row, row, NEG)[0:1, :]
            tbl = pltpu.roll(jnp.broadcast_to(row, (BLOCK, TOEP)), BLOCK + TOEP // 2 + 1, axis=1,
                             stride=1, stride_axis=0)[:, :3 * BLOCK]
            bias_scr[0, h] = tbl
            bias_scr[1, h] = jnp.where(kj >= BLOCK, tbl, NEG)
            bias_scr[2, h] = jnp.where(kj < 2 * BLOCK, tbl, NEG)
        k_scr[0, :, 0:BLOCK, :] = jnp.zeros((4, BLOCK, LANES), bf16)
        v_scr[0, :, 0:2 * BLOCK, :] = jnp.zeros((N_KV_HEADS, 2 * BLOCK, LANES), bf16)
        u_scr[0, 0:POOL_HALO, :] = jnp.zeros((POOL_HALO, D_POOL), f32)
        pw2_scr[...] = jnp.zeros(pw2_scr.shape, bf16)
        for gi in range(len(POOL_WINDOWS)):
            r0 = (gi % 2) * POOL_GROUP
            pw2_scr[gi // 2, r0:r0 + POOL_GROUP, r0:r0 + POOL_GROUP] = poolw_ref[gi].astype(bf16)
        row_lo = lax.broadcasted_iota(jnp.int32, (2 * BLOCK, LANES), 0) < BLOCK
        ones_pat = jnp.where(row_lo, jnp.where(lo, 1.0, 0.0), jnp.where(lo, 0.0, 1.0)).astype(bf16)
        for kb in range(3):
            ones_scr[kb * 2 * BLOCK:(kb + 1) * 2 * BLOCK, :] = ones_pat
        g_rows = jnp.broadcast_to(gpre_ref[...], (LANES, D))
        for kb in range(D // LANES):
            rows = slice(kb * LANES, (kb + 1) * LANES)
            g_col = jnp.transpose(g_rows[:, rows])
            g_col = jnp.concatenate([g_col, g_col], axis=1)
            g_half = 0.5 * g_col
            for cp in win_copy(kb=kb):
                cp.wait()
            for c in range(D_IN // CHUNK):
                cs = slice(c * CHUNK, (c + 1) * CHUNK)
                is_gate = GA0 <= cs.start < U0 or cs.start >= GP0
                win_scr[rows, cs] = (win_f32[rows, cs] * (g_half if is_gate else g_col)).astype(bf16)
        for kb in range(n_wchunks):
            for cp in wout_copy(kb=kb):
                cp.start()

    b_is_first = i == 1
    b_is_last = i == n_tiles
    tile_b = jnp.clip(i - 1, 0, n_tiles - 1)

    def run_step(pa, pb, do_a=True, do_b=True, do_c=True):
        enabled = {"a": do_a, "b": do_b, "c": do_c}
        zero = jnp.zeros((), bf16)

        def xres_copy(tile, slot):
            rows = pl.ds(pl.multiple_of(tile * T, T), T)
            return pltpu.make_async_copy(x_hbm.at[rows, :], xres_scr.at[slot], x_sem.at[slot])

        if do_c:
            xres_copy(i - 2, pa).wait()
        if do_b:
            xres_copy(i - 1, pb).start()

        def proj(c0, width=CHUNK):
            return jnp.dot(xn_scr[...], win_scr[:, c0:c0 + width], preferred_element_type=f32)

        def a_kv():
            kv = proj(K0, 2 * D_KV)
            k2 = kv[:, :D_KV].astype(bf16)
            kr = pltpu.roll(kv[:, :D_KV], HEAD_DIM, axis=1).astype(bf16)
            kvars = (jnp.where(lo, k2, zero), jnp.where(lo, zero, kr),
                     jnp.where(lo, kr, zero), jnp.where(lo, zero, k2))
            for var, kk in enumerate(kvars):
                k_scr[pa, var, BLOCK:BLOCK + T, :] = kk
                k_scr[pb, var, BLOCK + T:, :] = kk[:BLOCK]
            v2 = kv[:, D_KV:].astype(bf16)
            vr = pltpu.roll(kv[:, D_KV:], HEAD_DIM, axis=1).astype(bf16)
            vvars = ((jnp.where(lo, v2, zero), jnp.where(lo, zero, vr)),
                     (jnp.where(lo, vr, zero), jnp.where(lo, zero, v2)))
            for j in range(N_KV_HEADS):
                for par in range(2):
                    vv = vvars[j][par]
                    for kb in range(R):
                        r0 = (kb + 1) * 2 * BLOCK + par * BLOCK
                        v_scr[pa, j, r0:r0 + BLOCK, :] = vv[kb * BLOCK:(kb + 1) * BLOCK]
                    r0 = (R + 1) * 2 * BLOCK + par * BLOCK
                    v_scr[pb, j, r0:r0 + BLOCK, :] = vv[:BLOCK]

        def a_q(c):
            cs = slice(c * CHUNK, (c + 1) * CHUNK)
            q_scr[pa, :, cs] = (proj(Q0 + cs.start) * (HEAD_DIM ** -0.5 * LOG2E)).astype(bf16)

        def a_gate(c0, dst, c):
            cs = slice(c * CHUNK, (c + 1) * CHUNK)
            dst[pa, :, cs] = _silu_of_half(proj(c0 + cs.start))

        def a_u(c):
            cs = slice(c * CHUNK, (c + 1) * CHUNK)
            u = proj(U0 + cs.start)
            u_scr[pa, POOL_HALO:POOL_HALO + T, cs] = u
            u_scr[pb, POOL_HALO + T:, cs] = jnp.where(b_is_last, 0.0, u[:POOL_HALO])

        units = [(b, j) for b in range(R) for j in range(N_KV_HEADS)]
        assert len(units) == 8, "the program order below is written for eight attention units per tile"

        def b_scores(n):
            b, j = units[n]
            rows = slice(b * BLOCK, (b + 1) * BLOCK)
            win = slice(b * BLOCK, b * BLOCK + 3 * BLOCK)
            c0 = j * Q_PER_KV * HEAD_DIM
            q2 = jnp.concatenate([q_scr[pb, rows, c0:c0 + LANES],
                                  q_scr[pb, rows, c0 + LANES:c0 + 2 * LANES]], axis=0)
            return [lax.dot_general(q2, k_scr[pb, 2 * j + par, win, :], _NT, preferred_element_type=f32)
                    for par in range(2)]

        def b_soft(n, s2s, par, pd, sink_term):
            b, j = units[n]
            if b == 0:
                tbl = jnp.where(b_is_first, 1, 0)
            elif b == R - 1:
                tbl = jnp.where(b_is_last, 2, 0)
            else:
                tbl = 0
            for half in range(2):
                h = j * Q_PER_KV + 2 * half + par
                s = s2s[par][half * BLOCK:(half + 1) * BLOCK, :] + bias_scr[tbl, h]
                sink = sink_ref[0, h] * LOG2E
                m = jnp.maximum(jnp.max(s, axis=-1, keepdims=True), sink)
                pd[(half, par)] = jnp.exp2(s - m).astype(bf16)
                sink_term[(half, par)] = jnp.exp2(sink - m)

        def b_pv(n, pd, sink_term):
            b, j = units[n]
            rows = slice(b * BLOCK, (b + 1) * BLOCK)
            c0 = j * Q_PER_KV * HEAD_DIM
            pcat = jnp.concatenate(
                [jnp.concatenate([pd[(half, par)][:, kb * BLOCK:(kb + 1) * BLOCK]
                                  for kb in range(3) for par in range(2)], axis=1)
                 for half in range(2)], axis=0)
            vwin = jnp.concatenate([v_scr[pb, j, 2 * b * BLOCK:2 * b * BLOCK + 6 * BLOCK, :], ones_scr[...]],
                                   axis=1)
            o2 = jnp.dot(pcat, vwin, preferred_element_type=f32)
            for half in range(2):
                hr = slice(half * BLOCK, (half + 1) * BLOCK)
                denom = o2[hr, LANES:] + jnp.where(lo, sink_term[(half, 0)], sink_term[(half, 1)])
                cs = c0 + half * LANES
                a = (o2[hr, :LANES] / denom) * sga_scr[pb, rows, cs:cs + LANES]
                mix_scr[pb, rows, cs:cs + LANES] = a.astype(bf16)

        def b_pool_y(gi):
            w = POOL_WINDOWS[gi]
            seq = n_tiles * T
            E = POOL_HALO
            left = w // 2
            right = w - 1 - left
            cols = slice(gi * POOL_GROUP, (gi + 1) * POOL_GROUP)
            ug = u_scr[pb, :, cols]
            wsum = _window_sums(ug, w)[E:E + T]
            uc = ug[E:E + T]
            y_mid = wsum * (1.0 / w) - uc
            parts = []
            for r0 in (0, T - E):
                t_glob = tile_b * T + r0 + lax.broadcasted_iota(jnp.int32, (E, LANES), 0)
                cnt = (jnp.minimum(t_glob + right + 1, seq) - jnp.maximum(t_glob - left, 0)).astype(f32)
                parts.append(wsum[r0:r0 + E] / cnt - uc[r0:r0 + E])
            return jnp.concatenate([parts[0], y_mid[E:T - E], parts[1]], axis=0).astype(bf16)

        def b_pool_pair(pr):
            cols = slice(pr * CHUNK, (pr + 1) * CHUNK)
            y2 = jnp.concatenate([b_pool_y(2 * pr), b_pool_y(2 * pr + 1)], axis=1)
            yw = jnp.dot(y2, pw2_scr[pr], preferred_element_type=f32)
            pz = (yw * pscale_ref[:, cols]) * sgp_scr[pb, :, cols]
            mix_scr[pb, :, D_ATTN + pr * CHUNK:D_ATTN + (pr + 1) * CHUNK] = pz.astype(bf16)

        half_cols = D // 2
        mixed_halves = []

        def c_out(ch):
            m = jnp.dot(mix_scr[pa], wout_scr[:, ch * half_cols:(ch + 1) * half_cols], preferred_element_type=f32)
            mixed_halves.append((m, jnp.sum(m * m, axis=-1, keepdims=True)))

        def c_post(rb):
            rows = slice(rb * BLOCK, (rb + 1) * BLOCK)
            mixed = jnp.concatenate([m[rows] for m, _ in mixed_halves], axis=1)
            ssq = mixed_halves[0][1][rows]
            for _, part_ssq in mixed_halves[1:]:
                ssq = ssq + part_ssq[rows]
            y = (mixed * lax.rsqrt(ssq * (1.0 / D) + EPS)) * gpost_ref[...]
            out_ref[rows, :] = xres_scr[pa, rows, :] + y

        PN = BLOCK // 2

        def prenorm(r8):
            rows = slice(r8 * PN, (r8 + 1) * PN)
            xv = xa_ref[rows, :]
            ms = jnp.mean(xv * xv, axis=-1, keepdims=True)
            xn_scr[rows, :] = (xv * lax.rsqrt(ms + EPS)).astype(bf16)

        part = functools.partial
        out_q = lambda ch: ("c", part(c_out, ch))
        post = lambda rb: ("c", part(c_post, rb))
        ga = lambda c: ("a", part(a_gate, GA0, sga_scr, c))
        gp = lambda c: ("a", part(a_gate, GP0, sgp_scr, c))
        uu = lambda c: ("a", part(a_u, c))
        qq = lambda c: ("a", part(a_q, c))
        kv = ("a", a_kv)
        pn = lambda r8: ("a", part(prenorm, r8))
        heavy = [[out_q(0), None, None],
                 [out_q(1), None, None],
                 [kv, uu(0), None],
                 [uu(1), qq(0), None],
                 [qq(1), ga(0), None],
                 [ga(1), None, None],
                 [gp(0), None, None],
                 [gp(1), None, None]]
        assert len(heavy) == len(units)
        slot_end = {0: [pn(4), pn(5), pn(6), pn(7)],
                    1: [post(0), post(1)],
                    2: [post(2), post(3)],
                    3: [("b", part(b_pool_pair, 0))],
                    4: [("b", part(b_pool_pair, 1))]}

        def run(piece):
            if piece is not None and enabled[piece[0]]:
                piece[1]()

        if do_a:
            for r8 in range(T // PN // 2):
                prenorm(r8)
        elif do_b:
            u_scr[pb, POOL_HALO + T:, :] = jnp.zeros((POOL_HALO, D_POOL), f32)
        s = {0: b_scores(0)} if do_b else {}
        for n in range(len(units)):
            if do_b and n + 1 < len(units):
                s[n + 1] = b_scores(n + 1)
            pd, sink_term = {}, {}
            run(heavy[n][0])
            if do_b:
                b_soft(n, s[n], 0, pd, sink_term)
            run(heavy[n][1])
            if do_b:
                b_soft(n, s.pop(n), 1, pd, sink_term)
            run(heavy[n][2])
            if do_b:
                b_pv(n, pd, sink_term)
            for piece in slot_end.get(n, ()):
                run(piece)

        if do_a:
            k_scr[pb, :, 0:BLOCK, :] = k_scr[pa, :, T:T + BLOCK, :]
            v_scr[pb, :, 0:2 * BLOCK, :] = v_scr[pa, :, R * 2 * BLOCK:(R + 1) * 2 * BLOCK, :]
            u_scr[pb, 0:POOL_HALO, :] = u_scr[pa, T:T + POOL_HALO, :]

    steady = (i >= 2) & (i < n_tiles)
    for pa in range(2):
        @pl.when(steady & (i % 2 == pa))
        def _(pa=pa):
            run_step(pa, 1 - pa)

    def ramp(step, before=None, **stages):
        @pl.when(i == step)
        def _():
            if before is not None:
                before()
            run_step(step % 2, 1 - step % 2, **stages)

    ramp(0, do_b=False, do_c=False)
    ramp(1, before=convert_wout, do_c=False)
    ramp(n_tiles, do_a=False)
    ramp(n_tiles + 1, do_a=False, do_b=False)


def kernel(x, pre_norm_g, w_in, rel_bias, attn_sink, pool_w, pool_scale, w_out, post_norm_g):
    B, S, D = x.shape
    assert B == 1 and D == D_MODEL and S % TILE == 0 and S // TILE >= 3
    assert pre_norm_g.shape == (1, D) and attn_sink.shape == (1, N_Q_HEADS), "single layer"
    assert rel_bias.shape == (N_BUCKETS, N_Q_HEADS) and rel_bias.dtype == attn_sink.dtype == jnp.float32
    T = TILE
    R = T // BLOCK
    n_tiles = S // T
    bf16 = jnp.bfloat16

    x2 = x.reshape(S, D)
    smem = pl.BlockSpec(memory_space=pltpu.SMEM)
    hbm = pl.BlockSpec(memory_space=pltpu.HBM)
    full = lambda shape: pl.BlockSpec(shape, lambda i: (0,) * len(shape))

    out = pl.pallas_call(
        functools.partial(_layer_kernel, n_tiles=n_tiles),
        grid=(n_tiles + 2,),
        in_specs=[
            smem, smem,
            pl.BlockSpec((T, D), lambda i: (jnp.minimum(i, n_tiles - 1), 0)),
            hbm,
            full((1, D)),
            hbm,
            full((len(POOL_WINDOWS), POOL_GROUP, POOL_GROUP)),
            full((1, D_POOL)),
            hbm,
            full((1, D)),
        ],
        out_specs=pl.BlockSpec((T, D), lambda i: (jnp.maximum(i - 2, 0), 0)),
        out_shape=jax.ShapeDtypeStruct((S, D), x.dtype),
        scratch_shapes=[
            pltpu.VMEM((3, N_Q_HEADS, BLOCK, 3 * BLOCK), jnp.float32),
            pltpu.VMEM((2, T, D_ATTN), bf16),
            pltpu.VMEM((2, 4, T + 2 * BLOCK, LANES), bf16),
            pltpu.VMEM((2, N_KV_HEADS, (R + 2) * 2 * BLOCK, LANES), bf16),
            pltpu.VMEM((2, T + 2 * POOL_HALO, D_POOL), jnp.float32),
            pltpu.VMEM((2, T, D_ATTN), jnp.float32),
            pltpu.VMEM((2, T, D_POOL), jnp.float32),
            pltpu.VMEM((2, T, D), bf16),
            pltpu.VMEM((T, D), bf16),
            pltpu.VMEM((len(POOL_WINDOWS) // 2, CHUNK, CHUNK), bf16),
            pltpu.VMEM((3 * 2 * BLOCK, LANES), bf16),
            pltpu.VMEM((D, D), bf16),
            pltpu.VMEM((D, D_IN), bf16),
            pltpu.VMEM((D, D_IN), jnp.float32),
            pltpu.VMEM((D, D), jnp.float32),
            pltpu.SemaphoreType.DMA((2 * (D // LANES) * W_COPIES_PER_CHUNK,)),
            pltpu.VMEM((2, T, D), jnp.float32),
            pltpu.SemaphoreType.DMA((2,)),
        ],
        compiler_params=pltpu.CompilerParams(
            dimension_semantics=("arbitrary",),
            vmem_limit_bytes=VMEM_LIMIT_BYTES,
        ),
        name="hymba_layer_fused",
    )(
        rel_bias.T, attn_sink,
        x2, x2,
        pre_norm_g,
        w_in.reshape(D, D_IN),
        pool_w.reshape(len(POOL_WINDOWS), POOL_GROUP, POOL_GROUP),
        pool_scale,
        w_out.reshape(D, D),
        post_norm_g,
    )
    return out.reshape(B, S, D)
```
